```python
import math
import jax
import jax.numpy as jnp
from jax import lax
import numpy as np

D_MODEL = 1024
BATCH = 16
SEQ = 4096
DEPTH = 4

GRID_W = 64
CTX_LEN = 256

HY_CH = 256
RW_HEADS = 4
RW_HD = 64
RW_W = RW_HEADS * RW_HD
AT_HQ = 8
AT_HKV = 2
AT_G = AT_HQ // AT_HKV
AT_HD = 64
AT_W = AT_HQ * AT_HD
AT_KVW = AT_HKV * AT_HD
MIX_W = HY_CH + RW_W + AT_W

RW_DECAY_R = 64
RW_ICLR_R = 64
RW_GATE_R = 128
RW_LORA_W = 2 * RW_DECAY_R + 2 * RW_ICLR_R + RW_GATE_R
RW_GN_EPS = RW_HD * 1e-5

HY_EMB = 33
HY_BANDS = (HY_EMB - 1) // 2
HY_HID = 64
HY_SHIFT = 0.05
HY_MIN_DECAY = math.log(1e-2) / 1.5
HY_MAX_DECAY = math.log(1e-2) / 0.3

ATT_WINDOW = 128
ATT_BLOCK = 128
ROPE_BASE = 10000.0
NEG_INF = -1e30

PEER_HEADS = 8
PEER_NKEYS = 128
PEER_NEXP = PEER_NKEYS * PEER_NKEYS
PEER_DQ = 256
PEER_TOPK = 16
PEER_CHUNK = 128

OFF_HY = 0
OFF_RK = OFF_HY + 3 * HY_CH
OFF_LORA = OFF_RK + 3 * RW_W
OFF_AT = OFF_LORA + RW_LORA_W
IN_W = OFF_AT + AT_W + 2 * AT_KVW
CONV_COLS = OFF_LORA

kernel_name = 'hybrid_hyena_rwkv7_swa_peer_dit'


def rmsnorm(x, g, eps=1e-6):
    xf = x.astype(jnp.float32)
    y = xf * lax.rsqrt(jnp.mean(xf * xf, axis=-1, keepdims=True) + eps)
    return (y * g.astype(jnp.float32)).astype(x.dtype)


def short_conv3(u, w, b):
    up = jnp.pad(u, ((0, 0), (1, 1), (0, 0)))
    return up[:, :-2] * w[0] + up[:, 1:-1] * w[1] + up[:, 2:] * w[2] + b


def hyena_filter(L, w1, b1, w2, b2, w3, b3, freq):
    f32 = jnp.float32
    j = jnp.arange(L, dtype=f32)
    t = j / max(L - 1, 1)
    bands = jnp.linspace(1e-4, HY_BANDS - 1, HY_BANDS, dtype=f32)
    ang = 2.0 * math.pi * j[:, None] * bands[None, :] / L
    z = jnp.concatenate([t[:, None], jnp.cos(ang), -jnp.sin(ang)], axis=-1)
    h = jnp.sin(freq[0] * (z @ w1 + b1))
    h = jnp.sin(freq[1] * (h @ w2 + b2))
    h = (h @ w3 + b3).astype(f32).reshape(L, 2, HY_CH)
    deltas = jnp.abs(jnp.linspace(HY_MIN_DECAY, HY_MAX_DECAY, HY_CH, dtype=f32))
    window = jnp.exp(-t[:, None] * deltas[None, :]) + HY_SHIFT
    h = h * window[:, None, :]
    k = jnp.concatenate([h[:, 0], jnp.zeros((1, HY_CH), f32), h[:0:-1, 1]], axis=0)
    return k * lax.rsqrt(jnp.sum(k * k, axis=0, keepdims=True) + 1e-6)


def hyena_mix(z, filt, d_bias):
    f32 = jnp.float32
    L = z.shape[1]
    x0 = z[..., :HY_CH].astype(f32)
    x1 = z[..., HY_CH:2 * HY_CH].astype(f32)
    v = z[..., 2 * HY_CH:].astype(f32)
    u = x1 * v
    n = 2 * L
    y = jnp.fft.irfft(jnp.fft.rfft(u, n=n, axis=1) * jnp.fft.rfft(filt, n=n, axis=0)[None], n=n, axis=1)[:, :L]
    return (x0 * (y + u * d_bias)).astype(z.dtype)


def rwkv7_scan(r, w, k, v, a, b):
    B, L, H, N = r.shape

    def step(S, inp):
        r_t, w_t, k_t, v_t, a_t, b_t = inp
        sa = jnp.einsum('bhvk,bhk->bhv', S, a_t)
        S = S * w_t[:, :, None, :] + sa[..., None] * b_t[:, :, None, :] + v_t[..., None] * k_t[:, :, None, :]
        return S, jnp.einsum('bhvk,bhk->bhv', S, r_t)

    S0 = jnp.zeros((B, H, N, N), jnp.float32)
    xs = tuple(jnp.swapaxes(s, 0, 1) for s in (r, w, k, v, a, b))
    _, ys = lax.scan(step, S0, xs)
    return jnp.swapaxes(ys, 0, 1)


def rwkv7_mix(rkv_c, rkv_l, lora_c, lora_l, lp):
    f32 = jnp.float32
    B, Lc, _ = rkv_c.shape
    T = rkv_l.shape[1]
    Lt = Lc + T
    hs = (B, Lt, RW_HEADS, RW_HD)
    rkv = jnp.concatenate([rkv_c, rkv_l], axis=1).astype(f32)
    lora = jnp.concatenate([lora_c, lora_l], axis=1).astype(f32)
    r = rkv[..., :RW_W].reshape(hs)
    k = rkv[..., RW_W:2 * RW_W].reshape(hs)
    v = rkv[..., 2 * RW_W:].reshape(hs)
    kk = k * lp['rw_k_k'].astype(f32).reshape(RW_HEADS, RW_HD)
    kk = kk * lax.rsqrt(jnp.sum(kk * kk, axis=-1, keepdims=True) + 1e-12)
    k_a = lp['rw_k_a'].astype(f32).reshape(RW_HEADS, RW_HD)
    r_k = lp['rw_r_k'].astype(f32)
    perm = jnp.concatenate([jnp.arange(Lc - 1, -1, -1), Lc + jnp.arange(T - 1, -1, -1)])
    y_sum = jnp.zeros(hs, f32)
    bonus = jnp.zeros(hs, f32)
    for d in range(2):
        wd = lora[..., d * RW_DECAY_R:(d + 1) * RW_DECAY_R]
        ad = lora[..., 2 * RW_DECAY_R + d * RW_ICLR_R:2 * RW_DECAY_R + (d + 1) * RW_ICLR_R]
        w_log = -jax.nn.softplus(-(lp['rw_w0'][d] + jnp.tanh(wd) @ lp['rw_w_up'][d])) - 0.5
        decay = jnp.exp(-jnp.exp(w_log.astype(f32))).reshape(hs)
        a = jax.nn.sigmoid((lp['rw_a0'][d] + ad @ lp['rw_a_up'][d]).astype(f32)).reshape(hs)
        k_d = k * (1.0 + (a - 1.0) * k_a)
        seqs = (r, decay, k_d, v, -kk, kk * a)
        if d == 1:
            seqs = tuple(s[:, perm] for s in seqs)
        y = rwkv7_scan(*seqs)
        if d == 1:
            y = y[:, perm]
        y_sum = y_sum + y
        bonus = bonus + jnp.sum(r * k_d * r_k, axis=-1, keepdims=True) * v
    mu = jnp.mean(y_sum, axis=-1, keepdims=True)
    var = jnp.mean(jnp.square(y_sum - mu), axis=-1, keepdims=True)
    yn = (y_sum - mu) * lax.rsqrt(var + RW_GN_EPS)
    yn = yn * lp['rw_ln_w'].astype(f32).reshape(RW_HEADS, RW_HD) + lp['rw_ln_b'].astype(f32).reshape(RW_HEADS, RW_HD)
    g = jax.nn.sigmoid(lora[..., 2 * RW_DECAY_R + 2 * RW_ICLR_R:]) @ lp['rw_g_up']
    out = ((yn + bonus).reshape(B, Lt, RW_W) * g).astype(rkv_l.dtype)
    return out[:, :Lc], out[:, Lc:]


def axial_rope(x):
    f32 = jnp.float32
    T = x.shape[1]
    rows = T // GRID_W
    row = jnp.repeat(jnp.arange(rows), GRID_W).astype(f32)
    col = jnp.tile(jnp.arange(GRID_W), rows).astype(f32)
    half = AT_HD // 2
    quarter = half // 2
    inv = ROPE_BASE ** (-jnp.arange(quarter, dtype=f32) / quarter)

    def rot(xa, pos):
        ang = pos[:, None] * inv[None, :]
        cos = jnp.cos(ang)[None, :, None, :]
        sin = jnp.sin(ang)[None, :, None, :]
        x1, x2 = xa[..., :quarter], xa[..., quarter:]
        return jnp.concatenate([x1 * cos - x2 * sin, x2 * cos + x1 * sin], axis=-1)

    xf = x.astype(f32)
    return jnp.concatenate([rot(xf[..., :half], row), rot(xf[..., half:], col)], axis=-1).astype(x.dtype)


def split_attn(z):
    B, L, _ = z.shape
    a = z[..., OFF_AT:]
    q = a[..., :AT_W].reshape(B, L, AT_HQ, AT_HD)
    k = a[..., AT_W:AT_W + AT_KVW].reshape(B, L, AT_HKV, AT_HD)
    v = a[..., AT_W + AT_KVW:].reshape(B, L, AT_HKV, AT_HD)
    return q, k, v


def windowed_gqa(q, k, v, kc, vc, sink):
    B, T = q.shape[:2]
    Lc = kc.shape[1]
    nblk = T // ATT_BLOCK
    span = ATT_BLOCK + 2 * ATT_WINDOW
    scale = AT_HD ** -0.5
    kp = jnp.pad(k, ((0, 0), (ATT_WINDOW, ATT_WINDOW), (0, 0), (0, 0)))
    vp = jnp.pad(v, ((0, 0), (ATT_WINDOW, ATT_WINDOW), (0, 0), (0, 0)))
    sink_b = jnp.broadcast_to(sink.astype(jnp.float32)[None, :, :, None, None], (B, AT_HKV, AT_G, ATT_BLOCK, 1))

    def block(i):
        s0 = i * ATT_BLOCK
        qb = lax.dynamic_slice_in_dim(q, s0, ATT_BLOCK, axis=1)
        kb = lax.dynamic_slice_in_dim(kp, s0, span, axis=1)
        vb = lax.dynamic_slice_in_dim(vp, s0, span, axis=1)
        qpos = s0 + jnp.arange(ATT_BLOCK)
        kpos = s0 - ATT_WINDOW + jnp.arange(span)
        valid = (jnp.abs(qpos[:, None] - kpos[None, :]) <= ATT_WINDOW) & (kpos >= 0)[None, :] & (kpos < T)[None, :]
        s_lat = jnp.einsum('bqgjd,bkgd->bgjqk', qb, kb).astype(jnp.float32) * scale
        s_lat = jnp.where(valid, s_lat, NEG_INF)
        s_ctx = jnp.einsum('bqgjd,bkgd->bgjqk', qb, kc).astype(jnp.float32) * scale
        p = jax.nn.softmax(jnp.concatenate([s_lat, s_ctx, sink_b], axis=-1), axis=-1).astype(v.dtype)
        return (jnp.einsum('bgjqk,bkgd->bqgjd', p[..., :span], vb)
                + jnp.einsum('bgjqk,bkgd->bqgjd', p[..., span:span + Lc], vc))

    o = lax.map(block, jnp.arange(nblk))
    return jnp.swapaxes(o, 0, 1).reshape(B, T, AT_W)


def context_attention(qc, kc, vc, sink):
    B, Lc = qc.shape[:2]
    scale = AT_HD ** -0.5
    s = jnp.einsum('bqgjd,bkgd->bgjqk', qc, kc).astype(jnp.float32) * scale
    sink_b = jnp.broadcast_to(sink.astype(jnp.float32)[None, :, :, None, None], (B, AT_HKV, AT_G, Lc, 1))
    p = jax.nn.softmax(jnp.concatenate([s, sink_b], axis=-1), axis=-1)[..., :Lc].astype(vc.dtype)
    return jnp.einsum('bgjqk,bkgd->bqgjd', p, vc).reshape(B, Lc, AT_W)


def token_mixer(hl, hc, lp, need_ctx):
    B, T, _ = hl.shape
    Lc = hc.shape[1]
    zl = hl @ lp['w_in']
    zc = hc @ lp['w_in']
    sl = short_conv3(zl[..., :CONV_COLS], lp['conv_w'], lp['conv_b'])
    sc = short_conv3(zc[..., :CONV_COLS], lp['conv_w'], lp['conv_b'])
    filt_p = (lp['hy_w1'], lp['hy_b1'], lp['hy_w2'], lp['hy_b2'], lp['hy_w3'], lp['hy_b3'], lp['hy_freq'])
    hy_l = hyena_mix(sl[..., OFF_HY:OFF_RK], hyena_filter(T, *filt_p), lp['hy_bias'])
    rw_c, rw_l = rwkv7_mix(sc[..., OFF_RK:], sl[..., OFF_RK:], zc[..., OFF_LORA:OFF_AT], zl[..., OFF_LORA:OFF_AT], lp)
    ql, kl, vl = split_attn(zl)
    qc, kc, vc = split_attn(zc)
    sink = lp['attn_sink'].reshape(AT_HKV, AT_G)
    ql = axial_rope(ql).reshape(B, T, AT_HKV, AT_G, AT_HD)
    kl = axial_rope(kl)
    at_l = windowed_gqa(ql, kl, vl, kc, vc, sink)
    out_l = jnp.concatenate([rmsnorm(hy_l, lp['hy_out_g']), rw_l, rmsnorm(at_l, lp['at_out_g'])], axis=-1) @ lp['w_out']
    if not need_ctx:
        return out_l, None
    hy_c = hyena_mix(sc[..., OFF_HY:OFF_RK], hyena_filter(Lc, *filt_p), lp['hy_bias'])
    at_c = context_attention(qc.reshape(B, Lc, AT_HKV, AT_G, AT_HD), kc, vc, sink)
    out_c = jnp.concatenate([rmsnorm(hy_c, lp['hy_out_g']), rw_c, rmsnorm(at_c, lp['at_out_g'])], axis=-1) @ lp['w_out']
    return out_l, out_c


def peer_ffn(h, wq, k1, k2, u, v):
    n, d = h.shape
    hb = h.reshape(n // PEER_CHUNK, PEER_CHUNK, d)

    def chunk(xc):
        q = (xc @ wq).reshape(PEER_CHUNK, PEER_HEADS, 2, PEER_DQ // 2)
        s1 = jnp.einsum('chd,hnd->chn', q[:, :, 0], k1).astype(jnp.float32)
        s2 = jnp.einsum('chd,hnd->chn', q[:, :, 1], k2).astype(jnp.float32)
        v1, i1 = lax.top_k(s1, PEER_TOPK)
        v2, i2 = lax.top_k(s2, PEER_TOPK)
        cand = (v1[..., :, None] + v2[..., None, :]).reshape(PEER_CHUNK, PEER_HEADS, PEER_TOPK * PEER_TOPK)
        cidx = (i1[..., :, None] * PEER_NKEYS + i2[..., None, :]).reshape(PEER_CHUNK, PEER_HEADS, PEER_TOPK * PEER_TOPK)
        sc, pos = lax.top_k(cand, PEER_TOPK)
        eidx = jnp.take_along_axis(cidx, pos, axis=-1)
        gate = jax.nn.softmax(sc, axis=-1)
        act = jax.nn.gelu(jnp.einsum('cd,chkd->chk', xc, u[eidx]).astype(jnp.float32), approximate=False)
        return jnp.einsum('chk,chkd->cd', (gate * act).astype(xc.dtype), v[eidx])

    return lax.map(chunk, hb).reshape(n, d)


def setup_inputs(seed: int = 0) -> dict:
    key = jax.random.key(seed)
    ks = iter(jax.random.split(key, 48))
    f32 = jnp.float32

    def nrm(shape, scale):
        return jax.random.normal(next(ks), shape, f32) * scale

    def gain(shape):
        return 1.0 + nrm(shape, 0.05)

    D = D_MODEL
    return {
        'x': nrm((BATCH, SEQ, D), 1.0),
        'c': nrm((BATCH, D), 1.0),
        'ctx': nrm((BATCH, CTX_LEN, D), 1.0),
        'c_ctx': nrm((D,), 1.0),
        'mod_w': nrm((DEPTH, D, 6 * D), 0.01),
        'mod_b': nrm((DEPTH, 6 * D), 0.01),
        'norm1_g': gain((DEPTH, D)),
        'norm2_g': gain((DEPTH, D)),
        'w_in': nrm((DEPTH, D, IN_W), D ** -0.5),
        'conv_w': nrm((DEPTH, 3, CONV_COLS), 0.6),
        'conv_b': nrm((DEPTH, CONV_COLS), 0.01),
        'hy_w1': nrm((DEPTH, HY_EMB, HY_HID), HY_EMB ** -0.5),
        'hy_b1': nrm((DEPTH, HY_HID), 0.1),
        'hy_w2': nrm((DEPTH, HY_HID, HY_HID), HY_HID ** -0.5),
        'hy_b2': nrm((DEPTH, HY_HID), 0.1),
        'hy_w3': nrm((DEPTH, HY_HID, 2 * HY_CH), HY_HID ** -0.5),
        'hy_b3': nrm((DEPTH, 2 * HY_CH), 0.01),
        'hy_freq': 1.0 + nrm((DEPTH, 2, HY_HID), 0.1),
        'hy_bias': nrm((DEPTH, HY_CH), 0.5),
        'rw_w0': jax.random.uniform(next(ks), (DEPTH, 2, RW_W), f32, -6.0, 0.5),
        'rw_w_up': nrm((DEPTH, 2, RW_DECAY_R, RW_W), 0.1 * RW_DECAY_R ** -0.5),
        'rw_a0': nrm((DEPTH, 2, RW_W), 0.1),
        'rw_a_up': nrm((DEPTH, 2, RW_ICLR_R, RW_W), RW_ICLR_R ** -0.5),
        'rw_g_up': nrm((DEPTH, RW_GATE_R, RW_W), RW_GATE_R ** -0.5),
        'rw_k_k': 0.85 + nrm((DEPTH, RW_W), 0.05),
        'rw_k_a': gain((DEPTH, RW_W)),
        'rw_r_k': nrm((DEPTH, RW_HEADS, RW_HD), 0.1),
        'rw_ln_w': gain((DEPTH, RW_W)),
        'rw_ln_b': nrm((DEPTH, RW_W), 0.01),
        'attn_sink': nrm((DEPTH, AT_HQ), 0.5),
        'hy_out_g': gain((DEPTH, HY_CH)),
        'at_out_g': gain((DEPTH, AT_W)),
        'w_out': nrm((DEPTH, MIX_W, D), MIX_W ** -0.5),
        'peer_wq': nrm((DEPTH, D, PEER_HEADS * PEER_DQ), D ** -0.5),
        'peer_k1': nrm((DEPTH, PEER_HEADS, PEER_NKEYS, PEER_DQ // 2), (PEER_DQ // 2) ** -0.5),
        'peer_k2': nrm((DEPTH, PEER_HEADS, PEER_NKEYS, PEER_DQ // 2), (PEER_DQ // 2) ** -0.5),
        'peer_u': nrm((DEPTH, PEER_NEXP, D), D ** -0.5),
        'peer_v': nrm((DEPTH, PEER_NEXP, D), 1.0),
        'final_g': gain((D,)),
    }


def reference(x, c, ctx, c_ctx, mod_w, mod_b, norm1_g, norm2_g, w_in, conv_w, conv_b,
              hy_w1, hy_b1, hy_w2, hy_b2, hy_w3, hy_b3, hy_freq, hy_bias,
              rw_w0, rw_w_up, rw_a0, rw_a_up, rw_g_up, rw_k_k, rw_k_a, rw_r_k, rw_ln_w, rw_ln_b,
              attn_sink, hy_out_g, at_out_g, w_out,
              peer_wq, peer_k1, peer_k2, peer_u, peer_v, final_g):
    B, T, D = x.shape
    Lc = ctx.shape[1]
    c_s = jax.nn.silu(c)
    cc_s = jax.nn.silu(c_ctx)
    xl, xc = x, ctx
    for i in range(DEPTH):
        last = i == DEPTH - 1
        lp = dict(w_in=w_in[i], conv_w=conv_w[i], conv_b=conv_b[i],
                  hy_w1=hy_w1[i], hy_b1=hy_b1[i], hy_w2=hy_w2[i], hy_b2=hy_b2[i],
                  hy_w3=hy_w3[i], hy_b3=hy_b3[i], hy_freq=hy_freq[i], hy_bias=hy_bias[i],
                  rw_w0=rw_w0[i], rw_w_up=rw_w_up[i], rw_a0=rw_a0[i], rw_a_up=rw_a_up[i],
                  rw_g_up=rw_g_up[i], rw_k_k=rw_k_k[i], rw_k_a=rw_k_a[i], rw_r_k=rw_r_k[i],
                  rw_ln_w=rw_ln_w[i], rw_ln_b=rw_ln_b[i], attn_sink=attn_sink[i],
                  hy_out_g=hy_out_g[i], at_out_g=at_out_g[i], w_out=w_out[i])
        sh1, sc1, g1, sh2, sc2, g2 = jnp.split((c_s @ mod_w[i] + mod_b[i])[:, None, :], 6, axis=-1)
        csh1, csc1, cg1, csh2, csc2, cg2 = jnp.split(cc_s @ mod_w[i] + mod_b[i], 6, axis=-1)
        hl = rmsnorm(xl, norm1_g[i]) * (1.0 + sc1) + sh1
        hc = rmsnorm(xc, norm1_g[i]) * (1.0 + csc1) + csh1
        ml, mc = token_mixer(hl, hc, lp, not last)
        xl = xl + g1 * ml
        h2l = rmsnorm(xl, norm2_g[i]) * (1.0 + sc2) + sh2
        if last:
            fl = peer_ffn(h2l.reshape(B * T, D), peer_wq[i], peer_k1[i], peer_k2[i], peer_u[i], peer_v[i]).reshape(B, T, D)
        else:
            xc = xc + cg1 * mc
            h2c = rmsnorm(xc, norm2_g[i]) * (1.0 + csc2) + csh2
            tok = jnp.concatenate([h2l.reshape(B * T, D), h2c.reshape(B * Lc, D)], axis=0)
            f = peer_ffn(tok, peer_wq[i], peer_k1[i], peer_k2[i], peer_u[i], peer_v[i])
            fl = f[:B * T].reshape(B, T, D)
            xc = xc + cg2 * f[B * T:].reshape(B, Lc, D)
        xl = xl + g2 * fl
    return rmsnorm(xl, final_g)
```

```python
import functools
import math

import jax
import jax.numpy as jnp
from jax import lax
from jax.experimental import pallas as pl
from jax.experimental.pallas import tpu as pltpu

D_MODEL = 1024
DEPTH = 4
GRID_W = 64

HY_CH = 256
RW_HEADS = 4
RW_HD = 64
RW_W = RW_HEADS * RW_HD
AT_HQ = 8
AT_HKV = 2
AT_G = AT_HQ // AT_HKV
AT_HD = 64
AT_W = AT_HQ * AT_HD
AT_KVW = AT_HKV * AT_HD

RW_DECAY_R = 64
RW_ICLR_R = 64
RW_GATE_R = 128
RW_LORA_W = 2 * RW_DECAY_R + 2 * RW_ICLR_R + RW_GATE_R
RW_GN_EPS = RW_HD * 1e-5

HY_EMB = 33
HY_BANDS = (HY_EMB - 1) // 2
HY_SHIFT = 0.05
HY_MIN_DECAY = math.log(1e-2) / 1.5
HY_MAX_DECAY = math.log(1e-2) / 0.3

ATT_WINDOW = 128
ATT_BLOCK = 128
ROPE_BASE = 10000.0
NEG_INF = -1e30

PEER_HEADS = 8
PEER_NKEYS = 128
PEER_DQ = 256
PEER_TOPK = 16
PEER_CHUNK = 128

OFF_HY = 0
OFF_RK = OFF_HY + 3 * HY_CH
OFF_LORA = OFF_RK + 3 * RW_W
OFF_AT = OFF_LORA + RW_LORA_W
CONV_COLS = OFF_LORA

VMEM_LIMIT = 48 * 1024 * 1024


def _mm_kernel(x_ref, w_ref, o_ref):
    o_ref[...] = jnp.dot(x_ref[...].astype(jnp.bfloat16), w_ref[...],
                         preferred_element_type=jnp.float32)


def matmul(x, w, tm=512):
    m, k = x.shape
    n = w.shape[1]
    assert m % tm == 0
    return pl.pallas_call(
        _mm_kernel,
        grid=(m // tm,),
        in_specs=[pl.BlockSpec((tm, k), lambda i: (i, 0)),
                  pl.BlockSpec((k, n), lambda i: (0, 0))],
        out_specs=pl.BlockSpec((tm, n), lambda i: (i, 0)),
        out_shape=jax.ShapeDtypeStruct((m, n), jnp.float32),
        compiler_params=pltpu.CompilerParams(dimension_semantics=("parallel",),
                                             vmem_limit_bytes=VMEM_LIMIT),
        name="matmul",
    )(x, w.astype(jnp.bfloat16))


def proj(x3, w):
    b, l, d = x3.shape
    return matmul(x3.reshape(b * l, d), w).reshape(b, l, w.shape[1])


def rmsnorm(x, g, eps=1e-6):
    y = x * lax.rsqrt(jnp.mean(x * x, axis=-1, keepdims=True) + eps)
    return y * g


def short_conv3(u, w, b):
    up = jnp.pad(u, ((0, 0), (1, 1), (0, 0)))
    return up[:, :-2] * w[0] + up[:, 1:-1] * w[1] + up[:, 2:] * w[2] + b


def hyena_filter(L, w1, b1, w2, b2, w3, b3, freq):
    f32 = jnp.float32
    j = jnp.arange(L, dtype=f32)
    t = j / max(L - 1, 1)
    bands = jnp.linspace(1e-4, HY_BANDS - 1, HY_BANDS, dtype=f32)
    ang = 2.0 * math.pi * j[:, None] * bands[None, :] / L
    z = jnp.concatenate([t[:, None], jnp.cos(ang), -jnp.sin(ang)], axis=-1)
    h = jnp.sin(freq[0] * (z @ w1 + b1))
    h = jnp.sin(freq[1] * (h @ w2 + b2))
    h = (h @ w3 + b3).astype(f32).reshape(L, 2, HY_CH)
    deltas = jnp.abs(jnp.linspace(HY_MIN_DECAY, HY_MAX_DECAY, HY_CH, dtype=f32))
    window = jnp.exp(-t[:, None] * deltas[None, :]) + HY_SHIFT
    h = h * window[:, None, :]
    k = jnp.concatenate([h[:, 0], jnp.zeros((1, HY_CH), f32), h[:0:-1, 1]], axis=0)
    return k * lax.rsqrt(jnp.sum(k * k, axis=0, keepdims=True) + 1e-6)


def hyena_mix(z, filt, d_bias):
    L = z.shape[1]
    x0 = z[..., :HY_CH]
    x1 = z[..., HY_CH:2 * HY_CH]
    v = z[..., 2 * HY_CH:]
    u = x1 * v
    n = 2 * L
    y = jnp.fft.irfft(jnp.fft.rfft(u, n=n, axis=1) * jnp.fft.rfft(filt, n=n, axis=0)[None], n=n, axis=1)[:, :L]
    return x0 * (y + u * d_bias)


def rwkv7_scan(r, w, k, v, a, b):
    B, L, H, N = r.shape

    def step(S, inp):
        r_t, w_t, k_t, v_t, a_t, b_t = inp
        sa = jnp.einsum('bhvk,bhk->bhv', S, a_t)
        S = S * w_t[:, :, None, :] + sa[..., None] * b_t[:, :, None, :] + v_t[..., None] * k_t[:, :, None, :]
        return S, jnp.einsum('bhvk,bhk->bhv', S, r_t)

    S0 = jnp.zeros((B, H, N, N), jnp.float32)
    xs = tuple(jnp.swapaxes(s, 0, 1) for s in (r, w, k, v, a, b))
    _, ys = lax.scan(step, S0, xs)
    return jnp.swapaxes(ys, 0, 1)


def rwkv7_mix(rkv_c, rkv_l, lora_c, lora_l, lp):
    f32 = jnp.float32
    B, Lc, _ = rkv_c.shape
    T = rkv_l.shape[1]
    Lt = Lc + T
    hs = (B, Lt, RW_HEADS, RW_HD)
    rkv = jnp.concatenate([rkv_c, rkv_l], axis=1)
    lora = jnp.concatenate([lora_c, lora_l], axis=1)
    r = rkv[..., :RW_W].reshape(hs)
    k = rkv[..., RW_W:2 * RW_W].reshape(hs)
    v = rkv[..., 2 * RW_W:].reshape(hs)
    kk = k * lp['rw_k_k'].reshape(RW_HEADS, RW_HD)
    kk = kk * lax.rsqrt(jnp.sum(kk * kk, axis=-1, keepdims=True) + 1e-12)
    k_a = lp['rw_k_a'].reshape(RW_HEADS, RW_HD)
    r_k = lp['rw_r_k']
    perm = jnp.concatenate([jnp.arange(Lc - 1, -1, -1), Lc + jnp.arange(T - 1, -1, -1)])
    y_sum = jnp.zeros(hs, f32)
    bonus = jnp.zeros(hs, f32)
    for d in range(2):
        wd = lora[..., d * RW_DECAY_R:(d + 1) * RW_DECAY_R]
        ad = lora[..., 2 * RW_DECAY_R + d * RW_ICLR_R:2 * RW_DECAY_R + (d + 1) * RW_ICLR_R]
        w_log = -jax.nn.softplus(-(lp['rw_w0'][d] + jnp.tanh(wd) @ lp['rw_w_up'][d])) - 0.5
        decay = jnp.exp(-jnp.exp(w_log)).reshape(hs)
        a = jax.nn.sigmoid(lp['rw_a0'][d] + ad @ lp['rw_a_up'][d]).reshape(hs)
        k_d = k * (1.0 + (a - 1.0) * k_a)
        seqs = (r, decay, k_d, v, -kk, kk * a)
        if d == 1:
            seqs = tuple(s[:, perm] for s in seqs)
        y = rwkv7_scan(*seqs)
        if d == 1:
            y = y[:, perm]
        y_sum = y_sum + y
        bonus = bonus + jnp.sum(r * k_d * r_k, axis=-1, keepdims=True) * v
    mu = jnp.mean(y_sum, axis=-1, keepdims=True)
    var = jnp.mean(jnp.square(y_sum - mu), axis=-1, keepdims=True)
    yn = (y_sum - mu) * lax.rsqrt(var + RW_GN_EPS)
    yn = yn * lp['rw_ln_w'].reshape(RW_HEADS, RW_HD) + lp['rw_ln_b'].reshape(RW_HEADS, RW_HD)
    g = jax.nn.sigmoid(lora[..., 2 * RW_DECAY_R + 2 * RW_ICLR_R:]) @ lp['rw_g_up']
    out = (yn + bonus).reshape(B, Lt, RW_W) * g
    return out[:, :Lc], out[:, Lc:]


def axial_rope(x):
    f32 = jnp.float32
    T = x.shape[1]
    rows = T // GRID_W
    row = jnp.repeat(jnp.arange(rows), GRID_W).astype(f32)
    col = jnp.tile(jnp.arange(GRID_W), rows).astype(f32)
    half = AT_HD // 2
    quarter = half // 2
    inv = ROPE_BASE ** (-jnp.arange(quarter, dtype=f32) / quarter)

    def rot(xa, pos):
        ang = pos[:, None] * inv[None, :]
        cos = jnp.cos(ang)[None, :, None, :]
        sin = jnp.sin(ang)[None, :, None, :]
        x1, x2 = xa[..., :quarter], xa[..., quarter:]
        return jnp.concatenate([x1 * cos - x2 * sin, x2 * cos + x1 * sin], axis=-1)

    return jnp.concatenate([rot(x[..., :half], row), rot(x[..., half:], col)], axis=-1)


def split_attn(z):
    B, L, _ = z.shape
    a = z[..., OFF_AT:]
    q = a[..., :AT_W].reshape(B, L, AT_HQ, AT_HD)
    k = a[..., AT_W:AT_W + AT_KVW].reshape(B, L, AT_HKV, AT_HD)
    v = a[..., AT_W + AT_KVW:].reshape(B, L, AT_HKV, AT_HD)
    return q, k, v


def windowed_gqa(q, k, v, kc, vc, sink):
    B, T = q.shape[:2]
    Lc = kc.shape[1]
    nblk = T // ATT_BLOCK
    span = ATT_BLOCK + 2 * ATT_WINDOW
    scale = AT_HD ** -0.5
    kp = jnp.pad(k, ((0, 0), (ATT_WINDOW, ATT_WINDOW), (0, 0), (0, 0)))
    vp = jnp.pad(v, ((0, 0), (ATT_WINDOW, ATT_WINDOW), (0, 0), (0, 0)))
    sink_b = jnp.broadcast_to(sink[None, :, :, None, None], (B, AT_HKV, AT_G, ATT_BLOCK, 1))

    def block(i):
        s0 = i * ATT_BLOCK
        qb = lax.dynamic_slice_in_dim(q, s0, ATT_BLOCK, axis=1)
        kb = lax.dynamic_slice_in_dim(kp, s0, span, axis=1)
        vb = lax.dynamic_slice_in_dim(vp, s0, span, axis=1)
        qpos = s0 + jnp.arange(ATT_BLOCK)
        kpos = s0 - ATT_WINDOW + jnp.arange(span)
        valid = (jnp.abs(qpos[:, None] - kpos[None, :]) <= ATT_WINDOW) & (kpos >= 0)[None, :] & (kpos < T)[None, :]
        s_lat = jnp.einsum('bqgjd,bkgd->bgjqk', qb, kb) * scale
        s_lat = jnp.where(valid, s_lat, NEG_INF)
        s_ctx = jnp.einsum('bqgjd,bkgd->bgjqk', qb, kc) * scale
        p = jax.nn.softmax(jnp.concatenate([s_lat, s_ctx, sink_b], axis=-1), axis=-1)
        return (jnp.einsum('bgjqk,bkgd->bqgjd', p[..., :span], vb)
                + jnp.einsum('bgjqk,bkgd->bqgjd', p[..., span:span + Lc], vc))

    o = lax.map(block, jnp.arange(nblk))
    return jnp.swapaxes(o, 0, 1).reshape(B, T, AT_W)


def context_attention(qc, kc, vc, sink):
    B, Lc = qc.shape[:2]
    scale = AT_HD ** -0.5
    s = jnp.einsum('bqgjd,bkgd->bgjqk', qc, kc) * scale
    sink_b = jnp.broadcast_to(sink[None, :, :, None, None], (B, AT_HKV, AT_G, Lc, 1))
    p = jax.nn.softmax(jnp.concatenate([s, sink_b], axis=-1), axis=-1)[..., :Lc]
    return jnp.einsum('bgjqk,bkgd->bqgjd', p, vc).reshape(B, Lc, AT_W)


def token_mixer(hl, hc, lp, need_ctx):
    B, T, _ = hl.shape
    Lc = hc.shape[1]
    zl = proj(hl, lp['w_in'])
    zc = proj(hc, lp['w_in'])
    sl = short_conv3(zl[..., :CONV_COLS], lp['conv_w'], lp['conv_b'])
    sc = short_conv3(zc[..., :CONV_COLS], lp['conv_w'], lp['conv_b'])
    filt_p = (lp['hy_w1'], lp['hy_b1'], lp['hy_w2'], lp['hy_b2'], lp['hy_w3'], lp['hy_b3'], lp['hy_freq'])
    hy_l = hyena_mix(sl[..., OFF_HY:OFF_RK], hyena_filter(T, *filt_p), lp['hy_bias'])
    rw_c, rw_l = rwkv7_mix(sc[..., OFF_RK:], sl[..., OFF_RK:], zc[..., OFF_LORA:OFF_AT], zl[..., OFF_LORA:OFF_AT], lp)
    ql, kl, vl = split_attn(zl)
    qc, kc, vc = split_attn(zc)
    sink = lp['attn_sink'].reshape(AT_HKV, AT_G)
    ql = axial_rope(ql).reshape(B, T, AT_HKV, AT_G, AT_HD)
    kl = axial_rope(kl)
    at_l = windowed_gqa(ql, kl, vl, kc, vc, sink)
    mix_l = jnp.concatenate([rmsnorm(hy_l, lp['hy_out_g']), rw_l, rmsnorm(at_l, lp['at_out_g'])], axis=-1)
    out_l = proj(mix_l, lp['w_out'])
    if not need_ctx:
        return out_l, None
    hy_c = hyena_mix(sc[..., OFF_HY:OFF_RK], hyena_filter(Lc, *filt_p), lp['hy_bias'])
    at_c = context_attention(qc.reshape(B, Lc, AT_HKV, AT_G, AT_HD), kc, vc, sink)
    mix_c = jnp.concatenate([rmsnorm(hy_c, lp['hy_out_g']), rw_c, rmsnorm(at_c, lp['at_out_g'])], axis=-1)
    out_c = proj(mix_c, lp['w_out'])
    return out_l, out_c


def peer_ffn(h, wq, k1, k2, u, v):
    n, d = h.shape
    qall = matmul(h, wq)
    hb = h.reshape(n // PEER_CHUNK, PEER_CHUNK, d)
    qb = qall.reshape(n // PEER_CHUNK, PEER_CHUNK, PEER_HEADS, 2, PEER_DQ // 2)

    def chunk(args):
        xc, q = args
        s1 = jnp.einsum('chd,hnd->chn', q[:, :, 0], k1)
        s2 = jnp.einsum('chd,hnd->chn', q[:, :, 1], k2)
        v1, i1 = lax.top_k(s1, PEER_TOPK)
        v2, i2 = lax.top_k(s2, PEER_TOPK)
        cand = (v1[..., :, None] + v2[..., None, :]).reshape(PEER_CHUNK, PEER_HEADS, PEER_TOPK * PEER_TOPK)
        cidx = (i1[..., :, None] * PEER_NKEYS + i2[..., None, :]).reshape(PEER_CHUNK, PEER_HEADS, PEER_TOPK * PEER_TOPK)
        sc, pos = lax.top_k(cand, PEER_TOPK)
        eidx = jnp.take_along_axis(cidx, pos, axis=-1)
        gate = jax.nn.softmax(sc, axis=-1)
        act = jax.nn.gelu(jnp.einsum('cd,chkd->chk', xc, u[eidx]), approximate=False)
        return jnp.einsum('chk,chkd->cd', gate * act, v[eidx])

    return lax.map(chunk, (hb, qb)).reshape(n, d)


def kernel(x, c, ctx, c_ctx, mod_w, mod_b, norm1_g, norm2_g, w_in, conv_w, conv_b, hy_w1, hy_b1, hy_w2, hy_b2, hy_w3, hy_b3, hy_freq, hy_bias, rw_w0, rw_w_up, rw_a0, rw_a_up, rw_g_up, rw_k_k, rw_k_a, rw_r_k, rw_ln_w, rw_ln_b, attn_sink, hy_out_g, at_out_g, w_out, peer_wq, peer_k1, peer_k2, peer_u, peer_v, final_g):
    B, T, D = x.shape
    Lc = ctx.shape[1]
    c_s = jax.nn.silu(c)
    cc_s = jax.nn.silu(c_ctx)
    xl, xc = x, ctx
    for i in range(DEPTH):
        last = i == DEPTH - 1
        lp = dict(w_in=w_in[i], conv_w=conv_w[i], conv_b=conv_b[i],
                  hy_w1=hy_w1[i], hy_b1=hy_b1[i], hy_w2=hy_w2[i], hy_b2=hy_b2[i],
                  hy_w3=hy_w3[i], hy_b3=hy_b3[i], hy_freq=hy_freq[i], hy_bias=hy_bias[i],
                  rw_w0=rw_w0[i], rw_w_up=rw_w_up[i], rw_a0=rw_a0[i], rw_a_up=rw_a_up[i],
                  rw_g_up=rw_g_up[i], rw_k_k=rw_k_k[i], rw_k_a=rw_k_a[i], rw_r_k=rw_r_k[i],
                  rw_ln_w=rw_ln_w[i], rw_ln_b=rw_ln_b[i], attn_sink=attn_sink[i],
                  hy_out_g=hy_out_g[i], at_out_g=at_out_g[i], w_out=w_out[i])
        sh1, sc1, g1, sh2, sc2, g2 = jnp.split((c_s @ mod_w[i] + mod_b[i])[:, None, :], 6, axis=-1)
        csh1, csc1, cg1, csh2, csc2, cg2 = jnp.split(cc_s @ mod_w[i] + mod_b[i], 6, axis=-1)
        hl = rmsnorm(xl, norm1_g[i]) * (1.0 + sc1) + sh1
        hc = rmsnorm(xc, norm1_g[i]) * (1.0 + csc1) + csh1
        ml, mc = token_mixer(hl, hc, lp, not last)
        xl = xl + g1 * ml
        h2l = rmsnorm(xl, norm2_g[i]) * (1.0 + sc2) + sh2
        if last:
            fl = peer_ffn(h2l.reshape(B * T, D), peer_wq[i], peer_k1[i], peer_k2[i], peer_u[i], peer_v[i]).reshape(B, T, D)
        else:
            xc = xc + cg1 * mc
            h2c = rmsnorm(xc, norm2_g[i]) * (1.0 + csc2) + csh2
            tok = jnp.concatenate([h2l.reshape(B * T, D), h2c.reshape(B * Lc, D)], axis=0)
            f = peer_ffn(tok, peer_wq[i], peer_k1[i], peer_k2[i], peer_u[i], peer_v[i])
            fl = f[:B * T].reshape(B, T, D)
            xc = xc + cg2 * f[B * T:].reshape(B, Lc, D)
        xl = xl + g2 * fl
    return rmsnorm(xl, final_g)
```

```python
import functools
import math

import jax
import jax.numpy as jnp
from jax import lax
from jax.experimental import pallas as pl
from jax.experimental.pallas import tpu as pltpu

D_MODEL = 1024
DEPTH = 4
GRID_W = 64

HY_CH = 256
RW_HEADS = 4
RW_HD = 64
RW_W = RW_HEADS * RW_HD
AT_HQ = 8
AT_HKV = 2
AT_G = AT_HQ // AT_HKV
AT_HD = 64
AT_W = AT_HQ * AT_HD
AT_KVW = AT_HKV * AT_HD

RW_DECAY_R = 64
RW_ICLR_R = 64
RW_GATE_R = 128
RW_LORA_W = 2 * RW_DECAY_R + 2 * RW_ICLR_R + RW_GATE_R
RW_GN_EPS = RW_HD * 1e-5

HY_EMB = 33
HY_BANDS = (HY_EMB - 1) // 2
HY_SHIFT = 0.05
HY_MIN_DECAY = math.log(1e-2) / 1.5
HY_MAX_DECAY = math.log(1e-2) / 0.3

ATT_WINDOW = 128
ATT_BLOCK = 128
ROPE_BASE = 10000.0
NEG_INF = -1e30

PEER_HEADS = 8
PEER_NKEYS = 128
PEER_DQ = 256
PEER_TOPK = 16
PEER_CHUNK = 128

OFF_HY = 0
OFF_RK = OFF_HY + 3 * HY_CH
OFF_LORA = OFF_RK + 3 * RW_W
OFF_AT = OFF_LORA + RW_LORA_W
CONV_COLS = OFF_LORA

VMEM_LIMIT = 48 * 1024 * 1024


def _mm_kernel(x_ref, w_ref, o_ref):
    o_ref[...] = jnp.dot(x_ref[...].astype(jnp.bfloat16), w_ref[...],
                         preferred_element_type=jnp.float32)


def matmul(x, w, tm=512):
    m, k = x.shape
    n = w.shape[1]
    assert m % tm == 0
    return pl.pallas_call(
        _mm_kernel,
        grid=(m // tm,),
        in_specs=[pl.BlockSpec((tm, k), lambda i: (i, 0)),
                  pl.BlockSpec((k, n), lambda i: (0, 0))],
        out_specs=pl.BlockSpec((tm, n), lambda i: (i, 0)),
        out_shape=jax.ShapeDtypeStruct((m, n), jnp.float32),
        compiler_params=pltpu.CompilerParams(dimension_semantics=("parallel",),
                                             vmem_limit_bytes=VMEM_LIMIT),
        name="matmul",
    )(x, w.astype(jnp.bfloat16))


def proj(x3, w):
    b, l, d = x3.shape
    return matmul(x3.reshape(b * l, d), w).reshape(b, l, w.shape[1])


def rmsnorm(x, g, eps=1e-6):
    y = x * lax.rsqrt(jnp.mean(x * x, axis=-1, keepdims=True) + eps)
    return y * g


def short_conv3(u, w, b):
    up = jnp.pad(u, ((0, 0), (1, 1), (0, 0)))
    return up[:, :-2] * w[0] + up[:, 1:-1] * w[1] + up[:, 2:] * w[2] + b


def hyena_filter(L, w1, b1, w2, b2, w3, b3, freq):
    f32 = jnp.float32
    j = jnp.arange(L, dtype=f32)
    t = j / max(L - 1, 1)
    bands = jnp.linspace(1e-4, HY_BANDS - 1, HY_BANDS, dtype=f32)
    ang = 2.0 * math.pi * j[:, None] * bands[None, :] / L
    z = jnp.concatenate([t[:, None], jnp.cos(ang), -jnp.sin(ang)], axis=-1)
    h = jnp.sin(freq[0] * (z @ w1 + b1))
    h = jnp.sin(freq[1] * (h @ w2 + b2))
    h = (h @ w3 + b3).astype(f32).reshape(L, 2, HY_CH)
    deltas = jnp.abs(jnp.linspace(HY_MIN_DECAY, HY_MAX_DECAY, HY_CH, dtype=f32))
    window = jnp.exp(-t[:, None] * deltas[None, :]) + HY_SHIFT
    h = h * window[:, None, :]
    k = jnp.concatenate([h[:, 0], jnp.zeros((1, HY_CH), f32), h[:0:-1, 1]], axis=0)
    return k * lax.rsqrt(jnp.sum(k * k, axis=0, keepdims=True) + 1e-6)


def hyena_mix(z, filt, d_bias):
    L = z.shape[1]
    x0 = z[..., :HY_CH]
    x1 = z[..., HY_CH:2 * HY_CH]
    v = z[..., 2 * HY_CH:]
    u = x1 * v
    n = 2 * L
    y = jnp.fft.irfft(jnp.fft.rfft(u, n=n, axis=1) * jnp.fft.rfft(filt, n=n, axis=0)[None], n=n, axis=1)[:, :L]
    return x0 * (y + u * d_bias)


RW_LANES = 128
RW_TCHUNK = 32


def _rwkv_scan_kernel(r_ref, w_ref, k_ref, v_ref, a_ref, b_ref, y_ref, s_ref, *, tc):
    @pl.when(pl.program_id(0) == 0)
    def _init():
        s_ref[...] = jnp.zeros_like(s_ref)

    def step(t, carry):
        a_t = a_ref[t]
        w_t = w_ref[t]
        k_t = k_ref[t]
        b_t = b_ref[t]
        r_t = r_ref[t]
        for v in range(RW_HD):
            sv = s_ref[v]
            sa = jnp.sum(sv * a_t, axis=0, keepdims=True)
            vv = v_ref[t, pl.ds(v, 1), :]
            sv = sv * w_t + sa * b_t + vv * k_t
            s_ref[v] = sv
            y_ref[t, pl.ds(v, 1), :] = jnp.sum(sv * r_t, axis=0, keepdims=True)
        return carry

    lax.fori_loop(0, tc, step, 0)


def rwkv_scan_lanes(r, w, k, v, a, b):
    L = r.shape[0]
    tc = RW_TCHUNK
    assert L % tc == 0 and r.shape[1:] == (RW_HD, RW_LANES)
    spec = pl.BlockSpec((tc, RW_HD, RW_LANES), lambda i: (i, 0, 0))
    return pl.pallas_call(
        functools.partial(_rwkv_scan_kernel, tc=tc),
        grid=(L // tc,),
        in_specs=[spec] * 6,
        out_specs=spec,
        out_shape=jax.ShapeDtypeStruct((L, RW_HD, RW_LANES), jnp.float32),
        scratch_shapes=[pltpu.VMEM((RW_HD, RW_HD, RW_LANES), jnp.float32)],
        compiler_params=pltpu.CompilerParams(dimension_semantics=("arbitrary",),
                                             vmem_limit_bytes=VMEM_LIMIT),
        name="rwkv_scan",
    )(r, w, k, v, a, b)


def rwkv7_mix(rkv_c, rkv_l, lora_c, lora_l, lp):
    f32 = jnp.float32
    B, Lc, _ = rkv_c.shape
    T = rkv_l.shape[1]
    Lt = Lc + T
    hs = (B, Lt, RW_HEADS, RW_HD)
    rkv = jnp.concatenate([rkv_c, rkv_l], axis=1)
    lora = jnp.concatenate([lora_c, lora_l], axis=1)
    r = rkv[..., :RW_W].reshape(hs)
    k = rkv[..., RW_W:2 * RW_W].reshape(hs)
    v = rkv[..., 2 * RW_W:].reshape(hs)
    kk = k * lp['rw_k_k'].reshape(RW_HEADS, RW_HD)
    kk = kk * lax.rsqrt(jnp.sum(kk * kk, axis=-1, keepdims=True) + 1e-12)
    k_a = lp['rw_k_a'].reshape(RW_HEADS, RW_HD)
    r_k = lp['rw_r_k']
    perm = jnp.concatenate([jnp.arange(Lc - 1, -1, -1), Lc + jnp.arange(T - 1, -1, -1)])
    assert 2 * B * RW_HEADS == RW_LANES
    bonus = jnp.zeros(hs, f32)
    per_dir = []
    for d in range(2):
        wd = lora[..., d * RW_DECAY_R:(d + 1) * RW_DECAY_R]
        ad = lora[..., 2 * RW_DECAY_R + d * RW_ICLR_R:2 * RW_DECAY_R + (d + 1) * RW_ICLR_R]
        w_log = -jax.nn.softplus(-(lp['rw_w0'][d] + jnp.tanh(wd) @ lp['rw_w_up'][d])) - 0.5
        decay = jnp.exp(-jnp.exp(w_log)).reshape(hs)
        a = jax.nn.sigmoid(lp['rw_a0'][d] + ad @ lp['rw_a_up'][d]).reshape(hs)
        k_d = k * (1.0 + (a - 1.0) * k_a)
        seqs = (r, decay, k_d, v, -kk, kk * a)
        if d == 1:
            seqs = tuple(s[:, perm] for s in seqs)
        per_dir.append(tuple(s.transpose(1, 3, 0, 2).reshape(Lt, RW_HD, B * RW_HEADS) for s in seqs))
        bonus = bonus + jnp.sum(r * k_d * r_k, axis=-1, keepdims=True) * v
    y2 = rwkv_scan_lanes(*(jnp.concatenate([f, bw], axis=-1) for f, bw in zip(*per_dir)))
    y2 = y2.reshape(Lt, RW_HD, 2, B, RW_HEADS).transpose(2, 3, 0, 4, 1)
    y_sum = y2[0] + y2[1][:, perm]
    mu = jnp.mean(y_sum, axis=-1, keepdims=True)
    var = jnp.mean(jnp.square(y_sum - mu), axis=-1, keepdims=True)
    yn = (y_sum - mu) * lax.rsqrt(var + RW_GN_EPS)
    yn = yn * lp['rw_ln_w'].reshape(RW_HEADS, RW_HD) + lp['rw_ln_b'].reshape(RW_HEADS, RW_HD)
    g = jax.nn.sigmoid(lora[..., 2 * RW_DECAY_R + 2 * RW_ICLR_R:]) @ lp['rw_g_up']
    out = (yn + bonus).reshape(B, Lt, RW_W) * g
    return out[:, :Lc], out[:, Lc:]


def axial_rope(x):
    f32 = jnp.float32
    T = x.shape[1]
    rows = T // GRID_W
    row = jnp.repeat(jnp.arange(rows), GRID_W).astype(f32)
    col = jnp.tile(jnp.arange(GRID_W), rows).astype(f32)
    half = AT_HD // 2
    quarter = half // 2
    inv = ROPE_BASE ** (-jnp.arange(quarter, dtype=f32) / quarter)

    def rot(xa, pos):
        ang = pos[:, None] * inv[None, :]
        cos = jnp.cos(ang)[None, :, None, :]
        sin = jnp.sin(ang)[None, :, None, :]
        x1, x2 = xa[..., :quarter], xa[..., quarter:]
        return jnp.concatenate([x1 * cos - x2 * sin, x2 * cos + x1 * sin], axis=-1)

    return jnp.concatenate([rot(x[..., :half], row), rot(x[..., half:], col)], axis=-1)


def split_attn(z):
    B, L, _ = z.shape
    a = z[..., OFF_AT:]
    q = a[..., :AT_W].reshape(B, L, AT_HQ, AT_HD)
    k = a[..., AT_W:AT_W + AT_KVW].reshape(B, L, AT_HKV, AT_HD)
    v = a[..., AT_W + AT_KVW:].reshape(B, L, AT_HKV, AT_HD)
    return q, k, v


def windowed_gqa(q, k, v, kc, vc, sink):
    B, T = q.shape[:2]
    Lc = kc.shape[1]
    nblk = T // ATT_BLOCK
    span = ATT_BLOCK + 2 * ATT_WINDOW
    scale = AT_HD ** -0.5
    kp = jnp.pad(k, ((0, 0), (ATT_WINDOW, ATT_WINDOW), (0, 0), (0, 0)))
    vp = jnp.pad(v, ((0, 0), (ATT_WINDOW, ATT_WINDOW), (0, 0), (0, 0)))
    sink_b = jnp.broadcast_to(sink[None, :, :, None, None], (B, AT_HKV, AT_G, ATT_BLOCK, 1))

    def block(i):
        s0 = i * ATT_BLOCK
        qb = lax.dynamic_slice_in_dim(q, s0, ATT_BLOCK, axis=1)
        kb = lax.dynamic_slice_in_dim(kp, s0, span, axis=1)
        vb = lax.dynamic_slice_in_dim(vp, s0, span, axis=1)
        qpos = s0 + jnp.arange(ATT_BLOCK)
        kpos = s0 - ATT_WINDOW + jnp.arange(span)
        valid = (jnp.abs(qpos[:, None] - kpos[None, :]) <= ATT_WINDOW) & (kpos >= 0)[None, :] & (kpos < T)[None, :]
        s_lat = jnp.einsum('bqgjd,bkgd->bgjqk', qb, kb) * scale
        s_lat = jnp.where(valid, s_lat, NEG_INF)
        s_ctx = jnp.einsum('bqgjd,bkgd->bgjqk', qb, kc) * scale
        p = jax.nn.softmax(jnp.concatenate([s_lat, s_ctx, sink_b], axis=-1), axis=-1)
        return (jnp.einsum('bgjqk,bkgd->bqgjd', p[..., :span], vb)
                + jnp.einsum('bgjqk,bkgd->bqgjd', p[..., span:span + Lc], vc))

    o = lax.map(block, jnp.arange(nblk))
    return jnp.swapaxes(o, 0, 1).reshape(B, T, AT_W)


def context_attention(qc, kc, vc, sink):
    B, Lc = qc.shape[:2]
    scale = AT_HD ** -0.5
    s = jnp.einsum('bqgjd,bkgd->bgjqk', qc, kc) * scale
    sink_b = jnp.broadcast_to(sink[None, :, :, None, None], (B, AT_HKV, AT_G, Lc, 1))
    p = jax.nn.softmax(jnp.concatenate([s, sink_b], axis=-1), axis=-1)[..., :Lc]
    return jnp.einsum('bgjqk,bkgd->bqgjd', p, vc).reshape(B, Lc, AT_W)


def token_mixer(hl, hc, lp, need_ctx):
    B, T, _ = hl.shape
    Lc = hc.shape[1]
    zl = proj(hl, lp['w_in'])
    zc = proj(hc, lp['w_in'])
    sl = short_conv3(zl[..., :CONV_COLS], lp['conv_w'], lp['conv_b'])
    sc = short_conv3(zc[..., :CONV_COLS], lp['conv_w'], lp['conv_b'])
    filt_p = (lp['hy_w1'], lp['hy_b1'], lp['hy_w2'], lp['hy_b2'], lp['hy_w3'], lp['hy_b3'], lp['hy_freq'])
    hy_l = hyena_mix(sl[..., OFF_HY:OFF_RK], hyena_filter(T, *filt_p), lp['hy_bias'])
    rw_c, rw_l = rwkv7_mix(sc[..., OFF_RK:], sl[..., OFF_RK:], zc[..., OFF_LORA:OFF_AT], zl[..., OFF_LORA:OFF_AT], lp)
    ql, kl, vl = split_attn(zl)
    qc, kc, vc = split_attn(zc)
    sink = lp['attn_sink'].reshape(AT_HKV, AT_G)
    ql = axial_rope(ql).reshape(B, T, AT_HKV, AT_G, AT_HD)
    kl = axial_rope(kl)
    at_l = windowed_gqa(ql, kl, vl, kc, vc, sink)
    mix_l = jnp.concatenate([rmsnorm(hy_l, lp['hy_out_g']), rw_l, rmsnorm(at_l, lp['at_out_g'])], axis=-1)
    out_l = proj(mix_l, lp['w_out'])
    if not need_ctx:
        return out_l, None
    hy_c = hyena_mix(sc[..., OFF_HY:OFF_RK], hyena_filter(Lc, *filt_p), lp['hy_bias'])
    at_c = context_attention(qc.reshape(B, Lc, AT_HKV, AT_G, AT_HD), kc, vc, sink)
    mix_c = jnp.concatenate([rmsnorm(hy_c, lp['hy_out_g']), rw_c, rmsnorm(at_c, lp['at_out_g'])], axis=-1)
    out_c = proj(mix_c, lp['w_out'])
    return out_l, out_c


def _top16_rows(s, n_rows):
    rio = lax.broadcasted_iota(jnp.int32, s.shape, 0)
    vals, idxs = [], []
    for _ in range(PEER_TOPK):
        m = jnp.max(s, axis=0, keepdims=True)
        idx = jnp.min(jnp.where(s == m, rio, n_rows), axis=0, keepdims=True)
        vals.append(m)
        idxs.append(idx)
        s = jnp.where(rio == idx, -jnp.inf, s)
    return jnp.concatenate(vals, axis=0), jnp.concatenate(idxs, axis=0)


def _route_kernel(x_ref, wq_ref, k1_ref, k2_ref, i1_ref, i2_ref, g_ref):
    q = jnp.dot(x_ref[...].astype(jnp.bfloat16), wq_ref[...], preferred_element_type=jnp.float32)
    q = q.astype(jnp.bfloat16)
    nt = (((1,), (1,)), ((), ()))
    s1 = lax.dot_general(k1_ref[0], q[:, :PEER_DQ // 2], nt, preferred_element_type=jnp.float32)
    s2 = lax.dot_general(k2_ref[0], q[:, PEER_DQ // 2:], nt, preferred_element_type=jnp.float32)
    v1, j1 = _top16_rows(s1, PEER_NKEYS)
    v2, j2 = _top16_rows(s2, PEER_NKEYS)
    cand = jnp.concatenate([v1[a:a + 1] + v2 for a in range(PEER_TOPK)], axis=0)
    ecand = jnp.concatenate([j1[a:a + 1] * PEER_NKEYS + j2 for a in range(PEER_TOPK)], axis=0)
    cio = lax.broadcasted_iota(jnp.int32, cand.shape, 0)
    scs, es = [], []
    for _ in range(PEER_TOPK):
        m = jnp.max(cand, axis=0, keepdims=True)
        cidx = jnp.min(jnp.where(cand == m, cio, PEER_TOPK * PEER_TOPK), axis=0, keepdims=True)
        sel = cio == cidx
        es.append(jnp.max(jnp.where(sel, ecand, -1), axis=0, keepdims=True))
        scs.append(m)
        cand = jnp.where(sel, -jnp.inf, cand)
    sc = jnp.concatenate(scs, axis=0)
    e = jnp.concatenate(es, axis=0)
    p = jnp.exp(sc - sc[0:1])
    i1_ref[0] = e >> 7
    i2_ref[0] = e & (PEER_NKEYS - 1)
    g_ref[0] = p / jnp.sum(p, axis=0, keepdims=True)


def peer_route(h, wq_bf, k1_bf, k2_bf, tt):
    n, d = h.shape
    assert n % tt == 0
    out_sds = [jax.ShapeDtypeStruct((PEER_HEADS, PEER_TOPK, n), jnp.int32),
               jax.ShapeDtypeStruct((PEER_HEADS, PEER_TOPK, n), jnp.int32),
               jax.ShapeDtypeStruct((PEER_HEADS, PEER_TOPK, n), jnp.float32)]
    ospec = pl.BlockSpec((1, PEER_TOPK, tt), lambda i, hh: (hh, 0, i))
    return pl.pallas_call(
        _route_kernel,
        grid=(n // tt, PEER_HEADS),
        in_specs=[pl.BlockSpec((tt, d), lambda i, hh: (i, 0)),
                  pl.BlockSpec((d, PEER_DQ), lambda i, hh: (0, hh)),
                  pl.BlockSpec((1, PEER_NKEYS, PEER_DQ // 2), lambda i, hh: (hh, 0, 0)),
                  pl.BlockSpec((1, PEER_NKEYS, PEER_DQ // 2), lambda i, hh: (hh, 0, 0))],
        out_specs=[ospec, ospec, ospec],
        out_shape=out_sds,
        compiler_params=pltpu.CompilerParams(dimension_semantics=("parallel", "arbitrary"),
                                             vmem_limit_bytes=VMEM_LIMIT),
        name="peer_route",
    )(h, wq_bf, k1_bf, k2_bf)


def _gelu(x):
    return 0.5 * x * (1.0 + lax.erf(x * (2.0 ** -0.5)))


def _expert_kernel(x_ref, i1_ref, i2_ref, g_ref, ut_ref, v_ref, o_ref, w_ref, rows_ref, acc_ref, *, tt, eb):
    j = pl.program_id(1)
    nk = PEER_NKEYS

    @pl.when(j == 0)
    def _build():
        rows_ref[0] = i1_ref[...].reshape(nk, tt).astype(jnp.float32).T
        rows_ref[1] = i2_ref[...].reshape(nk, tt).astype(jnp.float32).T
        rows_ref[2] = g_ref[...].reshape(nk, tt).T
        kio = lax.broadcasted_iota(jnp.int32, (nk, nk), 0).astype(jnp.float32)
        nt = (((1,), (1,)), ((), ()))

        def body(t, carry):
            i1row = jnp.broadcast_to(rows_ref[0, pl.ds(t, 1), :], (nk, nk))
            i2row = jnp.broadcast_to(rows_ref[1, pl.ds(t, 1), :], (nk, nk))
            grow = jnp.broadcast_to(rows_ref[2, pl.ds(t, 1), :], (nk, nk))
            m1 = jnp.where(i1row == kio, grow, 0.0)
            hi = m1.astype(jnp.bfloat16)
            lo = (m1 - hi.astype(jnp.float32)).astype(jnp.bfloat16)
            p2 = jnp.where(i2row == kio, 1.0, 0.0).astype(jnp.bfloat16)
            lhs = jnp.concatenate([hi, lo], axis=1)
            rhs = jnp.concatenate([p2, p2], axis=1)
            wt = lax.dot_general(lhs, rhs, nt, preferred_element_type=jnp.float32)
            w_ref[pl.ds(pl.multiple_of(t * nk, nk), nk), :] = wt
            return carry

        lax.fori_loop(0, tt, body, 0)
        acc_ref[...] = jnp.zeros_like(acc_ref)

    a = jnp.dot(x_ref[...].astype(jnp.bfloat16), ut_ref[...], preferred_element_type=jnp.float32)
    nrow = eb // nk
    wj = jnp.concatenate([w_ref[pl.ds(j * nrow + r, tt, stride=nk), :] for r in range(nrow)], axis=1)
    cmat = (wj * _gelu(a)).astype(jnp.bfloat16)
    acc_ref[...] += jnp.dot(cmat, v_ref[...], preferred_element_type=jnp.float32)

    @pl.when(j == pl.num_programs(1) - 1)
    def _fin():
        o_ref[...] = acc_ref[...]


def peer_experts(h, i1, i2, g, ut_bf, v_bf, tt, eb):
    n, d = h.shape
    nexp = v_bf.shape[0]
    assert n % tt == 0 and nexp % eb == 0 and nexp == PEER_NKEYS * PEER_NKEYS
    rspec = pl.BlockSpec((PEER_HEADS, PEER_TOPK, tt), lambda i, j: (0, 0, i))
    return pl.pallas_call(
        functools.partial(_expert_kernel, tt=tt, eb=eb),
        grid=(n // tt, nexp // eb),
        in_specs=[pl.BlockSpec((tt, d), lambda i, j: (i, 0)), rspec, rspec, rspec,
                  pl.BlockSpec((d, eb), lambda i, j: (0, j)),
                  pl.BlockSpec((eb, d), lambda i, j: (j, 0))],
        out_specs=pl.BlockSpec((tt, d), lambda i, j: (i, 0)),
        out_shape=jax.ShapeDtypeStruct((n, d), jnp.float32),
        scratch_shapes=[pltpu.VMEM((tt * PEER_NKEYS, PEER_NKEYS), jnp.float32),
                        pltpu.VMEM((3, tt, PEER_NKEYS), jnp.float32),
                        pltpu.VMEM((tt, d), jnp.float32)],
        compiler_params=pltpu.CompilerParams(dimension_semantics=("parallel", "arbitrary"),
                                             vmem_limit_bytes=VMEM_LIMIT),
        name="peer_experts",
    )(h, i1, i2, g, ut_bf, v_bf)


PEER_TOKEN_TILE = 256
PEER_EXPERT_BLOCK = 1024


def peer_ffn(h, wq, k1, k2, u, v):
    bf = jnp.bfloat16
    i1, i2, g = peer_route(h, wq.astype(bf), k1.astype(bf), k2.astype(bf), PEER_TOKEN_TILE)
    return peer_experts(h, i1, i2, g, u.T.astype(bf), v.astype(bf), PEER_TOKEN_TILE, PEER_EXPERT_BLOCK)


def kernel(x, c, ctx, c_ctx, mod_w, mod_b, norm1_g, norm2_g, w_in, conv_w, conv_b, hy_w1, hy_b1, hy_w2, hy_b2, hy_w3, hy_b3, hy_freq, hy_bias, rw_w0, rw_w_up, rw_a0, rw_a_up, rw_g_up, rw_k_k, rw_k_a, rw_r_k, rw_ln_w, rw_ln_b, attn_sink, hy_out_g, at_out_g, w_out, peer_wq, peer_k1, peer_k2, peer_u, peer_v, final_g):
    B, T, D = x.shape
    Lc = ctx.shape[1]
    c_s = jax.nn.silu(c)
    cc_s = jax.nn.silu(c_ctx)
    xl, xc = x, ctx
    for i in range(DEPTH):
        last = i == DEPTH - 1
        lp = dict(w_in=w_in[i], conv_w=conv_w[i], conv_b=conv_b[i],
                  hy_w1=hy_w1[i], hy_b1=hy_b1[i], hy_w2=hy_w2[i], hy_b2=hy_b2[i],
                  hy_w3=hy_w3[i], hy_b3=hy_b3[i], hy_freq=hy_freq[i], hy_bias=hy_bias[i],
                  rw_w0=rw_w0[i], rw_w_up=rw_w_up[i], rw_a0=rw_a0[i], rw_a_up=rw_a_up[i],
                  rw_g_up=rw_g_up[i], rw_k_k=rw_k_k[i], rw_k_a=rw_k_a[i], rw_r_k=rw_r_k[i],
                  rw_ln_w=rw_ln_w[i], rw_ln_b=rw_ln_b[i], attn_sink=attn_sink[i],
                  hy_out_g=hy_out_g[i], at_out_g=at_out_g[i], w_out=w_out[i])
        sh1, sc1, g1, sh2, sc2, g2 = jnp.split((c_s @ mod_w[i] + mod_b[i])[:, None, :], 6, axis=-1)
        csh1, csc1, cg1, csh2, csc2, cg2 = jnp.split(cc_s @ mod_w[i] + mod_b[i], 6, axis=-1)
        hl = rmsnorm(xl, norm1_g[i]) * (1.0 + sc1) + sh1
        hc = rmsnorm(xc, norm1_g[i]) * (1.0 + csc1) + csh1
        ml, mc = token_mixer(hl, hc, lp, not last)
        xl = xl + g1 * ml
        h2l = rmsnorm(xl, norm2_g[i]) * (1.0 + sc2) + sh2
        if last:
            fl = peer_ffn(h2l.reshape(B * T, D), peer_wq[i], peer_k1[i], peer_k2[i], peer_u[i], peer_v[i]).reshape(B, T, D)
        else:
            xc = xc + cg1 * mc
            h2c = rmsnorm(xc, norm2_g[i]) * (1.0 + csc2) + csh2
            tok = jnp.concatenate([h2l.reshape(B * T, D), h2c.reshape(B * Lc, D)], axis=0)
            f = peer_ffn(tok, peer_wq[i], peer_k1[i], peer_k2[i], peer_u[i], peer_v[i])
            fl = f[:B * T].reshape(B, T, D)
            xc = xc + cg2 * f[B * T:].reshape(B, Lc, D)
        xl = xl + g2 * fl
    return rmsnorm(xl, final_g)
```

```python
import functools
import math

import jax
import jax.numpy as jnp
import numpy as np
from jax import lax
from jax.experimental import pallas as pl
from jax.experimental.pallas import tpu as pltpu

D_MODEL = 1024
DEPTH = 4
GRID_W = 64

HY_CH = 256
RW_HEADS = 4
RW_HD = 64
RW_W = RW_HEADS * RW_HD
AT_HQ = 8
AT_HKV = 2
AT_G = AT_HQ // AT_HKV
AT_HD = 64
AT_W = AT_HQ * AT_HD
AT_KVW = AT_HKV * AT_HD

RW_DECAY_R = 64
RW_ICLR_R = 64
RW_GATE_R = 128
RW_LORA_W = 2 * RW_DECAY_R + 2 * RW_ICLR_R + RW_GATE_R
RW_GN_EPS = RW_HD * 1e-5

HY_EMB = 33
HY_BANDS = (HY_EMB - 1) // 2
HY_SHIFT = 0.05
HY_MIN_DECAY = math.log(1e-2) / 1.5
HY_MAX_DECAY = math.log(1e-2) / 0.3

ATT_WINDOW = 128
ATT_BLOCK = 128
ROPE_BASE = 10000.0
NEG_INF = -1e30

PEER_HEADS = 8
PEER_NKEYS = 128
PEER_DQ = 256
PEER_TOPK = 16
PEER_CHUNK = 128

OFF_HY = 0
OFF_RK = OFF_HY + 3 * HY_CH
OFF_LORA = OFF_RK + 3 * RW_W
OFF_AT = OFF_LORA + RW_LORA_W
CONV_COLS = OFF_LORA

VMEM_LIMIT = 48 * 1024 * 1024


def _mm_kernel(x_ref, w_ref, o_ref):
    o_ref[...] = jnp.dot(x_ref[...].astype(jnp.bfloat16), w_ref[...],
                         preferred_element_type=jnp.float32)


def matmul(x, w, tm=512):
    m, k = x.shape
    n = w.shape[1]
    assert m % tm == 0
    return pl.pallas_call(
        _mm_kernel,
        grid=(m // tm,),
        in_specs=[pl.BlockSpec((tm, k), lambda i: (i, 0)),
                  pl.BlockSpec((k, n), lambda i: (0, 0))],
        out_specs=pl.BlockSpec((tm, n), lambda i: (i, 0)),
        out_shape=jax.ShapeDtypeStruct((m, n), jnp.float32),
        compiler_params=pltpu.CompilerParams(dimension_semantics=("parallel",),
                                             vmem_limit_bytes=VMEM_LIMIT),
        name="matmul",
    )(x, w.astype(jnp.bfloat16))


def proj(x3, w):
    b, l, d = x3.shape
    return matmul(x3.reshape(b * l, d), w).reshape(b, l, w.shape[1])


def rmsnorm(x, g, eps=1e-6):
    y = x * lax.rsqrt(jnp.mean(x * x, axis=-1, keepdims=True) + eps)
    return y * g


def short_conv3(u, w, b):
    up = jnp.pad(u, ((0, 0), (1, 1), (0, 0)))
    return up[:, :-2] * w[0] + up[:, 1:-1] * w[1] + up[:, 2:] * w[2] + b


def hyena_filter(L, w1, b1, w2, b2, w3, b3, freq):
    f32 = jnp.float32
    j = jnp.arange(L, dtype=f32)
    t = j / max(L - 1, 1)
    bands = jnp.linspace(1e-4, HY_BANDS - 1, HY_BANDS, dtype=f32)
    ang = 2.0 * math.pi * j[:, None] * bands[None, :] / L
    z = jnp.concatenate([t[:, None], jnp.cos(ang), -jnp.sin(ang)], axis=-1)
    h = jnp.sin(freq[0] * (z @ w1 + b1))
    h = jnp.sin(freq[1] * (h @ w2 + b2))
    h = (h @ w3 + b3).astype(f32).reshape(L, 2, HY_CH)
    deltas = jnp.abs(jnp.linspace(HY_MIN_DECAY, HY_MAX_DECAY, HY_CH, dtype=f32))
    window = jnp.exp(-t[:, None] * deltas[None, :]) + HY_SHIFT
    h = h * window[:, None, :]
    k = jnp.concatenate([h[:, 0], jnp.zeros((1, HY_CH), f32), h[:0:-1, 1]], axis=0)
    return k * lax.rsqrt(jnp.sum(k * k, axis=0, keepdims=True) + 1e-6)


def hyena_mix(z, filt, d_bias):
    L = z.shape[1]
    x0 = z[..., :HY_CH]
    x1 = z[..., HY_CH:2 * HY_CH]
    v = z[..., 2 * HY_CH:]
    u = x1 * v
    n = 2 * L
    y = jnp.fft.irfft(jnp.fft.rfft(u, n=n, axis=1) * jnp.fft.rfft(filt, n=n, axis=0)[None], n=n, axis=1)[:, :L]
    return x0 * (y + u * d_bias)


RW_LANES = 128
RW_TCHUNK = 32


def _rwkv_scan_kernel(r_ref, w_ref, k_ref, v_ref, a_ref, b_ref, y_ref, s_ref, *, tc):
    @pl.when(pl.program_id(0) == 0)
    def _init():
        s_ref[...] = jnp.zeros_like(s_ref)

    def step(t, carry):
        a_t = a_ref[t]
        w_t = w_ref[t]
        k_t = k_ref[t]
        b_t = b_ref[t]
        r_t = r_ref[t]
        for v in range(RW_HD):
            sv = s_ref[v]
            sa = jnp.sum(sv * a_t, axis=0, keepdims=True)
            vv = v_ref[t, pl.ds(v, 1), :]
            sv = sv * w_t + sa * b_t + vv * k_t
            s_ref[v] = sv
            y_ref[t, pl.ds(v, 1), :] = jnp.sum(sv * r_t, axis=0, keepdims=True)
        return carry

    lax.fori_loop(0, tc, step, 0)


def rwkv_scan_lanes(r, w, k, v, a, b):
    L = r.shape[0]
    tc = RW_TCHUNK
    assert L % tc == 0 and r.shape[1:] == (RW_HD, RW_LANES)
    spec = pl.BlockSpec((tc, RW_HD, RW_LANES), lambda i: (i, 0, 0))
    return pl.pallas_call(
        functools.partial(_rwkv_scan_kernel, tc=tc),
        grid=(L // tc,),
        in_specs=[spec] * 6,
        out_specs=spec,
        out_shape=jax.ShapeDtypeStruct((L, RW_HD, RW_LANES), jnp.float32),
        scratch_shapes=[pltpu.VMEM((RW_HD, RW_HD, RW_LANES), jnp.float32)],
        compiler_params=pltpu.CompilerParams(dimension_semantics=("arbitrary",),
                                             vmem_limit_bytes=VMEM_LIMIT),
        name="rwkv_scan",
    )(r, w, k, v, a, b)


def rwkv7_mix(rkv_c, rkv_l, lora_c, lora_l, lp):
    f32 = jnp.float32
    B, Lc, _ = rkv_c.shape
    T = rkv_l.shape[1]
    Lt = Lc + T
    hs = (B, Lt, RW_HEADS, RW_HD)
    rkv = jnp.concatenate([rkv_c, rkv_l], axis=1)
    lora = jnp.concatenate([lora_c, lora_l], axis=1)
    r = rkv[..., :RW_W].reshape(hs)
    k = rkv[..., RW_W:2 * RW_W].reshape(hs)
    v = rkv[..., 2 * RW_W:].reshape(hs)
    kk = k * lp['rw_k_k'].reshape(RW_HEADS, RW_HD)
    kk = kk * lax.rsqrt(jnp.sum(kk * kk, axis=-1, keepdims=True) + 1e-12)
    k_a = lp['rw_k_a'].reshape(RW_HEADS, RW_HD)
    r_k = lp['rw_r_k']
    perm = jnp.concatenate([jnp.arange(Lc - 1, -1, -1), Lc + jnp.arange(T - 1, -1, -1)])
    assert 2 * B * RW_HEADS == RW_LANES
    bonus = jnp.zeros(hs, f32)
    per_dir = []
    for d in range(2):
        wd = lora[..., d * RW_DECAY_R:(d + 1) * RW_DECAY_R]
        ad = lora[..., 2 * RW_DECAY_R + d * RW_ICLR_R:2 * RW_DECAY_R + (d + 1) * RW_ICLR_R]
        w_log = -jax.nn.softplus(-(lp['rw_w0'][d] + jnp.tanh(wd) @ lp['rw_w_up'][d])) - 0.5
        decay = jnp.exp(-jnp.exp(w_log)).reshape(hs)
        a = jax.nn.sigmoid(lp['rw_a0'][d] + ad @ lp['rw_a_up'][d]).reshape(hs)
        k_d = k * (1.0 + (a - 1.0) * k_a)
        seqs = (r, decay, k_d, v, -kk, kk * a)
        if d == 1:
            seqs = tuple(s[:, perm] for s in seqs)
        per_dir.append(tuple(s.transpose(1, 3, 0, 2).reshape(Lt, RW_HD, B * RW_HEADS) for s in seqs))
        bonus = bonus + jnp.sum(r * k_d * r_k, axis=-1, keepdims=True) * v
    y2 = rwkv_scan_lanes(*(jnp.concatenate([f, bw], axis=-1) for f, bw in zip(*per_dir)))
    y2 = y2.reshape(Lt, RW_HD, 2, B, RW_HEADS).transpose(2, 3, 0, 4, 1)
    y_sum = y2[0] + y2[1][:, perm]
    mu = jnp.mean(y_sum, axis=-1, keepdims=True)
    var = jnp.mean(jnp.square(y_sum - mu), axis=-1, keepdims=True)
    yn = (y_sum - mu) * lax.rsqrt(var + RW_GN_EPS)
    yn = yn * lp['rw_ln_w'].reshape(RW_HEADS, RW_HD) + lp['rw_ln_b'].reshape(RW_HEADS, RW_HD)
    g = jax.nn.sigmoid(lora[..., 2 * RW_DECAY_R + 2 * RW_ICLR_R:]) @ lp['rw_g_up']
    out = (yn + bonus).reshape(B, Lt, RW_W) * g
    return out[:, :Lc], out[:, Lc:]


def rope_tables(T):
    f32 = jnp.float32
    quarter = AT_HD // 4
    inv = ROPE_BASE ** (-jnp.arange(quarter, dtype=f32) / quarter)
    rows = T // GRID_W
    row = jnp.repeat(jnp.arange(rows), GRID_W).astype(f32)
    col = jnp.tile(jnp.arange(GRID_W), rows).astype(f32)
    out_c, out_s = [], []
    for pos in (row, col):
        ang = pos[:, None] * inv[None, :]
        c, s = jnp.cos(ang), jnp.sin(ang)
        out_c += [c, c]
        out_s += [-s, s]
    return jnp.concatenate(out_c, axis=1), jnp.concatenate(out_s, axis=1)


def _rope(x, c, s):
    lane = lax.broadcasted_iota(jnp.int32, x.shape, 1)
    first = (lane % 32) < 16
    partner = jnp.where(first, pltpu.roll(x, x.shape[1] - 16, axis=1), pltpu.roll(x, 16, axis=1))
    return x * c + partner * s


def _attn_kernel(q_ref, kp_ref, kc_ref, kn_ref, vp_ref, vc_ref, vn_ref, kx_ref, vx_ref, sink_ref,
                 cq_ref, sq_ref, ckp_ref, skp_ref, ckc_ref, skc_ref, ckn_ref, skn_ref, o_ref, *, nblk):
    i = pl.program_id(1)
    blk = ATT_BLOCK
    scale = AT_HD ** -0.5
    q = _rope(q_ref[0], cq_ref[...], sq_ref[...])
    kspan = jnp.concatenate([_rope(kp_ref[0], ckp_ref[...], skp_ref[...]),
                             _rope(kc_ref[0], ckc_ref[...], skc_ref[...]),
                             _rope(kn_ref[0], ckn_ref[...], skn_ref[...])], axis=0)
    vspan = jnp.concatenate([vp_ref[0], vc_ref[0], vn_ref[0]], axis=0)
    kx = kx_ref[0]
    vx = vx_ref[0]
    qpos = lax.broadcasted_iota(jnp.int32, (blk, 3 * blk), 0) + blk
    kpos = lax.broadcasted_iota(jnp.int32, (blk, 3 * blk), 1)
    valid = jnp.abs(qpos - kpos) <= ATT_WINDOW
    valid = valid & ((kpos >= blk) | (i > 0)) & ((kpos < 2 * blk) | (i < nblk - 1))
    valid4 = jnp.concatenate([valid] * AT_G, axis=0)
    nt = (((1,), (1,)), ((), ()))
    bf = jnp.bfloat16
    outs = []
    for g in range(AT_HKV):
        kg = kspan[:, g * AT_HD:(g + 1) * AT_HD].astype(bf)
        vg = vspan[:, g * AT_HD:(g + 1) * AT_HD].astype(bf)
        kxg = kx[:, g * AT_HD:(g + 1) * AT_HD].astype(bf)
        vxg = vx[:, g * AT_HD:(g + 1) * AT_HD].astype(bf)
        qg = jnp.concatenate([q[:, (g * AT_G + j) * AT_HD:(g * AT_G + j + 1) * AT_HD] for j in range(AT_G)],
                             axis=0).astype(bf)
        sk = jnp.concatenate([jnp.full((blk, 1), sink_ref[g * AT_G + j], jnp.float32) for j in range(AT_G)], axis=0)
        s_lat = lax.dot_general(qg, kg, nt, preferred_element_type=jnp.float32) * scale
        s_lat = jnp.where(valid4, s_lat, NEG_INF)
        s_ctx = lax.dot_general(qg, kxg, nt, preferred_element_type=jnp.float32) * scale
        m = jnp.maximum(jnp.maximum(jnp.max(s_lat, axis=1, keepdims=True), jnp.max(s_ctx, axis=1, keepdims=True)), sk)
        p_lat = jnp.exp(s_lat - m)
        p_ctx = jnp.exp(s_ctx - m)
        den = jnp.sum(p_lat, axis=1, keepdims=True) + jnp.sum(p_ctx, axis=1, keepdims=True) + jnp.exp(sk - m)
        inv = 1.0 / den
        o = (jnp.dot((p_lat * inv).astype(bf), vg, preferred_element_type=jnp.float32)
             + jnp.dot((p_ctx * inv).astype(bf), vxg, preferred_element_type=jnp.float32))
        outs += [o[j * blk:(j + 1) * blk] for j in range(AT_G)]
    o_ref[0] = jnp.concatenate(outs, axis=1)


def windowed_attention(q, k, v, kx, vx, sink):
    B, T, _ = q.shape
    Lc = kx.shape[1]
    assert T % ATT_BLOCK == 0 and ATT_WINDOW == ATT_BLOCK
    nblk = T // ATT_BLOCK
    c1, s1 = rope_tables(T)
    cq, sq = jnp.tile(c1, (1, AT_HQ)), jnp.tile(s1, (1, AT_HQ))
    ck, sk = jnp.tile(c1, (1, AT_HKV)), jnp.tile(s1, (1, AT_HKV))

    def prev(b, i):
        return (b, jnp.maximum(i - 1, 0), 0)

    def cur(b, i):
        return (b, i, 0)

    def nxt(b, i):
        return (b, jnp.minimum(i + 1, nblk - 1), 0)

    def kb(f):
        return pl.BlockSpec((1, ATT_BLOCK, AT_KVW), f)

    def tb(f):
        return pl.BlockSpec((ATT_BLOCK, AT_KVW), lambda b, i: f(b, i)[1:])

    return pl.pallas_call(
        functools.partial(_attn_kernel, nblk=nblk),
        grid=(B, nblk),
        in_specs=[pl.BlockSpec((1, ATT_BLOCK, AT_W), cur), kb(prev), kb(cur), kb(nxt), kb(prev), kb(cur), kb(nxt),
                  pl.BlockSpec((1, Lc, AT_KVW), lambda b, i: (b, 0, 0)),
                  pl.BlockSpec((1, Lc, AT_KVW), lambda b, i: (b, 0, 0)),
                  pl.BlockSpec(memory_space=pltpu.SMEM),
                  pl.BlockSpec((ATT_BLOCK, AT_W), lambda b, i: (i, 0)),
                  pl.BlockSpec((ATT_BLOCK, AT_W), lambda b, i: (i, 0)),
                  tb(prev), tb(prev), tb(cur), tb(cur), tb(nxt), tb(nxt)],
        out_specs=pl.BlockSpec((1, ATT_BLOCK, AT_W), cur),
        out_shape=jax.ShapeDtypeStruct((B, T, AT_W), jnp.float32),
        compiler_params=pltpu.CompilerParams(dimension_semantics=("parallel", "arbitrary"),
                                             vmem_limit_bytes=VMEM_LIMIT),
        name="window_attention",
    )(q, k, k, k, v, v, v, kx, vx, sink, cq, sq, ck, sk, ck, sk, ck, sk)


def split_attn(z):
    B, L, _ = z.shape
    a = z[..., OFF_AT:]
    q = a[..., :AT_W].reshape(B, L, AT_HQ, AT_HD)
    k = a[..., AT_W:AT_W + AT_KVW].reshape(B, L, AT_HKV, AT_HD)
    v = a[..., AT_W + AT_KVW:].reshape(B, L, AT_HKV, AT_HD)
    return q, k, v


def context_attention(qc, kc, vc, sink):
    B, Lc = qc.shape[:2]
    scale = AT_HD ** -0.5
    s = jnp.einsum('bqgjd,bkgd->bgjqk', qc, kc) * scale
    sink_b = jnp.broadcast_to(sink[None, :, :, None, None], (B, AT_HKV, AT_G, Lc, 1))
    p = jax.nn.softmax(jnp.concatenate([s, sink_b], axis=-1), axis=-1)[..., :Lc]
    return jnp.einsum('bgjqk,bkgd->bqgjd', p, vc).reshape(B, Lc, AT_W)


def token_mixer(hl, hc, lp, need_ctx):
    B, T, _ = hl.shape
    Lc = hc.shape[1]
    zl = proj(hl, lp['w_in'])
    zc = proj(hc, lp['w_in'])
    sl = short_conv3(zl[..., :CONV_COLS], lp['conv_w'], lp['conv_b'])
    sc = short_conv3(zc[..., :CONV_COLS], lp['conv_w'], lp['conv_b'])
    filt_p = (lp['hy_w1'], lp['hy_b1'], lp['hy_w2'], lp['hy_b2'], lp['hy_w3'], lp['hy_b3'], lp['hy_freq'])
    hy_l = hyena_mix(sl[..., OFF_HY:OFF_RK], hyena_filter(T, *filt_p), lp['hy_bias'])
    rw_c, rw_l = rwkv7_mix(sc[..., OFF_RK:], sl[..., OFF_RK:], zc[..., OFF_LORA:OFF_AT], zl[..., OFF_LORA:OFF_AT], lp)
    qc, kc, vc = split_attn(zc)
    sink = lp['attn_sink'].reshape(AT_HKV, AT_G)
    off_k = OFF_AT + AT_W
    off_v = off_k + AT_KVW
    at_l = windowed_attention(zl[..., OFF_AT:off_k], zl[..., off_k:off_v], zl[..., off_v:],
                              zc[..., off_k:off_v], zc[..., off_v:], lp['attn_sink'])
    mix_l = jnp.concatenate([rmsnorm(hy_l, lp['hy_out_g']), rw_l, rmsnorm(at_l, lp['at_out_g'])], axis=-1)
    out_l = proj(mix_l, lp['w_out'])
    if not need_ctx:
        return out_l, None
    hy_c = hyena_mix(sc[..., OFF_HY:OFF_RK], hyena_filter(Lc, *filt_p), lp['hy_bias'])
    at_c = context_attention(qc.reshape(B, Lc, AT_HKV, AT_G, AT_HD), kc, vc, sink)
    mix_c = jnp.concatenate([rmsnorm(hy_c, lp['hy_out_g']), rw_c, rmsnorm(at_c, lp['at_out_g'])], axis=-1)
    out_c = proj(mix_c, lp['w_out'])
    return out_l, out_c


def _top16_rows(s, n_rows):
    rio = lax.broadcasted_iota(jnp.int32, s.shape, 0)
    vals, idxs = [], []
    for _ in range(PEER_TOPK):
        m = jnp.max(s, axis=0, keepdims=True)
        idx = jnp.min(jnp.where(s == m, rio, n_rows), axis=0, keepdims=True)
        vals.append(m)
        idxs.append(idx)
        s = jnp.where(rio == idx, -jnp.inf, s)
    return jnp.concatenate(vals, axis=0), jnp.concatenate(idxs, axis=0)


LANE = 128


def _cand_layout():
    rows = [(0, b) for b in range(PEER_TOPK)]
    for a in range(1, 8):
        rows += [(a, b) for b in range(8)]
    rows += [(a, 0) for a in range(8, PEER_TOPK)]
    a = np.array([r[0] for r in rows])
    b = np.array([r[1] for r in rows])
    return a, b, (a + 1) * (b + 1) <= PEER_TOPK


def cand_table():
    a, b, valid = _cand_layout()
    cio = np.where(valid, a * PEER_TOPK + b, PEER_TOPK * PEER_TOPK).astype(np.int32)
    return jnp.asarray(np.broadcast_to(cio[:, None], (cio.shape[0], LANE)))


def _route_kernel(x_ref, wq_ref, k1_ref, k2_ref, cio_ref, i1_ref, i2_ref, g_ref, q_ref, *, tt):
    q = jnp.dot(x_ref[...].astype(jnp.bfloat16), wq_ref[...], preferred_element_type=jnp.float32)
    q_ref[...] = q.astype(jnp.bfloat16)
    nt = (((1,), (1,)), ((), ()))
    cio = cio_ref[...]
    ncand = PEER_TOPK * PEER_TOPK

    def chunk(c, carry):
        t0 = pl.multiple_of(c * LANE, LANE)
        qc = q_ref[pl.ds(t0, LANE), :]
        s1 = lax.dot_general(k1_ref[0], qc[:, :PEER_DQ // 2], nt, preferred_element_type=jnp.float32)
        s2 = lax.dot_general(k2_ref[0], qc[:, PEER_DQ // 2:], nt, preferred_element_type=jnp.float32)
        v1, j1 = _top16_rows(s1, PEER_NKEYS)
        v2, j2 = _top16_rows(s2, PEER_NKEYS)
        j1 = j1 * PEER_NKEYS
        cand = jnp.concatenate([v1[0:1] + v2] + [v1[a:a + 1] + v2[0:8] for a in range(1, 8)]
                               + [v1[8:16] + v2[0:1]], axis=0)
        ecand = jnp.concatenate([j1[0:1] + j2] + [j1[a:a + 1] + j2[0:8] for a in range(1, 8)]
                                + [j1[8:16] + j2[0:1]], axis=0)
        cand = jnp.where(cio < ncand, cand, -jnp.inf)
        scs, es = [], []
        for _ in range(PEER_TOPK):
            m = jnp.max(cand, axis=0, keepdims=True)
            cidx = jnp.min(jnp.where(cand == m, cio, ncand), axis=0, keepdims=True)
            sel = cio == cidx
            es.append(jnp.max(jnp.where(sel, ecand, -1), axis=0, keepdims=True))
            scs.append(m)
            cand = jnp.where(sel, -jnp.inf, cand)
        sc = jnp.concatenate(scs, axis=0)
        e = jnp.concatenate(es, axis=0)
        p = jnp.exp(sc - sc[0:1])
        i1_ref[0, :, pl.ds(t0, LANE)] = e >> 7
        i2_ref[0, :, pl.ds(t0, LANE)] = e & (PEER_NKEYS - 1)
        g_ref[0, :, pl.ds(t0, LANE)] = p / jnp.sum(p, axis=0, keepdims=True)
        return carry

    lax.fori_loop(0, tt // LANE, chunk, 0, unroll=2)


def peer_route(h, wq_bf, k1_bf, k2_bf, tt):
    n, d = h.shape
    assert n % tt == 0 and tt % (2 * LANE) == 0
    cio = cand_table()
    out_sds = [jax.ShapeDtypeStruct((PEER_HEADS, PEER_TOPK, n), jnp.int32),
               jax.ShapeDtypeStruct((PEER_HEADS, PEER_TOPK, n), jnp.int32),
               jax.ShapeDtypeStruct((PEER_HEADS, PEER_TOPK, n), jnp.float32)]
    ospec = pl.BlockSpec((1, PEER_TOPK, tt), lambda i, hh: (hh, 0, i))
    return pl.pallas_call(
        functools.partial(_route_kernel, tt=tt),
        grid=(n // tt, PEER_HEADS),
        in_specs=[pl.BlockSpec((tt, d), lambda i, hh: (i, 0)),
                  pl.BlockSpec((d, PEER_DQ), lambda i, hh: (0, hh)),
                  pl.BlockSpec((1, PEER_NKEYS, PEER_DQ // 2), lambda i, hh: (hh, 0, 0)),
                  pl.BlockSpec((1, PEER_NKEYS, PEER_DQ // 2), lambda i, hh: (hh, 0, 0)),
                  pl.BlockSpec(cio.shape, lambda i, hh: (0, 0))],
        out_specs=[ospec, ospec, ospec],
        out_shape=out_sds,
        scratch_shapes=[pltpu.VMEM((tt, PEER_DQ), jnp.bfloat16)],
        compiler_params=pltpu.CompilerParams(dimension_semantics=("parallel", "arbitrary"),
                                             vmem_limit_bytes=VMEM_LIMIT),
        name="peer_route",
    )(h, wq_bf, k1_bf, k2_bf, cio)


def _gelu(x):
    return 0.5 * x * (1.0 + lax.erf(x * (2.0 ** -0.5)))


PEER_BUILD_GROUP = 8


def _expert_kernel(x_ref, i1_ref, i2_ref, g_ref, ut_ref, v_ref, o_ref, w_ref, rows_ref, xb_ref, *, tt, eb):
    j = pl.program_id(1)
    nk = PEER_NKEYS
    grp = PEER_BUILD_GROUP

    @pl.when(j == 0)
    def _build():
        xb_ref[...] = x_ref[...].astype(jnp.bfloat16)
        rows_ref[0] = i1_ref[...].reshape(nk, tt).astype(jnp.float32).T
        rows_ref[1] = i2_ref[...].reshape(nk, tt).astype(jnp.float32).T
        rows_ref[2] = g_ref[...].reshape(nk, tt).T
        kio = lax.broadcasted_iota(jnp.int32, (nk, nk), 0).astype(jnp.float32)
        nt = (((1,), (1,)), ((), ()))

        def body(gidx, carry):
            t0 = pl.multiple_of(gidx * grp, grp)
            tiles = []
            for s in range(grp):
                i1row = jnp.broadcast_to(rows_ref[0, pl.ds(t0 + s, 1), :], (nk, nk))
                i2row = jnp.broadcast_to(rows_ref[1, pl.ds(t0 + s, 1), :], (nk, nk))
                grow = jnp.broadcast_to(rows_ref[2, pl.ds(t0 + s, 1), :], (nk, nk))
                m1 = jnp.where(i1row == kio, grow, 0.0)
                hi = m1.astype(jnp.bfloat16)
                lo = (m1 - hi.astype(jnp.float32)).astype(jnp.bfloat16)
                p2 = jnp.where(i2row == kio, 1.0, 0.0).astype(jnp.bfloat16)
                lhs = jnp.concatenate([hi, lo], axis=1)
                rhs = jnp.concatenate([p2, p2], axis=1)
                tiles.append(lax.dot_general(lhs, rhs, nt, preferred_element_type=jnp.float32))
            w_ref[:, pl.ds(t0, grp), :] = jnp.swapaxes(jnp.stack(tiles, axis=0), 0, 1)
            return carry

        lax.fori_loop(0, tt // grp, body, 0)

    a = jnp.dot(xb_ref[...], ut_ref[...], preferred_element_type=jnp.float32)
    nrow = eb // nk
    wj = jnp.concatenate([w_ref[j * nrow + r] for r in range(nrow)], axis=1)
    cmat = (wj * _gelu(a)).astype(jnp.bfloat16)
    contrib = jnp.dot(cmat, v_ref[...], preferred_element_type=jnp.float32)

    @pl.when(j == 0)
    def _first():
        o_ref[...] = contrib

    @pl.when(j > 0)
    def _rest():
        o_ref[...] += contrib


def peer_experts(h, i1, i2, g, ut_bf, v_bf, tt, eb):
    n, d = h.shape
    nexp = v_bf.shape[0]
    assert n % tt == 0 and nexp % eb == 0 and nexp == PEER_NKEYS * PEER_NKEYS
    rspec = pl.BlockSpec((PEER_HEADS, PEER_TOPK, tt), lambda i, j: (0, 0, i))
    return pl.pallas_call(
        functools.partial(_expert_kernel, tt=tt, eb=eb),
        grid=(n // tt, nexp // eb),
        in_specs=[pl.BlockSpec((tt, d), lambda i, j: (i, 0)), rspec, rspec, rspec,
                  pl.BlockSpec((d, eb), lambda i, j: (0, j)),
                  pl.BlockSpec((eb, d), lambda i, j: (j, 0))],
        out_specs=pl.BlockSpec((tt, d), lambda i, j: (i, 0)),
        out_shape=jax.ShapeDtypeStruct((n, d), jnp.float32),
        scratch_shapes=[pltpu.VMEM((PEER_NKEYS, tt, PEER_NKEYS), jnp.float32),
                        pltpu.VMEM((3, tt, PEER_NKEYS), jnp.float32),
                        pltpu.VMEM((tt, d), jnp.bfloat16)],
        compiler_params=pltpu.CompilerParams(dimension_semantics=("parallel", "arbitrary"),
                                             vmem_limit_bytes=VMEM_LIMIT),
        name="peer_experts",
    )(h, i1, i2, g, ut_bf, v_bf)


PEER_ROUTE_TILE = 512
PEER_TOKEN_TILE = 256
PEER_EXPERT_BLOCK = 1024


def peer_ffn(h, wq, k1, k2, u, v):
    bf = jnp.bfloat16
    i1, i2, g = peer_route(h, wq.astype(bf), k1.astype(bf), k2.astype(bf), PEER_ROUTE_TILE)
    return peer_experts(h, i1, i2, g, u.T.astype(bf), v.astype(bf), PEER_TOKEN_TILE, PEER_EXPERT_BLOCK)


def kernel(x, c, ctx, c_ctx, mod_w, mod_b, norm1_g, norm2_g, w_in, conv_w, conv_b, hy_w1, hy_b1, hy_w2, hy_b2, hy_w3, hy_b3, hy_freq, hy_bias, rw_w0, rw_w_up, rw_a0, rw_a_up, rw_g_up, rw_k_k, rw_k_a, rw_r_k, rw_ln_w, rw_ln_b, attn_sink, hy_out_g, at_out_g, w_out, peer_wq, peer_k1, peer_k2, peer_u, peer_v, final_g):
    B, T, D = x.shape
    Lc = ctx.shape[1]
    c_s = jax.nn.silu(c)
    cc_s = jax.nn.silu(c_ctx)
    xl, xc = x, ctx
    for i in range(DEPTH):
        last = i == DEPTH - 1
        lp = dict(w_in=w_in[i], conv_w=conv_w[i], conv_b=conv_b[i],
                  hy_w1=hy_w1[i], hy_b1=hy_b1[i], hy_w2=hy_w2[i], hy_b2=hy_b2[i],
                  hy_w3=hy_w3[i], hy_b3=hy_b3[i], hy_freq=hy_freq[i], hy_bias=hy_bias[i],
                  rw_w0=rw_w0[i], rw_w_up=rw_w_up[i], rw_a0=rw_a0[i], rw_a_up=rw_a_up[i],
                  rw_g_up=rw_g_up[i], rw_k_k=rw_k_k[i], rw_k_a=rw_k_a[i], rw_r_k=rw_r_k[i],
                  rw_ln_w=rw_ln_w[i], rw_ln_b=rw_ln_b[i], attn_sink=attn_sink[i],
                  hy_out_g=hy_out_g[i], at_out_g=at_out_g[i], w_out=w_out[i])
        sh1, sc1, g1, sh2, sc2, g2 = jnp.split((c_s @ mod_w[i] + mod_b[i])[:, None, :], 6, axis=-1)
        csh1, csc1, cg1, csh2, csc2, cg2 = jnp.split(cc_s @ mod_w[i] + mod_b[i], 6, axis=-1)
        hl = rmsnorm(xl, norm1_g[i]) * (1.0 + sc1) + sh1
        hc = rmsnorm(xc, norm1_g[i]) * (1.0 + csc1) + csh1
        ml, mc = token_mixer(hl, hc, lp, not last)
        xl = xl + g1 * ml
        h2l = rmsnorm(xl, norm2_g[i]) * (1.0 + sc2) + sh2
        if last:
            fl = peer_ffn(h2l.reshape(B * T, D), peer_wq[i], peer_k1[i], peer_k2[i], peer_u[i], peer_v[i]).reshape(B, T, D)
        else:
            xc = xc + cg1 * mc
            h2c = rmsnorm(xc, norm2_g[i]) * (1.0 + csc2) + csh2
            tok = jnp.concatenate([h2l.reshape(B * T, D), h2c.reshape(B * Lc, D)], axis=0)
            f = peer_ffn(tok, peer_wq[i], peer_k1[i], peer_k2[i], peer_u[i], peer_v[i])
            fl = f[:B * T].reshape(B, T, D)
            xc = xc + cg2 * f[B * T:].reshape(B, Lc, D)
        xl = xl + g2 * fl
    return rmsnorm(xl, final_g)
```

```python
import functools
import math

import jax
import jax.numpy as jnp
import numpy as np
from jax import lax
from jax.experimental import pallas as pl
from jax.experimental.pallas import tpu as pltpu

D_MODEL = 1024
DEPTH = 4
GRID_W = 64

HY_CH = 256
RW_HEADS = 4
RW_HD = 64
RW_W = RW_HEADS * RW_HD
AT_HQ = 8
AT_HKV = 2
AT_G = AT_HQ // AT_HKV
AT_HD = 64
AT_W = AT_HQ * AT_HD
AT_KVW = AT_HKV * AT_HD

RW_DECAY_R = 64
RW_ICLR_R = 64
RW_GATE_R = 128
RW_LORA_W = 2 * RW_DECAY_R + 2 * RW_ICLR_R + RW_GATE_R
RW_GN_EPS = RW_HD * 1e-5

HY_EMB = 33
HY_BANDS = (HY_EMB - 1) // 2
HY_SHIFT = 0.05
HY_MIN_DECAY = math.log(1e-2) / 1.5
HY_MAX_DECAY = math.log(1e-2) / 0.3

ATT_WINDOW = 128
ATT_BLOCK = 128
ROPE_BASE = 10000.0
NEG_INF = -1e30

PEER_HEADS = 8
PEER_NKEYS = 128
PEER_DQ = 256
PEER_TOPK = 16
PEER_CHUNK = 128

OFF_HY = 0
OFF_RK = OFF_HY + 3 * HY_CH
OFF_LORA = OFF_RK + 3 * RW_W
OFF_AT = OFF_LORA + RW_LORA_W
CONV_COLS = OFF_LORA

VMEM_LIMIT = 48 * 1024 * 1024


def _mm_kernel(x_ref, w_ref, o_ref):
    o_ref[...] = jnp.dot(x_ref[...].astype(jnp.bfloat16), w_ref[...],
                         preferred_element_type=jnp.float32)


def matmul(x, w, tm=512):
    m, k = x.shape
    n = w.shape[1]
    assert m % tm == 0
    return pl.pallas_call(
        _mm_kernel,
        grid=(m // tm,),
        in_specs=[pl.BlockSpec((tm, k), lambda i: (i, 0)),
                  pl.BlockSpec((k, n), lambda i: (0, 0))],
        out_specs=pl.BlockSpec((tm, n), lambda i: (i, 0)),
        out_shape=jax.ShapeDtypeStruct((m, n), jnp.float32),
        compiler_params=pltpu.CompilerParams(dimension_semantics=("parallel",),
                                             vmem_limit_bytes=VMEM_LIMIT),
        name="matmul",
    )(x, w.astype(jnp.bfloat16))


def proj(x3, w):
    b, l, d = x3.shape
    return matmul(x3.reshape(b * l, d), w).reshape(b, l, w.shape[1])


def rmsnorm(x, g, eps=1e-6):
    y = x * lax.rsqrt(jnp.mean(x * x, axis=-1, keepdims=True) + eps)
    return y * g


def short_conv3(u, w, b):
    up = jnp.pad(u, ((0, 0), (1, 1), (0, 0)))
    return up[:, :-2] * w[0] + up[:, 1:-1] * w[1] + up[:, 2:] * w[2] + b


def hyena_filter(L, w1, b1, w2, b2, w3, b3, freq):
    f32 = jnp.float32
    j = jnp.arange(L, dtype=f32)
    t = j / max(L - 1, 1)
    bands = jnp.linspace(1e-4, HY_BANDS - 1, HY_BANDS, dtype=f32)
    ang = 2.0 * math.pi * j[:, None] * bands[None, :] / L
    z = jnp.concatenate([t[:, None], jnp.cos(ang), -jnp.sin(ang)], axis=-1)
    h = jnp.sin(freq[0] * (z @ w1 + b1))
    h = jnp.sin(freq[1] * (h @ w2 + b2))
    h = (h @ w3 + b3).astype(f32).reshape(L, 2, HY_CH)
    deltas = jnp.abs(jnp.linspace(HY_MIN_DECAY, HY_MAX_DECAY, HY_CH, dtype=f32))
    window = jnp.exp(-t[:, None] * deltas[None, :]) + HY_SHIFT
    h = h * window[:, None, :]
    k = jnp.concatenate([h[:, 0], jnp.zeros((1, HY_CH), f32), h[:0:-1, 1]], axis=0)
    return k * lax.rsqrt(jnp.sum(k * k, axis=0, keepdims=True) + 1e-6)


HY_BLOCK = 256


def _hyena_kernel(fr_ref, u_ref, y_ref, *, nblk, nb):
    P = HY_BLOCK
    ncol = nblk * nb
    two_l = 2 * nblk * P
    u = u_ref[0].astype(jnp.bfloat16)
    col = lax.broadcasted_iota(jnp.int32, (P, ncol), 1)
    acc = jnp.zeros((P, ncol), jnp.float32)
    for m in range(-(nblk - 1), nblk):
        start = (-P * m) % two_l
        gr = fr_ref[0, :, start:start + 2 * P]
        rolled = pltpu.roll(jnp.broadcast_to(gr, (P, 2 * P)), P + 1, 1, stride=1, stride_axis=0)
        t_m = rolled[:, :P].astype(jnp.bfloat16)
        prod = jnp.dot(t_m, u, preferred_element_type=jnp.float32)
        if m == 0:
            acc = acc + prod
        elif m > 0:
            acc = acc + jnp.where(col >= nb * m, pltpu.roll(prod, nb * m, 1), 0.0)
        else:
            acc = acc + jnp.where(col < ncol + nb * m, pltpu.roll(prod, ncol + nb * m, 1), 0.0)
    y_ref[0] = acc


def hyena_conv(u, filt):
    B, L, C = u.shape
    P = HY_BLOCK
    assert L % P == 0 and filt.shape == (2 * L, C)
    nblk = L // P
    idx = (P - 1 - np.arange(2 * L + 2 * P)) % (2 * L)
    fr = filt[idx].T.reshape(C, 1, 2 * L + 2 * P)
    uc = u.reshape(B, nblk, P, C).transpose(3, 2, 1, 0).reshape(C, P, nblk * B)
    y = pl.pallas_call(
        functools.partial(_hyena_kernel, nblk=nblk, nb=B),
        grid=(C,),
        in_specs=[pl.BlockSpec((1, 1, 2 * L + 2 * P), lambda c: (c, 0, 0)),
                  pl.BlockSpec((1, P, nblk * B), lambda c: (c, 0, 0))],
        out_specs=pl.BlockSpec((1, P, nblk * B), lambda c: (c, 0, 0)),
        out_shape=jax.ShapeDtypeStruct((C, P, nblk * B), jnp.float32),
        compiler_params=pltpu.CompilerParams(dimension_semantics=("parallel",), vmem_limit_bytes=VMEM_LIMIT),
        name="hyena_conv",
    )(fr, uc)
    return y.reshape(C, P, nblk, B).transpose(3, 2, 1, 0).reshape(B, L, C)


def hyena_mix(z, filt, d_bias):
    x0 = z[..., :HY_CH]
    x1 = z[..., HY_CH:2 * HY_CH]
    v = z[..., 2 * HY_CH:]
    u = x1 * v
    y = hyena_conv(u, filt)
    return x0 * (y + u * d_bias)


RW_LANES = 128
RW_TCHUNK = 32


def _rwkv_scan_kernel(rf, wf, kf, vf, af, bf_, rb, wb, kb, vb, ab, bb, yf_ref, yb_ref, s_ref, *, tc):
    @pl.when(pl.program_id(0) == 0)
    def _init():
        s_ref[...] = jnp.zeros_like(s_ref)

    fwd = lax.broadcasted_iota(jnp.int32, (RW_HD, RW_LANES), 1) < RW_LANES // 2
    fwd_row = fwd[0:1]

    def step(t, carry):
        tb = tc - 1 - t
        a_t = jnp.where(fwd, af[t], ab[tb])
        w_t = jnp.where(fwd, wf[t], wb[tb])
        k_t = jnp.where(fwd, kf[t], kb[tb])
        b_t = jnp.where(fwd, bf_[t], bb[tb])
        r_t = jnp.where(fwd, rf[t], rb[tb])
        for v in range(RW_HD):
            sv = s_ref[v]
            sa = jnp.sum(sv * a_t, axis=0, keepdims=True)
            vv = jnp.where(fwd_row, vf[t, pl.ds(v, 1), :], vb[tb, pl.ds(v, 1), :])
            sv = sv * w_t + sa * b_t + vv * k_t
            s_ref[v] = sv
            y = jnp.sum(sv * r_t, axis=0, keepdims=True)
            yf_ref[t, pl.ds(v, 1), :] = y
            yb_ref[tb, pl.ds(v, 1), :] = y
        return carry

    lax.fori_loop(0, tc, step, 0)


def rwkv_scan_lanes(r, w, k, v, a, b, n_ctx):
    L = r.shape[0]
    tc = RW_TCHUNK
    assert L % tc == 0 and n_ctx % tc == 0 and r.shape[1:] == (RW_HD, RW_LANES)
    nblk, cblk = L // tc, n_ctx // tc

    def fmap(i):
        return (i, 0, 0)

    def bmap(i):
        return (jnp.where(i < cblk, cblk - 1 - i, nblk - 1 + cblk - i), 0, 0)

    fspec = pl.BlockSpec((tc, RW_HD, RW_LANES), fmap)
    bspec = pl.BlockSpec((tc, RW_HD, RW_LANES), bmap)
    sds = jax.ShapeDtypeStruct((L, RW_HD, RW_LANES), jnp.float32)
    return pl.pallas_call(
        functools.partial(_rwkv_scan_kernel, tc=tc),
        grid=(nblk,),
        in_specs=[fspec] * 6 + [bspec] * 6,
        out_specs=[fspec, bspec],
        out_shape=[sds, sds],
        scratch_shapes=[pltpu.VMEM((RW_HD, RW_HD, RW_LANES), jnp.float32)],
        compiler_params=pltpu.CompilerParams(dimension_semantics=("arbitrary",),
                                             vmem_limit_bytes=VMEM_LIMIT),
        name="rwkv_scan",
    )(r, w, k, v, a, b, r, w, k, v, a, b)


def rwkv7_mix(rkv_c, rkv_l, lora_c, lora_l, lp):
    f32 = jnp.float32
    B, Lc, _ = rkv_c.shape
    T = rkv_l.shape[1]
    Lt = Lc + T
    hs = (B, Lt, RW_HEADS, RW_HD)
    rkv = jnp.concatenate([rkv_c, rkv_l], axis=1)
    lora = jnp.concatenate([lora_c, lora_l], axis=1)
    r = rkv[..., :RW_W].reshape(hs)
    k = rkv[..., RW_W:2 * RW_W].reshape(hs)
    v = rkv[..., 2 * RW_W:].reshape(hs)
    kk = k * lp['rw_k_k'].reshape(RW_HEADS, RW_HD)
    kk = kk * lax.rsqrt(jnp.sum(kk * kk, axis=-1, keepdims=True) + 1e-12)
    k_a = lp['rw_k_a'].reshape(RW_HEADS, RW_HD)
    r_k = lp['rw_r_k']
    assert 2 * B * RW_HEADS == RW_LANES
    bonus = jnp.zeros(hs, f32)
    per_dir = []
    for d in range(2):
        wd = lora[..., d * RW_DECAY_R:(d + 1) * RW_DECAY_R]
        ad = lora[..., 2 * RW_DECAY_R + d * RW_ICLR_R:2 * RW_DECAY_R + (d + 1) * RW_ICLR_R]
        w_log = -jax.nn.softplus(-(lp['rw_w0'][d] + jnp.tanh(wd) @ lp['rw_w_up'][d])) - 0.5
        decay = jnp.exp(-jnp.exp(w_log)).reshape(hs)
        a = jax.nn.sigmoid(lp['rw_a0'][d] + ad @ lp['rw_a_up'][d]).reshape(hs)
        k_d = k * (1.0 + (a - 1.0) * k_a)
        per_dir.append((r, decay, k_d, v, -kk, kk * a))
        bonus = bonus + jnp.sum(r * k_d * r_k, axis=-1, keepdims=True) * v

    def lanes(xf, xb):
        return jnp.concatenate([x.transpose(1, 3, 0, 2).reshape(Lt, RW_HD, B * RW_HEADS) for x in (xf, xb)], axis=-1)

    yf, yb = rwkv_scan_lanes(*(lanes(xf, xb) for xf, xb in zip(*per_dir)), n_ctx=Lc)
    half = RW_LANES // 2
    y_sum = (yf[..., :half] + yb[..., half:]).reshape(Lt, RW_HD, B, RW_HEADS).transpose(2, 0, 3, 1)
    mu = jnp.mean(y_sum, axis=-1, keepdims=True)
    var = jnp.mean(jnp.square(y_sum - mu), axis=-1, keepdims=True)
    yn = (y_sum - mu) * lax.rsqrt(var + RW_GN_EPS)
    yn = yn * lp['rw_ln_w'].reshape(RW_HEADS, RW_HD) + lp['rw_ln_b'].reshape(RW_HEADS, RW_HD)
    g = jax.nn.sigmoid(lora[..., 2 * RW_DECAY_R + 2 * RW_ICLR_R:]) @ lp['rw_g_up']
    out = (yn + bonus).reshape(B, Lt, RW_W) * g
    return out[:, :Lc], out[:, Lc:]


def rope_tables(T):
    f32 = jnp.float32
    quarter = AT_HD // 4
    inv = ROPE_BASE ** (-jnp.arange(quarter, dtype=f32) / quarter)
    rows = T // GRID_W
    row = jnp.repeat(jnp.arange(rows), GRID_W).astype(f32)
    col = jnp.tile(jnp.arange(GRID_W), rows).astype(f32)
    out_c, out_s = [], []
    for pos in (row, col):
        ang = pos[:, None] * inv[None, :]
        c, s = jnp.cos(ang), jnp.sin(ang)
        out_c += [c, c]
        out_s += [-s, s]
    return jnp.concatenate(out_c, axis=1), jnp.concatenate(out_s, axis=1)


def _rope(x, c, s):
    lane = lax.broadcasted_iota(jnp.int32, x.shape, 1)
    first = (lane % 32) < 16
    partner = jnp.where(first, pltpu.roll(x, x.shape[1] - 16, axis=1), pltpu.roll(x, 16, axis=1))
    return x * c + partner * s


def _attn_kernel(q_ref, kp_ref, kc_ref, kn_ref, vp_ref, vc_ref, vn_ref, kx_ref, vx_ref, sink_ref,
                 cq_ref, sq_ref, ckp_ref, skp_ref, ckc_ref, skc_ref, ckn_ref, skn_ref, o_ref, *, nblk):
    i = pl.program_id(1)
    blk = ATT_BLOCK
    scale = AT_HD ** -0.5
    q = _rope(q_ref[0], cq_ref[...], sq_ref[...])
    kspan = jnp.concatenate([_rope(kp_ref[0], ckp_ref[...], skp_ref[...]),
                             _rope(kc_ref[0], ckc_ref[...], skc_ref[...]),
                             _rope(kn_ref[0], ckn_ref[...], skn_ref[...])], axis=0)
    vspan = jnp.concatenate([vp_ref[0], vc_ref[0], vn_ref[0]], axis=0)
    kx = kx_ref[0]
    vx = vx_ref[0]
    qpos = lax.broadcasted_iota(jnp.int32, (blk, 3 * blk), 0) + blk
    kpos = lax.broadcasted_iota(jnp.int32, (blk, 3 * blk), 1)
    valid = jnp.abs(qpos - kpos) <= ATT_WINDOW
    valid = valid & ((kpos >= blk) | (i > 0)) & ((kpos < 2 * blk) | (i < nblk - 1))
    valid4 = jnp.concatenate([valid] * AT_G, axis=0)
    nt = (((1,), (1,)), ((), ()))
    bf = jnp.bfloat16
    outs = []
    for g in range(AT_HKV):
        kg = kspan[:, g * AT_HD:(g + 1) * AT_HD].astype(bf)
        vg = vspan[:, g * AT_HD:(g + 1) * AT_HD].astype(bf)
        kxg = kx[:, g * AT_HD:(g + 1) * AT_HD].astype(bf)
        vxg = vx[:, g * AT_HD:(g + 1) * AT_HD].astype(bf)
        qg = jnp.concatenate([q[:, (g * AT_G + j) * AT_HD:(g * AT_G + j + 1) * AT_HD] for j in range(AT_G)],
                             axis=0).astype(bf)
        sk = jnp.concatenate([jnp.full((blk, 1), sink_ref[g * AT_G + j], jnp.float32) for j in range(AT_G)], axis=0)
        s_lat = lax.dot_general(qg, kg, nt, preferred_element_type=jnp.float32) * scale
        s_lat = jnp.where(valid4, s_lat, NEG_INF)
        s_ctx = lax.dot_general(qg, kxg, nt, preferred_element_type=jnp.float32) * scale
        m = jnp.maximum(jnp.maximum(jnp.max(s_lat, axis=1, keepdims=True), jnp.max(s_ctx, axis=1, keepdims=True)), sk)
        p_lat = jnp.exp(s_lat - m)
        p_ctx = jnp.exp(s_ctx - m)
        den = jnp.sum(p_lat, axis=1, keepdims=True) + jnp.sum(p_ctx, axis=1, keepdims=True) + jnp.exp(sk - m)
        inv = 1.0 / den
        o = (jnp.dot((p_lat * inv).astype(bf), vg, preferred_element_type=jnp.float32)
             + jnp.dot((p_ctx * inv).astype(bf), vxg, preferred_element_type=jnp.float32))
        outs += [o[j * blk:(j + 1) * blk] for j in range(AT_G)]
    o_ref[0] = jnp.concatenate(outs, axis=1)


def windowed_attention(q, k, v, kx, vx, sink):
    B, T, _ = q.shape
    Lc = kx.shape[1]
    assert T % ATT_BLOCK == 0 and ATT_WINDOW == ATT_BLOCK
    nblk = T // ATT_BLOCK
    c1, s1 = rope_tables(T)
    cq, sq = jnp.tile(c1, (1, AT_HQ)), jnp.tile(s1, (1, AT_HQ))
    ck, sk = jnp.tile(c1, (1, AT_HKV)), jnp.tile(s1, (1, AT_HKV))

    def prev(b, i):
        return (b, jnp.maximum(i - 1, 0), 0)

    def cur(b, i):
        return (b, i, 0)

    def nxt(b, i):
        return (b, jnp.minimum(i + 1, nblk - 1), 0)

    def kb(f):
        return pl.BlockSpec((1, ATT_BLOCK, AT_KVW), f)

    def tb(f):
        return pl.BlockSpec((ATT_BLOCK, AT_KVW), lambda b, i: f(b, i)[1:])

    return pl.pallas_call(
        functools.partial(_attn_kernel, nblk=nblk),
        grid=(B, nblk),
        in_specs=[pl.BlockSpec((1, ATT_BLOCK, AT_W), cur), kb(prev), kb(cur), kb(nxt), kb(prev), kb(cur), kb(nxt),
                  pl.BlockSpec((1, Lc, AT_KVW), lambda b, i: (b, 0, 0)),
                  pl.BlockSpec((1, Lc, AT_KVW), lambda b, i: (b, 0, 0)),
                  pl.BlockSpec(memory_space=pltpu.SMEM),
                  pl.BlockSpec((ATT_BLOCK, AT_W), lambda b, i: (i, 0)),
                  pl.BlockSpec((ATT_BLOCK, AT_W), lambda b, i: (i, 0)),
                  tb(prev), tb(prev), tb(cur), tb(cur), tb(nxt), tb(nxt)],
        out_specs=pl.BlockSpec((1, ATT_BLOCK, AT_W), cur),
        out_shape=jax.ShapeDtypeStruct((B, T, AT_W), jnp.float32),
        compiler_params=pltpu.CompilerParams(dimension_semantics=("parallel", "arbitrary"),
                                             vmem_limit_bytes=VMEM_LIMIT),
        name="window_attention",
    )(q, k, k, k, v, v, v, kx, vx, sink, cq, sq, ck, sk, ck, sk, ck, sk)


def split_attn(z):
    B, L, _ = z.shape
    a = z[..., OFF_AT:]
    q = a[..., :AT_W].reshape(B, L, AT_HQ, AT_HD)
    k = a[..., AT_W:AT_W + AT_KVW].reshape(B, L, AT_HKV, AT_HD)
    v = a[..., AT_W + AT_KVW:].reshape(B, L, AT_HKV, AT_HD)
    return q, k, v


def context_attention(qc, kc, vc, sink):
    B, Lc = qc.shape[:2]
    scale = AT_HD ** -0.5
    s = jnp.einsum('bqgjd,bkgd->bgjqk', qc, kc) * scale
    sink_b = jnp.broadcast_to(sink[None, :, :, None, None], (B, AT_HKV, AT_G, Lc, 1))
    p = jax.nn.softmax(jnp.concatenate([s, sink_b], axis=-1), axis=-1)[..., :Lc]
    return jnp.einsum('bgjqk,bkgd->bqgjd', p, vc).reshape(B, Lc, AT_W)


def token_mixer(hl, hc, lp, need_ctx):
    B, T, _ = hl.shape
    Lc = hc.shape[1]
    zl = proj(hl, lp['w_in'])
    zc = proj(hc, lp['w_in'])
    sl = short_conv3(zl[..., :CONV_COLS], lp['conv_w'], lp['conv_b'])
    sc = short_conv3(zc[..., :CONV_COLS], lp['conv_w'], lp['conv_b'])
    filt_p = (lp['hy_w1'], lp['hy_b1'], lp['hy_w2'], lp['hy_b2'], lp['hy_w3'], lp['hy_b3'], lp['hy_freq'])
    hy_l = hyena_mix(sl[..., OFF_HY:OFF_RK], hyena_filter(T, *filt_p), lp['hy_bias'])
    rw_c, rw_l = rwkv7_mix(sc[..., OFF_RK:], sl[..., OFF_RK:], zc[..., OFF_LORA:OFF_AT], zl[..., OFF_LORA:OFF_AT], lp)
    qc, kc, vc = split_attn(zc)
    sink = lp['attn_sink'].reshape(AT_HKV, AT_G)
    off_k = OFF_AT + AT_W
    off_v = off_k + AT_KVW
    at_l = windowed_attention(zl[..., OFF_AT:off_k], zl[..., off_k:off_v], zl[..., off_v:],
                              zc[..., off_k:off_v], zc[..., off_v:], lp['attn_sink'])
    mix_l = jnp.concatenate([rmsnorm(hy_l, lp['hy_out_g']), rw_l, rmsnorm(at_l, lp['at_out_g'])], axis=-1)
    out_l = proj(mix_l, lp['w_out'])
    if not need_ctx:
        return out_l, None
    hy_c = hyena_mix(sc[..., OFF_HY:OFF_RK], hyena_filter(Lc, *filt_p), lp['hy_bias'])
    at_c = context_attention(qc.reshape(B, Lc, AT_HKV, AT_G, AT_HD), kc, vc, sink)
    mix_c = jnp.concatenate([rmsnorm(hy_c, lp['hy_out_g']), rw_c, rmsnorm(at_c, lp['at_out_g'])], axis=-1)
    out_c = proj(mix_c, lp['w_out'])
    return out_l, out_c


def _top16_rows(s, n_rows):
    rio = lax.broadcasted_iota(jnp.int32, s.shape, 0)
    vals, idxs = [], []
    for _ in range(PEER_TOPK):
        m = jnp.max(s, axis=0, keepdims=True)
        idx = jnp.min(jnp.where(s == m, rio, n_rows), axis=0, keepdims=True)
        vals.append(m)
        idxs.append(idx)
        s = jnp.where(rio == idx, -jnp.inf, s)
    return jnp.concatenate(vals, axis=0), jnp.concatenate(idxs, axis=0)


LANE = 128


def _cand_layout():
    rows = [(0, b) for b in range(PEER_TOPK)]
    for a in range(1, 8):
        rows += [(a, b) for b in range(8)]
    rows += [(a, 0) for a in range(8, PEER_TOPK)]
    a = np.array([r[0] for r in rows])
    b = np.array([r[1] for r in rows])
    return a, b, (a + 1) * (b + 1) <= PEER_TOPK


def cand_table():
    a, b, valid = _cand_layout()
    cio = np.where(valid, a * PEER_TOPK + b, PEER_TOPK * PEER_TOPK).astype(np.int32)
    return jnp.asarray(np.broadcast_to(cio[:, None], (cio.shape[0], LANE)))


def _route_kernel(x_ref, wq_ref, k1_ref, k2_ref, cio_ref, i1_ref, i2_ref, g_ref, q_ref, *, tt):
    q = jnp.dot(x_ref[...].astype(jnp.bfloat16), wq_ref[...], preferred_element_type=jnp.float32)
    q_ref[...] = q.astype(jnp.bfloat16)
    nt = (((1,), (1,)), ((), ()))
    cio = cio_ref[...]
    ncand = PEER_TOPK * PEER_TOPK

    def chunk(c, carry):
        t0 = pl.multiple_of(c * LANE, LANE)
        qc = q_ref[pl.ds(t0, LANE), :]
        s1 = lax.dot_general(k1_ref[0], qc[:, :PEER_DQ // 2], nt, preferred_element_type=jnp.float32)
        s2 = lax.dot_general(k2_ref[0], qc[:, PEER_DQ // 2:], nt, preferred_element_type=jnp.float32)
        v1, j1 = _top16_rows(s1, PEER_NKEYS)
        v2, j2 = _top16_rows(s2, PEER_NKEYS)
        j1 = j1 * PEER_NKEYS
        cand = jnp.concatenate([v1[0:1] + v2] + [v1[a:a + 1] + v2[0:8] for a in range(1, 8)]
                               + [v1[8:16] + v2[0:1]], axis=0)
        ecand = jnp.concatenate([j1[0:1] + j2] + [j1[a:a + 1] + j2[0:8] for a in range(1, 8)]
                                + [j1[8:16] + j2[0:1]], axis=0)
        cand = jnp.where(cio < ncand, cand, -jnp.inf)
        scs, es = [], []
        for _ in range(PEER_TOPK):
            m = jnp.max(cand, axis=0, keepdims=True)
            cidx = jnp.min(jnp.where(cand == m, cio, ncand), axis=0, keepdims=True)
            sel = cio == cidx
            es.append(jnp.max(jnp.where(sel, ecand, -1), axis=0, keepdims=True))
            scs.append(m)
            cand = jnp.where(sel, -jnp.inf, cand)
        sc = jnp.concatenate(scs, axis=0)
        e = jnp.concatenate(es, axis=0)
        p = jnp.exp(sc - sc[0:1])
        i1_ref[0, :, pl.ds(t0, LANE)] = e >> 7
        i2_ref[0, :, pl.ds(t0, LANE)] = e & (PEER_NKEYS - 1)
        g_ref[0, :, pl.ds(t0, LANE)] = p / jnp.sum(p, axis=0, keepdims=True)
        return carry

    lax.fori_loop(0, tt // LANE, chunk, 0, unroll=2)


def peer_route(h, wq_bf, k1_bf, k2_bf, tt):
    n, d = h.shape
    assert n % tt == 0 and tt % (2 * LANE) == 0
    cio = cand_table()
    out_sds = [jax.ShapeDtypeStruct((PEER_HEADS, PEER_TOPK, n), jnp.int32),
               jax.ShapeDtypeStruct((PEER_HEADS, PEER_TOPK, n), jnp.int32),
               jax.ShapeDtypeStruct((PEER_HEADS, PEER_TOPK, n), jnp.float32)]
    ospec = pl.BlockSpec((1, PEER_TOPK, tt), lambda i, hh: (hh, 0, i))
    return pl.pallas_call(
        functools.partial(_route_kernel, tt=tt),
        grid=(n // tt, PEER_HEADS),
        in_specs=[pl.BlockSpec((tt, d), lambda i, hh: (i, 0)),
                  pl.BlockSpec((d, PEER_DQ), lambda i, hh: (0, hh)),
                  pl.BlockSpec((1, PEER_NKEYS, PEER_DQ // 2), lambda i, hh: (hh, 0, 0)),
                  pl.BlockSpec((1, PEER_NKEYS, PEER_DQ // 2), lambda i, hh: (hh, 0, 0)),
                  pl.BlockSpec(cio.shape, lambda i, hh: (0, 0))],
        out_specs=[ospec, ospec, ospec],
        out_shape=out_sds,
        scratch_shapes=[pltpu.VMEM((tt, PEER_DQ), jnp.bfloat16)],
        compiler_params=pltpu.CompilerParams(dimension_semantics=("parallel", "arbitrary"),
                                             vmem_limit_bytes=VMEM_LIMIT),
        name="peer_route",
    )(h, wq_bf, k1_bf, k2_bf, cio)


def _gelu(x):
    return 0.5 * x * (1.0 + lax.erf(x * (2.0 ** -0.5)))


PEER_BUILD_GROUP = 8


def _expert_kernel(x_ref, i1_ref, i2_ref, g_ref, ut_ref, v_ref, o_ref, w_ref, rows_ref, xb_ref, *, tt, eb):
    j = pl.program_id(1)
    nk = PEER_NKEYS
    grp = PEER_BUILD_GROUP

    @pl.when(j == 0)
    def _build():
        xb_ref[...] = x_ref[...].astype(jnp.bfloat16)
        rows_ref[0] = i1_ref[...].reshape(nk, tt).astype(jnp.float32).T
        rows_ref[1] = i2_ref[...].reshape(nk, tt).astype(jnp.float32).T
        rows_ref[2] = g_ref[...].reshape(nk, tt).T
        kio = lax.broadcasted_iota(jnp.int32, (nk, nk), 0).astype(jnp.float32)
        nt = (((1,), (1,)), ((), ()))

        def body(gidx, carry):
            t0 = pl.multiple_of(gidx * grp, grp)
            tiles = []
            for s in range(grp):
                i1row = jnp.broadcast_to(rows_ref[0, pl.ds(t0 + s, 1), :], (nk, nk))
                i2row = jnp.broadcast_to(rows_ref[1, pl.ds(t0 + s, 1), :], (nk, nk))
                grow = jnp.broadcast_to(rows_ref[2, pl.ds(t0 + s, 1), :], (nk, nk))
                m1 = jnp.where(i1row == kio, grow, 0.0)
                hi = m1.astype(jnp.bfloat16)
                lo = (m1 - hi.astype(jnp.float32)).astype(jnp.bfloat16)
                p2 = jnp.where(i2row == kio, 1.0, 0.0).astype(jnp.bfloat16)
                lhs = jnp.concatenate([hi, lo], axis=1)
                rhs = jnp.concatenate([p2, p2], axis=1)
                tiles.append(lax.dot_general(lhs, rhs, nt, preferred_element_type=jnp.float32))
            w_ref[:, pl.ds(t0, grp), :] = jnp.swapaxes(jnp.stack(tiles, axis=0), 0, 1)
            return carry

        lax.fori_loop(0, tt // grp, body, 0)

    a = jnp.dot(xb_ref[...], ut_ref[...], preferred_element_type=jnp.float32)
    nrow = eb // nk
    wj = jnp.concatenate([w_ref[j * nrow + r] for r in range(nrow)], axis=1)
    cmat = (wj * _gelu(a)).astype(jnp.bfloat16)
    contrib = jnp.dot(cmat, v_ref[...], preferred_element_type=jnp.float32)

    @pl.when(j == 0)
    def _first():
        o_ref[...] = contrib

    @pl.when(j > 0)
    def _rest():
        o_ref[...] += contrib


def peer_experts(h, i1, i2, g, ut_bf, v_bf, tt, eb):
    n, d = h.shape
    nexp = v_bf.shape[0]
    assert n % tt == 0 and nexp % eb == 0 and nexp == PEER_NKEYS * PEER_NKEYS
    rspec = pl.BlockSpec((PEER_HEADS, PEER_TOPK, tt), lambda i, j: (0, 0, i))
    return pl.pallas_call(
        functools.partial(_expert_kernel, tt=tt, eb=eb),
        grid=(n // tt, nexp // eb),
        in_specs=[pl.BlockSpec((tt, d), lambda i, j: (i, 0)), rspec, rspec, rspec,
                  pl.BlockSpec((d, eb), lambda i, j: (0, j)),
                  pl.BlockSpec((eb, d), lambda i, j: (j, 0))],
        out_specs=pl.BlockSpec((tt, d), lambda i, j: (i, 0)),
        out_shape=jax.ShapeDtypeStruct((n, d), jnp.float32),
        scratch_shapes=[pltpu.VMEM((PEER_NKEYS, tt, PEER_NKEYS), jnp.float32),
                        pltpu.VMEM((3, tt, PEER_NKEYS), jnp.float32),
                        pltpu.VMEM((tt, d), jnp.bfloat16)],
        compiler_params=pltpu.CompilerParams(dimension_semantics=("parallel", "arbitrary"),
                                             vmem_limit_bytes=VMEM_LIMIT),
        name="peer_experts",
    )(h, i1, i2, g, ut_bf, v_bf)


PEER_ROUTE_TILE = 512
PEER_TOKEN_TILE = 256
PEER_EXPERT_BLOCK = 1024


def peer_ffn(h, wq, k1, k2, u, v):
    bf = jnp.bfloat16
    i1, i2, g = peer_route(h, wq.astype(bf), k1.astype(bf), k2.astype(bf), PEER_ROUTE_TILE)
    return peer_experts(h, i1, i2, g, u.T.astype(bf), v.astype(bf), PEER_TOKEN_TILE, PEER_EXPERT_BLOCK)


def kernel(x, c, ctx, c_ctx, mod_w, mod_b, norm1_g, norm2_g, w_in, conv_w, conv_b, hy_w1, hy_b1, hy_w2, hy_b2, hy_w3, hy_b3, hy_freq, hy_bias, rw_w0, rw_w_up, rw_a0, rw_a_up, rw_g_up, rw_k_k, rw_k_a, rw_r_k, rw_ln_w, rw_ln_b, attn_sink, hy_out_g, at_out_g, w_out, peer_wq, peer_k1, peer_k2, peer_u, peer_v, final_g):
    B, T, D = x.shape
    Lc = ctx.shape[1]
    c_s = jax.nn.silu(c)
    cc_s = jax.nn.silu(c_ctx)
    xl, xc = x, ctx
    for i in range(DEPTH):
        last = i == DEPTH - 1
        lp = dict(w_in=w_in[i], conv_w=conv_w[i], conv_b=conv_b[i],
                  hy_w1=hy_w1[i], hy_b1=hy_b1[i], hy_w2=hy_w2[i], hy_b2=hy_b2[i],
                  hy_w3=hy_w3[i], hy_b3=hy_b3[i], hy_freq=hy_freq[i], hy_bias=hy_bias[i],
                  rw_w0=rw_w0[i], rw_w_up=rw_w_up[i], rw_a0=rw_a0[i], rw_a_up=rw_a_up[i],
                  rw_g_up=rw_g_up[i], rw_k_k=rw_k_k[i], rw_k_a=rw_k_a[i], rw_r_k=rw_r_k[i],
                  rw_ln_w=rw_ln_w[i], rw_ln_b=rw_ln_b[i], attn_sink=attn_sink[i],
                  hy_out_g=hy_out_g[i], at_out_g=at_out_g[i], w_out=w_out[i])
        sh1, sc1, g1, sh2, sc2, g2 = jnp.split((c_s @ mod_w[i] + mod_b[i])[:, None, :], 6, axis=-1)
        csh1, csc1, cg1, csh2, csc2, cg2 = jnp.split(cc_s @ mod_w[i] + mod_b[i], 6, axis=-1)
        hl = rmsnorm(xl, norm1_g[i]) * (1.0 + sc1) + sh1
        hc = rmsnorm(xc, norm1_g[i]) * (1.0 + csc1) + csh1
        ml, mc = token_mixer(hl, hc, lp, not last)
        xl = xl + g1 * ml
        h2l = rmsnorm(xl, norm2_g[i]) * (1.0 + sc2) + sh2
        if last:
            fl = peer_ffn(h2l.reshape(B * T, D), peer_wq[i], peer_k1[i], peer_k2[i], peer_u[i], peer_v[i]).reshape(B, T, D)
        else:
            xc = xc + cg1 * mc
            h2c = rmsnorm(xc, norm2_g[i]) * (1.0 + csc2) + csh2
            tok = jnp.concatenate([h2l.reshape(B * T, D), h2c.reshape(B * Lc, D)], axis=0)
            f = peer_ffn(tok, peer_wq[i], peer_k1[i], peer_k2[i], peer_u[i], peer_v[i])
            fl = f[:B * T].reshape(B, T, D)
            xc = xc + cg2 * f[B * T:].reshape(B, Lc, D)
        xl = xl + g2 * fl
    return rmsnorm(xl, final_g)
```

```python
import functools
import math

import jax
import jax.numpy as jnp
import numpy as np
from jax import lax
from jax.experimental import pallas as pl
from jax.experimental.pallas import tpu as pltpu

D_MODEL = 1024
DEPTH = 4
GRID_W = 64

HY_CH = 256
RW_HEADS = 4
RW_HD = 64
RW_W = RW_HEADS * RW_HD
AT_HQ = 8
AT_HKV = 2
AT_G = AT_HQ // AT_HKV
AT_HD = 64
AT_W = AT_HQ * AT_HD
AT_KVW = AT_HKV * AT_HD

RW_DECAY_R = 64
RW_ICLR_R = 64
RW_GATE_R = 128
RW_LORA_W = 2 * RW_DECAY_R + 2 * RW_ICLR_R + RW_GATE_R
RW_GN_EPS = RW_HD * 1e-5

HY_EMB = 33
HY_BANDS = (HY_EMB - 1) // 2
HY_SHIFT = 0.05
HY_MIN_DECAY = math.log(1e-2) / 1.5
HY_MAX_DECAY = math.log(1e-2) / 0.3

ATT_WINDOW = 128
ATT_BLOCK = 128
ROPE_BASE = 10000.0
NEG_INF = -1e30

PEER_HEADS = 8
PEER_NKEYS = 128
PEER_DQ = 256
PEER_TOPK = 16
PEER_CHUNK = 128

OFF_HY = 0
OFF_RK = OFF_HY + 3 * HY_CH
OFF_LORA = OFF_RK + 3 * RW_W
OFF_AT = OFF_LORA + RW_LORA_W
CONV_COLS = OFF_LORA

VMEM_LIMIT = 48 * 1024 * 1024


def _mm_kernel(x_ref, w_ref, o_ref):
    o_ref[...] = jnp.dot(x_ref[...].astype(jnp.bfloat16), w_ref[...],
                         preferred_element_type=jnp.float32)


def matmul(x, w, tm=512):
    m, k = x.shape
    n = w.shape[1]
    assert m % tm == 0
    return pl.pallas_call(
        _mm_kernel,
        grid=(m // tm,),
        in_specs=[pl.BlockSpec((tm, k), lambda i: (i, 0)),
                  pl.BlockSpec((k, n), lambda i: (0, 0))],
        out_specs=pl.BlockSpec((tm, n), lambda i: (i, 0)),
        out_shape=jax.ShapeDtypeStruct((m, n), jnp.float32),
        compiler_params=pltpu.CompilerParams(dimension_semantics=("parallel",),
                                             vmem_limit_bytes=VMEM_LIMIT),
        name="matmul",
    )(x, w.astype(jnp.bfloat16))


def proj(x3, w):
    b, l, d = x3.shape
    return matmul(x3.reshape(b * l, d), w).reshape(b, l, w.shape[1])


def rmsnorm(x, g, eps=1e-6):
    y = x * lax.rsqrt(jnp.mean(x * x, axis=-1, keepdims=True) + eps)
    return y * g


def short_conv3(u, w, b):
    up = jnp.pad(u, ((0, 0), (1, 1), (0, 0)))
    return up[:, :-2] * w[0] + up[:, 1:-1] * w[1] + up[:, 2:] * w[2] + b


def hyena_filter(L, w1, b1, w2, b2, w3, b3, freq):
    f32 = jnp.float32
    j = jnp.arange(L, dtype=f32)
    t = j / max(L - 1, 1)
    bands = jnp.linspace(1e-4, HY_BANDS - 1, HY_BANDS, dtype=f32)
    ang = 2.0 * math.pi * j[:, None] * bands[None, :] / L
    z = jnp.concatenate([t[:, None], jnp.cos(ang), -jnp.sin(ang)], axis=-1)
    h = jnp.sin(freq[0] * (z @ w1 + b1))
    h = jnp.sin(freq[1] * (h @ w2 + b2))
    h = (h @ w3 + b3).astype(f32).reshape(L, 2, HY_CH)
    deltas = jnp.abs(jnp.linspace(HY_MIN_DECAY, HY_MAX_DECAY, HY_CH, dtype=f32))
    window = jnp.exp(-t[:, None] * deltas[None, :]) + HY_SHIFT
    h = h * window[:, None, :]
    k = jnp.concatenate([h[:, 0], jnp.zeros((1, HY_CH), f32), h[:0:-1, 1]], axis=0)
    return k * lax.rsqrt(jnp.sum(k * k, axis=0, keepdims=True) + 1e-6)


HY_BLOCK = 256


def _hyena_kernel(fr_ref, u_ref, y_ref, *, nblk, nb):
    P = HY_BLOCK
    ncol = nblk * nb
    two_l = 2 * nblk * P
    u = u_ref[0].astype(jnp.bfloat16)
    col = lax.broadcasted_iota(jnp.int32, (P, ncol), 1)
    acc = jnp.zeros((P, ncol), jnp.float32)
    for m in range(-(nblk - 1), nblk):
        start = (-P * m) % two_l
        gr = fr_ref[0, :, start:start + 2 * P]
        rolled = pltpu.roll(jnp.broadcast_to(gr, (P, 2 * P)), P + 1, 1, stride=1, stride_axis=0)
        t_m = rolled[:, :P].astype(jnp.bfloat16)
        prod = jnp.dot(t_m, u, preferred_element_type=jnp.float32)
        if m == 0:
            acc = acc + prod
        elif m > 0:
            acc = acc + jnp.where(col >= nb * m, pltpu.roll(prod, nb * m, 1), 0.0)
        else:
            acc = acc + jnp.where(col < ncol + nb * m, pltpu.roll(prod, ncol + nb * m, 1), 0.0)
    y_ref[0] = acc


def hyena_conv(u, filt):
    B, L, C = u.shape
    P = HY_BLOCK
    assert L % P == 0 and filt.shape == (2 * L, C)
    nblk = L // P
    idx = (P - 1 - np.arange(2 * L + 2 * P)) % (2 * L)
    fr = filt[idx].T.reshape(C, 1, 2 * L + 2 * P)
    uc = u.reshape(B, nblk, P, C).transpose(3, 2, 1, 0).reshape(C, P, nblk * B)
    y = pl.pallas_call(
        functools.partial(_hyena_kernel, nblk=nblk, nb=B),
        grid=(C,),
        in_specs=[pl.BlockSpec((1, 1, 2 * L + 2 * P), lambda c: (c, 0, 0)),
                  pl.BlockSpec((1, P, nblk * B), lambda c: (c, 0, 0))],
        out_specs=pl.BlockSpec((1, P, nblk * B), lambda c: (c, 0, 0)),
        out_shape=jax.ShapeDtypeStruct((C, P, nblk * B), jnp.float32),
        compiler_params=pltpu.CompilerParams(dimension_semantics=("parallel",), vmem_limit_bytes=VMEM_LIMIT),
        name="hyena_conv",
    )(fr, uc)
    return y.reshape(C, P, nblk, B).transpose(3, 2, 1, 0).reshape(B, L, C)


def hyena_mix(z, filt, d_bias):
    x0 = z[..., :HY_CH]
    x1 = z[..., HY_CH:2 * HY_CH]
    v = z[..., 2 * HY_CH:]
    u = x1 * v
    y = hyena_conv(u, filt)
    return x0 * (y + u * d_bias)


RW_LANES = 128
RW_TCHUNK = 32


def _rwkv_scan_kernel(rf, wf, kf, vf, af, bf_, rb, wb, kb, vb, ab, bb, yf_ref, yb_ref, s_ref, *, tc):
    @pl.when(pl.program_id(0) == 0)
    def _init():
        s_ref[...] = jnp.zeros_like(s_ref)

    fwd = lax.broadcasted_iota(jnp.int32, (RW_HD, RW_LANES), 1) < RW_LANES // 2
    fwd_row = fwd[0:1]

    def step(t, carry):
        tb = tc - 1 - t
        a_t = jnp.where(fwd, af[t], ab[tb])
        w_t = jnp.where(fwd, wf[t], wb[tb])
        k_t = jnp.where(fwd, kf[t], kb[tb])
        b_t = jnp.where(fwd, bf_[t], bb[tb])
        r_t = jnp.where(fwd, rf[t], rb[tb])
        for v in range(RW_HD):
            sv = s_ref[v]
            sa = jnp.sum(sv * a_t, axis=0, keepdims=True)
            vv = jnp.where(fwd_row, vf[t, pl.ds(v, 1), :], vb[tb, pl.ds(v, 1), :])
            sv = sv * w_t + sa * b_t + vv * k_t
            s_ref[v] = sv
            y = jnp.sum(sv * r_t, axis=0, keepdims=True)
            yf_ref[t, pl.ds(v, 1), :] = y
            yb_ref[tb, pl.ds(v, 1), :] = y
        return carry

    lax.fori_loop(0, tc, step, 0)


def rwkv_scan_lanes(r, w, k, v, a, b, n_ctx):
    L = r.shape[0]
    tc = RW_TCHUNK
    assert L % tc == 0 and n_ctx % tc == 0 and r.shape[1:] == (RW_HD, RW_LANES)
    nblk, cblk = L // tc, n_ctx // tc

    def fmap(i):
        return (i, 0, 0)

    def bmap(i):
        return (jnp.where(i < cblk, cblk - 1 - i, nblk - 1 + cblk - i), 0, 0)

    fspec = pl.BlockSpec((tc, RW_HD, RW_LANES), fmap)
    bspec = pl.BlockSpec((tc, RW_HD, RW_LANES), bmap)
    sds = jax.ShapeDtypeStruct((L, RW_HD, RW_LANES), jnp.float32)
    return pl.pallas_call(
        functools.partial(_rwkv_scan_kernel, tc=tc),
        grid=(nblk,),
        in_specs=[fspec] * 6 + [bspec] * 6,
        out_specs=[fspec, bspec],
        out_shape=[sds, sds],
        scratch_shapes=[pltpu.VMEM((RW_HD, RW_HD, RW_LANES), jnp.float32)],
        compiler_params=pltpu.CompilerParams(dimension_semantics=("arbitrary",),
                                             vmem_limit_bytes=VMEM_LIMIT),
        name="rwkv_scan",
    )(r, w, k, v, a, b, r, w, k, v, a, b)


def rwkv7_mix(rkv_c, rkv_l, lora_c, lora_l, lp):
    f32 = jnp.float32
    B, Lc, _ = rkv_c.shape
    T = rkv_l.shape[1]
    Lt = Lc + T
    hs = (B, Lt, RW_HEADS, RW_HD)
    rkv = jnp.concatenate([rkv_c, rkv_l], axis=1)
    lora = jnp.concatenate([lora_c, lora_l], axis=1)
    r = rkv[..., :RW_W].reshape(hs)
    k = rkv[..., RW_W:2 * RW_W].reshape(hs)
    v = rkv[..., 2 * RW_W:].reshape(hs)
    kk = k * lp['rw_k_k'].reshape(RW_HEADS, RW_HD)
    kk = kk * lax.rsqrt(jnp.sum(kk * kk, axis=-1, keepdims=True) + 1e-12)
    k_a = lp['rw_k_a'].reshape(RW_HEADS, RW_HD)
    r_k = lp['rw_r_k']
    assert 2 * B * RW_HEADS == RW_LANES
    bonus = jnp.zeros(hs, f32)
    per_dir = []
    for d in range(2):
        wd = lora[..., d * RW_DECAY_R:(d + 1) * RW_DECAY_R]
        ad = lora[..., 2 * RW_DECAY_R + d * RW_ICLR_R:2 * RW_DECAY_R + (d + 1) * RW_ICLR_R]
        w_log = -jax.nn.softplus(-(lp['rw_w0'][d] + jnp.tanh(wd) @ lp['rw_w_up'][d])) - 0.5
        decay = jnp.exp(-jnp.exp(w_log)).reshape(hs)
        a = jax.nn.sigmoid(lp['rw_a0'][d] + ad @ lp['rw_a_up'][d]).reshape(hs)
        k_d = k * (1.0 + (a - 1.0) * k_a)
        per_dir.append((r, decay, k_d, v, -kk, kk * a))
        bonus = bonus + jnp.sum(r * k_d * r_k, axis=-1, keepdims=True) * v

    def lanes(xf, xb):
        return jnp.concatenate([x.transpose(1, 3, 0, 2).reshape(Lt, RW_HD, B * RW_HEADS) for x in (xf, xb)], axis=-1)

    yf, yb = rwkv_scan_lanes(*(lanes(xf, xb) for xf, xb in zip(*per_dir)), n_ctx=Lc)
    half = RW_LANES // 2
    y_sum = (yf[..., :half] + yb[..., half:]).reshape(Lt, RW_HD, B, RW_HEADS).transpose(2, 0, 3, 1)
    mu = jnp.mean(y_sum, axis=-1, keepdims=True)
    var = jnp.mean(jnp.square(y_sum - mu), axis=-1, keepdims=True)
    yn = (y_sum - mu) * lax.rsqrt(var + RW_GN_EPS)
    yn = yn * lp['rw_ln_w'].reshape(RW_HEADS, RW_HD) + lp['rw_ln_b'].reshape(RW_HEADS, RW_HD)
    g = jax.nn.sigmoid(lora[..., 2 * RW_DECAY_R + 2 * RW_ICLR_R:]) @ lp['rw_g_up']
    out = (yn + bonus).reshape(B, Lt, RW_W) * g
    return out[:, :Lc], out[:, Lc:]


def rope_tables(T):
    f32 = jnp.float32
    quarter = AT_HD // 4
    inv = ROPE_BASE ** (-jnp.arange(quarter, dtype=f32) / quarter)
    rows = T // GRID_W
    row = jnp.repeat(jnp.arange(rows), GRID_W).astype(f32)
    col = jnp.tile(jnp.arange(GRID_W), rows).astype(f32)
    out_c, out_s = [], []
    for pos in (row, col):
        ang = pos[:, None] * inv[None, :]
        c, s = jnp.cos(ang), jnp.sin(ang)
        out_c += [c, c]
        out_s += [-s, s]
    return jnp.concatenate(out_c, axis=1), jnp.concatenate(out_s, axis=1)


def _rope(x, c, s):
    lane = lax.broadcasted_iota(jnp.int32, x.shape, 1)
    first = (lane % 32) < 16
    partner = jnp.where(first, pltpu.roll(x, x.shape[1] - 16, axis=1), pltpu.roll(x, 16, axis=1))
    return x * c + partner * s


def _attn_kernel(q_ref, kp_ref, kc_ref, kn_ref, vp_ref, vc_ref, vn_ref, kx_ref, vx_ref, sink_ref,
                 cq_ref, sq_ref, ckp_ref, skp_ref, ckc_ref, skc_ref, ckn_ref, skn_ref, o_ref, *, nblk):
    i = pl.program_id(1)
    blk = ATT_BLOCK
    scale = AT_HD ** -0.5
    q = _rope(q_ref[0], cq_ref[...], sq_ref[...])
    kspan = jnp.concatenate([_rope(kp_ref[0], ckp_ref[...], skp_ref[...]),
                             _rope(kc_ref[0], ckc_ref[...], skc_ref[...]),
                             _rope(kn_ref[0], ckn_ref[...], skn_ref[...])], axis=0)
    vspan = jnp.concatenate([vp_ref[0], vc_ref[0], vn_ref[0]], axis=0)
    kx = kx_ref[0]
    vx = vx_ref[0]
    qpos = lax.broadcasted_iota(jnp.int32, (blk, 3 * blk), 0) + blk
    kpos = lax.broadcasted_iota(jnp.int32, (blk, 3 * blk), 1)
    valid = jnp.abs(qpos - kpos) <= ATT_WINDOW
    valid = valid & ((kpos >= blk) | (i > 0)) & ((kpos < 2 * blk) | (i < nblk - 1))
    valid4 = jnp.concatenate([valid] * AT_G, axis=0)
    nt = (((1,), (1,)), ((), ()))
    bf = jnp.bfloat16
    outs = []
    for g in range(AT_HKV):
        kg = kspan[:, g * AT_HD:(g + 1) * AT_HD].astype(bf)
        vg = vspan[:, g * AT_HD:(g + 1) * AT_HD].astype(bf)
        kxg = kx[:, g * AT_HD:(g + 1) * AT_HD].astype(bf)
        vxg = vx[:, g * AT_HD:(g + 1) * AT_HD].astype(bf)
        qg = jnp.concatenate([q[:, (g * AT_G + j) * AT_HD:(g * AT_G + j + 1) * AT_HD] for j in range(AT_G)],
                             axis=0).astype(bf)
        sk = jnp.concatenate([jnp.full((blk, 1), sink_ref[g * AT_G + j], jnp.float32) for j in range(AT_G)], axis=0)
        s_lat = lax.dot_general(qg, kg, nt, preferred_element_type=jnp.float32) * scale
        s_lat = jnp.where(valid4, s_lat, NEG_INF)
        s_ctx = lax.dot_general(qg, kxg, nt, preferred_element_type=jnp.float32) * scale
        m = jnp.maximum(jnp.maximum(jnp.max(s_lat, axis=1, keepdims=True), jnp.max(s_ctx, axis=1, keepdims=True)), sk)
        p_lat = jnp.exp(s_lat - m)
        p_ctx = jnp.exp(s_ctx - m)
        den = jnp.sum(p_lat, axis=1, keepdims=True) + jnp.sum(p_ctx, axis=1, keepdims=True) + jnp.exp(sk - m)
        inv = 1.0 / den
        o = (jnp.dot((p_lat * inv).astype(bf), vg, preferred_element_type=jnp.float32)
             + jnp.dot((p_ctx * inv).astype(bf), vxg, preferred_element_type=jnp.float32))
        outs += [o[j * blk:(j + 1) * blk] for j in range(AT_G)]
    o_ref[0] = jnp.concatenate(outs, axis=1)


def windowed_attention(q, k, v, kx, vx, sink):
    B, T, _ = q.shape
    Lc = kx.shape[1]
    assert T % ATT_BLOCK == 0 and ATT_WINDOW == ATT_BLOCK
    nblk = T // ATT_BLOCK
    c1, s1 = rope_tables(T)
    cq, sq = jnp.tile(c1, (1, AT_HQ)), jnp.tile(s1, (1, AT_HQ))
    ck, sk = jnp.tile(c1, (1, AT_HKV)), jnp.tile(s1, (1, AT_HKV))

    def prev(b, i):
        return (b, jnp.maximum(i - 1, 0), 0)

    def cur(b, i):
        return (b, i, 0)

    def nxt(b, i):
        return (b, jnp.minimum(i + 1, nblk - 1), 0)

    def kb(f):
        return pl.BlockSpec((1, ATT_BLOCK, AT_KVW), f)

    def tb(f):
        return pl.BlockSpec((ATT_BLOCK, AT_KVW), lambda b, i: f(b, i)[1:])

    return pl.pallas_call(
        functools.partial(_attn_kernel, nblk=nblk),
        grid=(B, nblk),
        in_specs=[pl.BlockSpec((1, ATT_BLOCK, AT_W), cur), kb(prev), kb(cur), kb(nxt), kb(prev), kb(cur), kb(nxt),
                  pl.BlockSpec((1, Lc, AT_KVW), lambda b, i: (b, 0, 0)),
                  pl.BlockSpec((1, Lc, AT_KVW), lambda b, i: (b, 0, 0)),
                  pl.BlockSpec(memory_space=pltpu.SMEM),
                  pl.BlockSpec((ATT_BLOCK, AT_W), lambda b, i: (i, 0)),
                  pl.BlockSpec((ATT_BLOCK, AT_W), lambda b, i: (i, 0)),
                  tb(prev), tb(prev), tb(cur), tb(cur), tb(nxt), tb(nxt)],
        out_specs=pl.BlockSpec((1, ATT_BLOCK, AT_W), cur),
        out_shape=jax.ShapeDtypeStruct((B, T, AT_W), jnp.float32),
        compiler_params=pltpu.CompilerParams(dimension_semantics=("parallel", "arbitrary"),
                                             vmem_limit_bytes=VMEM_LIMIT),
        name="window_attention",
    )(q, k, k, k, v, v, v, kx, vx, sink, cq, sq, ck, sk, ck, sk, ck, sk)


def split_attn(z):
    B, L, _ = z.shape
    a = z[..., OFF_AT:]
    q = a[..., :AT_W].reshape(B, L, AT_HQ, AT_HD)
    k = a[..., AT_W:AT_W + AT_KVW].reshape(B, L, AT_HKV, AT_HD)
    v = a[..., AT_W + AT_KVW:].reshape(B, L, AT_HKV, AT_HD)
    return q, k, v


def context_attention(qc, kc, vc, sink):
    B, Lc = qc.shape[:2]
    scale = AT_HD ** -0.5
    s = jnp.einsum('bqgjd,bkgd->bgjqk', qc, kc) * scale
    sink_b = jnp.broadcast_to(sink[None, :, :, None, None], (B, AT_HKV, AT_G, Lc, 1))
    p = jax.nn.softmax(jnp.concatenate([s, sink_b], axis=-1), axis=-1)[..., :Lc]
    return jnp.einsum('bgjqk,bkgd->bqgjd', p, vc).reshape(B, Lc, AT_W)


def token_mixer(hl, hc, lp, need_ctx):
    B, T, _ = hl.shape
    Lc = hc.shape[1]
    zl = proj(hl, lp['w_in'])
    zc = proj(hc, lp['w_in'])
    sl = short_conv3(zl[..., :CONV_COLS], lp['conv_w'], lp['conv_b'])
    sc = short_conv3(zc[..., :CONV_COLS], lp['conv_w'], lp['conv_b'])
    filt_p = (lp['hy_w1'], lp['hy_b1'], lp['hy_w2'], lp['hy_b2'], lp['hy_w3'], lp['hy_b3'], lp['hy_freq'])
    hy_l = hyena_mix(sl[..., OFF_HY:OFF_RK], hyena_filter(T, *filt_p), lp['hy_bias'])
    rw_c, rw_l = rwkv7_mix(sc[..., OFF_RK:], sl[..., OFF_RK:], zc[..., OFF_LORA:OFF_AT], zl[..., OFF_LORA:OFF_AT], lp)
    qc, kc, vc = split_attn(zc)
    sink = lp['attn_sink'].reshape(AT_HKV, AT_G)
    off_k = OFF_AT + AT_W
    off_v = off_k + AT_KVW
    at_l = windowed_attention(zl[..., OFF_AT:off_k], zl[..., off_k:off_v], zl[..., off_v:],
                              zc[..., off_k:off_v], zc[..., off_v:], lp['attn_sink'])
    parts_l = (hy_l, rw_l, at_l)
    if not need_ctx:
        return parts_l, None
    hy_c = hyena_mix(sc[..., OFF_HY:OFF_RK], hyena_filter(Lc, *filt_p), lp['hy_bias'])
    at_c = context_attention(qc.reshape(B, Lc, AT_HKV, AT_G, AT_HD), kc, vc, sink)
    return parts_l, (hy_c, rw_c, at_c)


NORM_EPS = 1e-6
MIXOUT_ROWS = 512


def _rms(x, g):
    return (x * lax.rsqrt(jnp.mean(x * x, axis=-1, keepdims=True) + NORM_EPS)) * g


def _mixout_kernel(hy_ref, rw_ref, at_ref, x_ref, g1_ref, sc2_ref, sh2_ref, hyg_ref, atg_ref, n2g_ref, w_ref,
                   xo_ref, h2_ref):
    mix = jnp.concatenate([_rms(hy_ref[...], hyg_ref[...]), rw_ref[...], _rms(at_ref[...], atg_ref[...])], axis=1)
    m = jnp.dot(mix.astype(jnp.bfloat16), w_ref[...], preferred_element_type=jnp.float32)
    xn = x_ref[...] + g1_ref[0] * m
    xo_ref[...] = xn
    h2_ref[...] = (_rms(xn, n2g_ref[...]) * (1.0 + sc2_ref[0]) + sh2_ref[0]).astype(h2_ref.dtype)


def mix_out(parts, x, gate1, scale2, shift2, hy_g, at_g, norm2_g, w_out, rows_per_mod):
    hy, rw, at = parts
    n, d = x.shape
    tm = MIXOUT_ROWS
    assert n % tm == 0 and rows_per_mod % tm == 0
    tpm = rows_per_mod // tm

    def row(w):
        return pl.BlockSpec((tm, w), lambda i: (i, 0))

    def vec(w):
        return pl.BlockSpec((1, w), lambda i: (0, 0))

    mod = pl.BlockSpec((1, 1, d), lambda i: (i // tpm, 0, 0))
    return pl.pallas_call(
        _mixout_kernel,
        grid=(n // tm,),
        in_specs=[row(hy.shape[1]), row(rw.shape[1]), row(at.shape[1]), row(d), mod, mod, mod,
                  vec(hy.shape[1]), vec(at.shape[1]), vec(d), pl.BlockSpec(w_out.shape, lambda i: (0, 0))],
        out_specs=[row(d), row(d)],
        out_shape=[jax.ShapeDtypeStruct((n, d), jnp.float32), jax.ShapeDtypeStruct((n, d), jnp.bfloat16)],
        compiler_params=pltpu.CompilerParams(dimension_semantics=("parallel",), vmem_limit_bytes=VMEM_LIMIT),
        name="mix_out",
    )(hy, rw, at, x, gate1.reshape(-1, 1, d), scale2.reshape(-1, 1, d), shift2.reshape(-1, 1, d),
      hy_g.reshape(1, -1), at_g.reshape(1, -1), norm2_g.reshape(1, -1), w_out.astype(jnp.bfloat16))


def _top16_rows(s, n_rows):
    rio = lax.broadcasted_iota(jnp.int32, s.shape, 0)
    vals, idxs = [], []
    for _ in range(PEER_TOPK):
        m = jnp.max(s, axis=0, keepdims=True)
        idx = jnp.min(jnp.where(s == m, rio, n_rows), axis=0, keepdims=True)
        vals.append(m)
        idxs.append(idx)
        s = jnp.where(rio == idx, -jnp.inf, s)
    return jnp.concatenate(vals, axis=0), jnp.concatenate(idxs, axis=0)


LANE = 128


def _cand_layout():
    rows = [(0, b) for b in range(PEER_TOPK)]
    for a in range(1, 8):
        rows += [(a, b) for b in range(8)]
    rows += [(a, 0) for a in range(8, PEER_TOPK)]
    a = np.array([r[0] for r in rows])
    b = np.array([r[1] for r in rows])
    return a, b, (a + 1) * (b + 1) <= PEER_TOPK


def cand_table():
    a, b, valid = _cand_layout()
    cio = np.where(valid, a * PEER_TOPK + b, PEER_TOPK * PEER_TOPK).astype(np.int32)
    return jnp.asarray(np.broadcast_to(cio[:, None], (cio.shape[0], LANE)))


def _route_kernel(x_ref, wq_ref, k1_ref, k2_ref, cio_ref, i1_ref, i2_ref, g_ref, q_ref, *, tt):
    q = jnp.dot(x_ref[...].astype(jnp.bfloat16), wq_ref[...], preferred_element_type=jnp.float32)
    q_ref[...] = q.astype(jnp.bfloat16)
    nt = (((1,), (1,)), ((), ()))
    cio = cio_ref[...]
    ncand = PEER_TOPK * PEER_TOPK

    def chunk(c, carry):
        t0 = pl.multiple_of(c * LANE, LANE)
        qc = q_ref[pl.ds(t0, LANE), :]
        s1 = lax.dot_general(k1_ref[0], qc[:, :PEER_DQ // 2], nt, preferred_element_type=jnp.float32)
        s2 = lax.dot_general(k2_ref[0], qc[:, PEER_DQ // 2:], nt, preferred_element_type=jnp.float32)
        v1, j1 = _top16_rows(s1, PEER_NKEYS)
        v2, j2 = _top16_rows(s2, PEER_NKEYS)
        j1 = j1 * PEER_NKEYS
        cand = jnp.concatenate([v1[0:1] + v2] + [v1[a:a + 1] + v2[0:8] for a in range(1, 8)]
                               + [v1[8:16] + v2[0:1]], axis=0)
        ecand = jnp.concatenate([j1[0:1] + j2] + [j1[a:a + 1] + j2[0:8] for a in range(1, 8)]
                                + [j1[8:16] + j2[0:1]], axis=0)
        cand = jnp.where(cio < ncand, cand, -jnp.inf)
        scs, es = [], []
        for _ in range(PEER_TOPK):
            m = jnp.max(cand, axis=0, keepdims=True)
            cidx = jnp.min(jnp.where(cand == m, cio, ncand), axis=0, keepdims=True)
            sel = cio == cidx
            es.append(jnp.max(jnp.where(sel, ecand, -1), axis=0, keepdims=True))
            scs.append(m)
            cand = jnp.where(sel, -jnp.inf, cand)
        sc = jnp.concatenate(scs, axis=0)
        e = jnp.concatenate(es, axis=0)
        p = jnp.exp(sc - sc[0:1])
        i1_ref[0, :, pl.ds(t0, LANE)] = e >> 7
        i2_ref[0, :, pl.ds(t0, LANE)] = e & (PEER_NKEYS - 1)
        g_ref[0, :, pl.ds(t0, LANE)] = p / jnp.sum(p, axis=0, keepdims=True)
        return carry

    lax.fori_loop(0, tt // LANE, chunk, 0, unroll=2)


def peer_route(h, wq_bf, k1_bf, k2_bf, tt):
    n, d = h.shape
    assert n % tt == 0 and tt % (2 * LANE) == 0
    cio = cand_table()
    out_sds = [jax.ShapeDtypeStruct((PEER_HEADS, PEER_TOPK, n), jnp.int32),
               jax.ShapeDtypeStruct((PEER_HEADS, PEER_TOPK, n), jnp.int32),
               jax.ShapeDtypeStruct((PEER_HEADS, PEER_TOPK, n), jnp.float32)]
    ospec = pl.BlockSpec((1, PEER_TOPK, tt), lambda i, hh: (hh, 0, i))
    return pl.pallas_call(
        functools.partial(_route_kernel, tt=tt),
        grid=(n // tt, PEER_HEADS),
        in_specs=[pl.BlockSpec((tt, d), lambda i, hh: (i, 0)),
                  pl.BlockSpec((d, PEER_DQ), lambda i, hh: (0, hh)),
                  pl.BlockSpec((1, PEER_NKEYS, PEER_DQ // 2), lambda i, hh: (hh, 0, 0)),
                  pl.BlockSpec((1, PEER_NKEYS, PEER_DQ // 2), lambda i, hh: (hh, 0, 0)),
                  pl.BlockSpec(cio.shape, lambda i, hh: (0, 0))],
        out_specs=[ospec, ospec, ospec],
        out_shape=out_sds,
        scratch_shapes=[pltpu.VMEM((tt, PEER_DQ), jnp.bfloat16)],
        compiler_params=pltpu.CompilerParams(dimension_semantics=("parallel", "arbitrary"),
                                             vmem_limit_bytes=VMEM_LIMIT),
        name="peer_route",
    )(h, wq_bf, k1_bf, k2_bf, cio)


def _gelu(x):
    return 0.5 * x * (1.0 + lax.erf(x * (2.0 ** -0.5)))


PEER_BUILD_GROUP = 16
PEER_VMEM_LIMIT = 56 * 1024 * 1024


def _expert_kernel(x_ref, i1_ref, i2_ref, g_ref, ut_ref, v_ref, o_ref, w_ref, rows_ref, xb_ref, *, tt, eb):
    j = pl.program_id(1)
    nk = PEER_NKEYS
    grp = PEER_BUILD_GROUP

    @pl.when(j == 0)
    def _build():
        xb_ref[...] = x_ref[...].astype(jnp.bfloat16)
        rows_ref[0] = i1_ref[...].reshape(nk, tt).astype(jnp.float32).T
        rows_ref[1] = i2_ref[...].reshape(nk, tt).astype(jnp.float32).T
        rows_ref[2] = g_ref[...].reshape(nk, tt).T
        kio = lax.broadcasted_iota(jnp.int32, (nk, nk), 0).astype(jnp.float32)
        nt = (((1,), (1,)), ((), ()))

        def body(gidx, carry):
            t0 = pl.multiple_of(gidx * grp, grp)
            tiles = []
            for s in range(grp):
                i1row = jnp.broadcast_to(rows_ref[0, pl.ds(t0 + s, 1), :], (nk, nk))
                i2row = jnp.broadcast_to(rows_ref[1, pl.ds(t0 + s, 1), :], (nk, nk))
                grow = jnp.broadcast_to(rows_ref[2, pl.ds(t0 + s, 1), :], (nk, nk))
                m1 = jnp.where(i1row == kio, grow, 0.0).astype(jnp.bfloat16)
                p2 = jnp.where(i2row == kio, 1.0, 0.0).astype(jnp.bfloat16)
                tiles.append(lax.dot_general(m1, p2, nt, preferred_element_type=jnp.float32))
            w_ref[:, pl.ds(t0, grp), :] = jnp.swapaxes(jnp.stack(tiles, axis=0), 0, 1).astype(jnp.bfloat16)
            return carry

        lax.fori_loop(0, tt // grp, body, 0)

    a = jnp.dot(xb_ref[...], ut_ref[...], preferred_element_type=jnp.float32)
    nrow = eb // nk
    wj = jnp.concatenate([w_ref[j * nrow + r] for r in range(nrow)], axis=1)
    cmat = (wj.astype(jnp.float32) * _gelu(a)).astype(jnp.bfloat16)
    contrib = jnp.dot(cmat, v_ref[...], preferred_element_type=jnp.float32)

    @pl.when(j == 0)
    def _first():
        o_ref[...] = contrib

    @pl.when(j > 0)
    def _rest():
        o_ref[...] += contrib


def peer_experts(h, i1, i2, g, ut_bf, v_bf, tt, eb):
    n, d = h.shape
    nexp = v_bf.shape[0]
    assert n % tt == 0 and nexp % eb == 0 and nexp == PEER_NKEYS * PEER_NKEYS
    rspec = pl.BlockSpec((PEER_HEADS, PEER_TOPK, tt), lambda i, j: (0, 0, i))
    return pl.pallas_call(
        functools.partial(_expert_kernel, tt=tt, eb=eb),
        grid=(n // tt, nexp // eb),
        in_specs=[pl.BlockSpec((tt, d), lambda i, j: (i, 0)), rspec, rspec, rspec,
                  pl.BlockSpec((d, eb), lambda i, j: (0, j)),
                  pl.BlockSpec((eb, d), lambda i, j: (j, 0))],
        out_specs=pl.BlockSpec((tt, d), lambda i, j: (i, 0)),
        out_shape=jax.ShapeDtypeStruct((n, d), jnp.float32),
        scratch_shapes=[pltpu.VMEM((PEER_NKEYS, tt, PEER_NKEYS), jnp.bfloat16),
                        pltpu.VMEM((3, tt, PEER_NKEYS), jnp.float32),
                        pltpu.VMEM((tt, d), jnp.bfloat16)],
        compiler_params=pltpu.CompilerParams(dimension_semantics=("parallel", "arbitrary"),
                                             vmem_limit_bytes=PEER_VMEM_LIMIT),
        name="peer_experts",
    )(h, i1, i2, g, ut_bf, v_bf)


PEER_ROUTE_TILE = 512
PEER_TOKEN_TILE = 512
PEER_EXPERT_BLOCK = 1024


def peer_ffn(h, wq, k1, k2, u, v):
    bf = jnp.bfloat16
    i1, i2, g = peer_route(h, wq.astype(bf), k1.astype(bf), k2.astype(bf), PEER_ROUTE_TILE)
    return peer_experts(h, i1, i2, g, u.T.astype(bf), v.astype(bf), PEER_TOKEN_TILE, PEER_EXPERT_BLOCK)


def kernel(x, c, ctx, c_ctx, mod_w, mod_b, norm1_g, norm2_g, w_in, conv_w, conv_b, hy_w1, hy_b1, hy_w2, hy_b2, hy_w3, hy_b3, hy_freq, hy_bias, rw_w0, rw_w_up, rw_a0, rw_a_up, rw_g_up, rw_k_k, rw_k_a, rw_r_k, rw_ln_w, rw_ln_b, attn_sink, hy_out_g, at_out_g, w_out, peer_wq, peer_k1, peer_k2, peer_u, peer_v, final_g):
    B, T, D = x.shape
    Lc = ctx.shape[1]
    c_s = jax.nn.silu(c)
    cc_s = jax.nn.silu(c_ctx)
    xl, xc = x, ctx
    for i in range(DEPTH):
        last = i == DEPTH - 1
        lp = dict(w_in=w_in[i], conv_w=conv_w[i], conv_b=conv_b[i],
                  hy_w1=hy_w1[i], hy_b1=hy_b1[i], hy_w2=hy_w2[i], hy_b2=hy_b2[i],
                  hy_w3=hy_w3[i], hy_b3=hy_b3[i], hy_freq=hy_freq[i], hy_bias=hy_bias[i],
                  rw_w0=rw_w0[i], rw_w_up=rw_w_up[i], rw_a0=rw_a0[i], rw_a_up=rw_a_up[i],
                  rw_g_up=rw_g_up[i], rw_k_k=rw_k_k[i], rw_k_a=rw_k_a[i], rw_r_k=rw_r_k[i],
                  rw_ln_w=rw_ln_w[i], rw_ln_b=rw_ln_b[i], attn_sink=attn_sink[i],
                  hy_out_g=hy_out_g[i], at_out_g=at_out_g[i], w_out=w_out[i])
        sh1, sc1, g1, sh2, sc2, g2 = jnp.split((c_s @ mod_w[i] + mod_b[i])[:, None, :], 6, axis=-1)
        csh1, csc1, cg1, csh2, csc2, cg2 = jnp.split(cc_s @ mod_w[i] + mod_b[i], 6, axis=-1)
        hl = rmsnorm(xl, norm1_g[i]) * (1.0 + sc1) + sh1
        hc = rmsnorm(xc, norm1_g[i]) * (1.0 + csc1) + csh1
        parts_l, parts_c = token_mixer(hl, hc, lp, not last)
        peer_w = (peer_wq[i], peer_k1[i], peer_k2[i], peer_u[i], peer_v[i])
        out_w = (lp['hy_out_g'], lp['at_out_g'], norm2_g[i], lp['w_out'])
        flat = lambda parts: tuple(p.reshape(-1, p.shape[-1]) for p in parts)
        xl2, h2l = mix_out(flat(parts_l), xl.reshape(B * T, D), g1[:, 0], sc2[:, 0], sh2[:, 0], *out_w, rows_per_mod=T)
        fl = peer_ffn(h2l, *peer_w)
        xl = xl2.reshape(B, T, D) + g2 * fl.reshape(B, T, D)
        if not last:
            xc2, h2c = mix_out(flat(parts_c), xc.reshape(B * Lc, D), cg1[None], csc2[None], csh2[None], *out_w,
                               rows_per_mod=B * Lc)
            fc = peer_ffn(h2c, *peer_w)
            xc = (xc2 + cg2 * fc).reshape(B, Lc, D)
    return rmsnorm(xl, final_g)
```

```python
import functools
import math

import jax
import jax.numpy as jnp
import numpy as np
from jax import lax
from jax.experimental import pallas as pl
from jax.experimental.pallas import tpu as pltpu

D_MODEL = 1024
DEPTH = 4
GRID_W = 64

HY_CH = 256
RW_HEADS = 4
RW_HD = 64
RW_W = RW_HEADS * RW_HD
AT_HQ = 8
AT_HKV = 2
AT_G = AT_HQ // AT_HKV
AT_HD = 64
AT_W = AT_HQ * AT_HD
AT_KVW = AT_HKV * AT_HD

RW_DECAY_R = 64
RW_ICLR_R = 64
RW_GATE_R = 128
RW_LORA_W = 2 * RW_DECAY_R + 2 * RW_ICLR_R + RW_GATE_R
RW_GN_EPS = RW_HD * 1e-5

HY_EMB = 33
HY_BANDS = (HY_EMB - 1) // 2
HY_SHIFT = 0.05
HY_MIN_DECAY = math.log(1e-2) / 1.5
HY_MAX_DECAY = math.log(1e-2) / 0.3

ATT_WINDOW = 128
ATT_BLOCK = 128
ROPE_BASE = 10000.0
NEG_INF = -1e30

PEER_HEADS = 8
PEER_NKEYS = 128
PEER_DQ = 256
PEER_TOPK = 16
PEER_CHUNK = 128

OFF_HY = 0
OFF_RK = OFF_HY + 3 * HY_CH
OFF_LORA = OFF_RK + 3 * RW_W
OFF_AT = OFF_LORA + RW_LORA_W
CONV_COLS = OFF_LORA

VMEM_LIMIT = 48 * 1024 * 1024


def rmsnorm(x, g, eps=1e-6):
    y = x * lax.rsqrt(jnp.mean(x * x, axis=-1, keepdims=True) + eps)
    return y * g


def hyena_filter(L, w1, b1, w2, b2, w3, b3, freq):
    f32 = jnp.float32
    j = jnp.arange(L, dtype=f32)
    t = j / max(L - 1, 1)
    bands = jnp.linspace(1e-4, HY_BANDS - 1, HY_BANDS, dtype=f32)
    ang = 2.0 * math.pi * j[:, None] * bands[None, :] / L
    z = jnp.concatenate([t[:, None], jnp.cos(ang), -jnp.sin(ang)], axis=-1)
    h = jnp.sin(freq[0] * (z @ w1 + b1))
    h = jnp.sin(freq[1] * (h @ w2 + b2))
    h = (h @ w3 + b3).astype(f32).reshape(L, 2, HY_CH)
    deltas = jnp.abs(jnp.linspace(HY_MIN_DECAY, HY_MAX_DECAY, HY_CH, dtype=f32))
    window = jnp.exp(-t[:, None] * deltas[None, :]) + HY_SHIFT
    h = h * window[:, None, :]
    k = jnp.concatenate([h[:, 0], jnp.zeros((1, HY_CH), f32), h[:0:-1, 1]], axis=0)
    return k * lax.rsqrt(jnp.sum(k * k, axis=0, keepdims=True) + 1e-6)


HY_BLOCK = 256


def _hyena_kernel(fr_ref, u_ref, y_ref, *, nblk, nb):
    P = HY_BLOCK
    ncol = nblk * nb
    two_l = 2 * nblk * P
    u = u_ref[0].astype(jnp.bfloat16)
    col = lax.broadcasted_iota(jnp.int32, (P, ncol), 1)
    acc = jnp.zeros((P, ncol), jnp.float32)
    for m in range(-(nblk - 1), nblk):
        start = (-P * m) % two_l
        gr = fr_ref[0, :, start:start + 2 * P]
        rolled = pltpu.roll(jnp.broadcast_to(gr, (P, 2 * P)), P + 1, 1, stride=1, stride_axis=0)
        t_m = rolled[:, :P].astype(jnp.bfloat16)
        prod = jnp.dot(t_m, u, preferred_element_type=jnp.float32)
        if m == 0:
            acc = acc + prod
        elif m > 0:
            acc = acc + jnp.where(col >= nb * m, pltpu.roll(prod, nb * m, 1), 0.0)
        else:
            acc = acc + jnp.where(col < ncol + nb * m, pltpu.roll(prod, ncol + nb * m, 1), 0.0)
    y_ref[0] = acc


def hyena_conv(u, filt):
    B, L, C = u.shape
    P = HY_BLOCK
    assert L % P == 0 and filt.shape == (2 * L, C)
    nblk = L // P
    idx = (P - 1 - np.arange(2 * L + 2 * P)) % (2 * L)
    fr = filt[idx].T.reshape(C, 1, 2 * L + 2 * P)
    uc = u.reshape(B, nblk, P, C).transpose(3, 2, 1, 0).reshape(C, P, nblk * B)
    y = pl.pallas_call(
        functools.partial(_hyena_kernel, nblk=nblk, nb=B),
        grid=(C,),
        in_specs=[pl.BlockSpec((1, 1, 2 * L + 2 * P), lambda c: (c, 0, 0)),
                  pl.BlockSpec((1, P, nblk * B), lambda c: (c, 0, 0))],
        out_specs=pl.BlockSpec((1, P, nblk * B), lambda c: (c, 0, 0)),
        out_shape=jax.ShapeDtypeStruct((C, P, nblk * B), jnp.float32),
        compiler_params=pltpu.CompilerParams(dimension_semantics=("parallel",), vmem_limit_bytes=VMEM_LIMIT),
        name="hyena_conv",
    )(fr, uc)
    return y.reshape(C, P, nblk, B).transpose(3, 2, 1, 0).reshape(B, L, C)


RW_LANES = 128
RW_TCHUNK = 32


def _rwkv_scan_kernel(rf, wf, kf, vf, af, bf_, rb, wb, kb, vb, ab, bb, yf_ref, yb_ref, s_ref, *, tc):
    @pl.when(pl.program_id(0) == 0)
    def _init():
        s_ref[...] = jnp.zeros_like(s_ref)

    fwd = lax.broadcasted_iota(jnp.int32, (RW_HD, RW_LANES), 1) < RW_LANES // 2
    fwd_row = fwd[0:1]

    def step(t, carry):
        tb = tc - 1 - t
        a_t = jnp.where(fwd, af[t], ab[tb])
        w_t = jnp.where(fwd, wf[t], wb[tb])
        k_t = jnp.where(fwd, kf[t], kb[tb])
        b_t = jnp.where(fwd, bf_[t], bb[tb])
        r_t = jnp.where(fwd, rf[t], rb[tb])
        for v in range(RW_HD):
            sv = s_ref[v]
            sa = jnp.sum(sv * a_t, axis=0, keepdims=True)
            vv = jnp.where(fwd_row, vf[t, pl.ds(v, 1), :], vb[tb, pl.ds(v, 1), :])
            sv = sv * w_t + sa * b_t + vv * k_t
            s_ref[v] = sv
            y = jnp.sum(sv * r_t, axis=0, keepdims=True)
            yf_ref[t, pl.ds(v, 1), :] = y
            yb_ref[tb, pl.ds(v, 1), :] = y
        return carry

    lax.fori_loop(0, tc, step, 0)


def rwkv_scan_lanes(r, w, k, v, a, b, n_ctx):
    L = r.shape[0]
    tc = RW_TCHUNK
    assert L % tc == 0 and n_ctx % tc == 0 and r.shape[1:] == (RW_HD, RW_LANES)
    nblk, cblk = L // tc, n_ctx // tc

    def fmap(i):
        return (i, 0, 0)

    def bmap(i):
        return (jnp.where(i < cblk, cblk - 1 - i, nblk - 1 + cblk - i), 0, 0)

    fspec = pl.BlockSpec((tc, RW_HD, RW_LANES), fmap)
    bspec = pl.BlockSpec((tc, RW_HD, RW_LANES), bmap)
    sds = jax.ShapeDtypeStruct((L, RW_HD, RW_LANES), jnp.float32)
    return pl.pallas_call(
        functools.partial(_rwkv_scan_kernel, tc=tc),
        grid=(nblk,),
        in_specs=[fspec] * 6 + [bspec] * 6,
        out_specs=[fspec, bspec],
        out_shape=[sds, sds],
        scratch_shapes=[pltpu.VMEM((RW_HD, RW_HD, RW_LANES), jnp.float32)],
        compiler_params=pltpu.CompilerParams(dimension_semantics=("arbitrary",),
                                             vmem_limit_bytes=VMEM_LIMIT),
        name="rwkv_scan",
    )(r, w, k, v, a, b, r, w, k, v, a, b)


def rwkv7_mix(rkv_c, rkv_l, lora_c, lora_l, lp):
    f32 = jnp.float32
    B, Lc, _ = rkv_c.shape
    T = rkv_l.shape[1]
    Lt = Lc + T
    hs = (B, Lt, RW_HEADS, RW_HD)
    rkv = jnp.concatenate([rkv_c, rkv_l], axis=1)
    lora = jnp.concatenate([lora_c, lora_l], axis=1)
    r = rkv[..., :RW_W].reshape(hs)
    k = rkv[..., RW_W:2 * RW_W].reshape(hs)
    v = rkv[..., 2 * RW_W:].reshape(hs)
    kk = k * lp['rw_k_k'].reshape(RW_HEADS, RW_HD)
    kk = kk * lax.rsqrt(jnp.sum(kk * kk, axis=-1, keepdims=True) + 1e-12)
    k_a = lp['rw_k_a'].reshape(RW_HEADS, RW_HD)
    r_k = lp['rw_r_k']
    assert 2 * B * RW_HEADS == RW_LANES
    bonus = jnp.zeros(hs, f32)
    per_dir = []
    for d in range(2):
        wd = lora[..., d * RW_DECAY_R:(d + 1) * RW_DECAY_R]
        ad = lora[..., 2 * RW_DECAY_R + d * RW_ICLR_R:2 * RW_DECAY_R + (d + 1) * RW_ICLR_R]
        w_log = -jax.nn.softplus(-(lp['rw_w0'][d] + jnp.tanh(wd) @ lp['rw_w_up'][d])) - 0.5
        decay = jnp.exp(-jnp.exp(w_log)).reshape(hs)
        a = jax.nn.sigmoid(lp['rw_a0'][d] + ad @ lp['rw_a_up'][d]).reshape(hs)
        k_d = k * (1.0 + (a - 1.0) * k_a)
        per_dir.append((r, decay, k_d, v, -kk, kk * a))
        bonus = bonus + jnp.sum(r * k_d * r_k, axis=-1, keepdims=True) * v

    def lanes(xf, xb):
        return jnp.concatenate([x.transpose(1, 3, 0, 2).reshape(Lt, RW_HD, B * RW_HEADS) for x in (xf, xb)], axis=-1)

    yf, yb = rwkv_scan_lanes(*(lanes(xf, xb) for xf, xb in zip(*per_dir)), n_ctx=Lc)
    half = RW_LANES // 2
    y_sum = (yf[..., :half] + yb[..., half:]).reshape(Lt, RW_HD, B, RW_HEADS).transpose(2, 0, 3, 1)
    mu = jnp.mean(y_sum, axis=-1, keepdims=True)
    var = jnp.mean(jnp.square(y_sum - mu), axis=-1, keepdims=True)
    yn = (y_sum - mu) * lax.rsqrt(var + RW_GN_EPS)
    yn = yn * lp['rw_ln_w'].reshape(RW_HEADS, RW_HD) + lp['rw_ln_b'].reshape(RW_HEADS, RW_HD)
    g = jax.nn.sigmoid(lora[..., 2 * RW_DECAY_R + 2 * RW_ICLR_R:]) @ lp['rw_g_up']
    out = (yn + bonus).reshape(B, Lt, RW_W) * g
    return out[:, :Lc], out[:, Lc:]


def rope_tables(T):
    f32 = jnp.float32
    quarter = AT_HD // 4
    inv = ROPE_BASE ** (-jnp.arange(quarter, dtype=f32) / quarter)
    rows = T // GRID_W
    row = jnp.repeat(jnp.arange(rows), GRID_W).astype(f32)
    col = jnp.tile(jnp.arange(GRID_W), rows).astype(f32)
    out_c, out_s = [], []
    for pos in (row, col):
        ang = pos[:, None] * inv[None, :]
        c, s = jnp.cos(ang), jnp.sin(ang)
        out_c += [c, c]
        out_s += [-s, s]
    return jnp.concatenate(out_c, axis=1), jnp.concatenate(out_s, axis=1)


def _rope(x, c, s):
    lane = lax.broadcasted_iota(jnp.int32, x.shape, 1)
    first = (lane % 32) < 16
    partner = jnp.where(first, pltpu.roll(x, x.shape[1] - 16, axis=1), pltpu.roll(x, 16, axis=1))
    return x * c + partner * s


def _attn_kernel(q_ref, kp_ref, kc_ref, kn_ref, vp_ref, vc_ref, vn_ref, kx_ref, vx_ref, sink_ref,
                 cq_ref, sq_ref, ckp_ref, skp_ref, ckc_ref, skc_ref, ckn_ref, skn_ref, o_ref, *, nblk):
    i = pl.program_id(1)
    blk = ATT_BLOCK
    scale = AT_HD ** -0.5
    q = _rope(q_ref[0], cq_ref[...], sq_ref[...])
    kspan = jnp.concatenate([_rope(kp_ref[0], ckp_ref[...], skp_ref[...]),
                             _rope(kc_ref[0], ckc_ref[...], skc_ref[...]),
                             _rope(kn_ref[0], ckn_ref[...], skn_ref[...])], axis=0)
    vspan = jnp.concatenate([vp_ref[0], vc_ref[0], vn_ref[0]], axis=0)
    kx = kx_ref[0]
    vx = vx_ref[0]
    qpos = lax.broadcasted_iota(jnp.int32, (blk, 3 * blk), 0) + blk
    kpos = lax.broadcasted_iota(jnp.int32, (blk, 3 * blk), 1)
    valid = jnp.abs(qpos - kpos) <= ATT_WINDOW
    valid = valid & ((kpos >= blk) | (i > 0)) & ((kpos < 2 * blk) | (i < nblk - 1))
    valid4 = jnp.concatenate([valid] * AT_G, axis=0)
    nt = (((1,), (1,)), ((), ()))
    bf = jnp.bfloat16
    outs = []
    for g in range(AT_HKV):
        kg = kspan[:, g * AT_HD:(g + 1) * AT_HD].astype(bf)
        vg = vspan[:, g * AT_HD:(g + 1) * AT_HD].astype(bf)
        kxg = kx[:, g * AT_HD:(g + 1) * AT_HD].astype(bf)
        vxg = vx[:, g * AT_HD:(g + 1) * AT_HD].astype(bf)
        qg = jnp.concatenate([q[:, (g * AT_G + j) * AT_HD:(g * AT_G + j + 1) * AT_HD] for j in range(AT_G)],
                             axis=0).astype(bf)
        sk = jnp.concatenate([jnp.full((blk, 1), sink_ref[g * AT_G + j], jnp.float32) for j in range(AT_G)], axis=0)
        s_lat = lax.dot_general(qg, kg, nt, preferred_element_type=jnp.float32) * scale
        s_lat = jnp.where(valid4, s_lat, NEG_INF)
        s_ctx = lax.dot_general(qg, kxg, nt, preferred_element_type=jnp.float32) * scale
        m = jnp.maximum(jnp.maximum(jnp.max(s_lat, axis=1, keepdims=True), jnp.max(s_ctx, axis=1, keepdims=True)), sk)
        p_lat = jnp.exp(s_lat - m)
        p_ctx = jnp.exp(s_ctx - m)
        den = jnp.sum(p_lat, axis=1, keepdims=True) + jnp.sum(p_ctx, axis=1, keepdims=True) + jnp.exp(sk - m)
        inv = 1.0 / den
        o = (jnp.dot((p_lat * inv).astype(bf), vg, preferred_element_type=jnp.float32)
             + jnp.dot((p_ctx * inv).astype(bf), vxg, preferred_element_type=jnp.float32))
        outs += [o[j * blk:(j + 1) * blk] for j in range(AT_G)]
    o_ref[0] = jnp.concatenate(outs, axis=1)


def windowed_attention(q, k, v, kx, vx, sink):
    B, T, _ = q.shape
    Lc = kx.shape[1]
    assert T % ATT_BLOCK == 0 and ATT_WINDOW == ATT_BLOCK
    nblk = T // ATT_BLOCK
    c1, s1 = rope_tables(T)
    cq, sq = jnp.tile(c1, (1, AT_HQ)), jnp.tile(s1, (1, AT_HQ))
    ck, sk = jnp.tile(c1, (1, AT_HKV)), jnp.tile(s1, (1, AT_HKV))

    def prev(b, i):
        return (b, jnp.maximum(i - 1, 0), 0)

    def cur(b, i):
        return (b, i, 0)

    def nxt(b, i):
        return (b, jnp.minimum(i + 1, nblk - 1), 0)

    def kb(f):
        return pl.BlockSpec((1, ATT_BLOCK, AT_KVW), f)

    def tb(f):
        return pl.BlockSpec((ATT_BLOCK, AT_KVW), lambda b, i: f(b, i)[1:])

    return pl.pallas_call(
        functools.partial(_attn_kernel, nblk=nblk),
        grid=(B, nblk),
        in_specs=[pl.BlockSpec((1, ATT_BLOCK, AT_W), cur), kb(prev), kb(cur), kb(nxt), kb(prev), kb(cur), kb(nxt),
                  pl.BlockSpec((1, Lc, AT_KVW), lambda b, i: (b, 0, 0)),
                  pl.BlockSpec((1, Lc, AT_KVW), lambda b, i: (b, 0, 0)),
                  pl.BlockSpec(memory_space=pltpu.SMEM),
                  pl.BlockSpec((ATT_BLOCK, AT_W), lambda b, i: (i, 0)),
                  pl.BlockSpec((ATT_BLOCK, AT_W), lambda b, i: (i, 0)),
                  tb(prev), tb(prev), tb(cur), tb(cur), tb(nxt), tb(nxt)],
        out_specs=pl.BlockSpec((1, ATT_BLOCK, AT_W), cur),
        out_shape=jax.ShapeDtypeStruct((B, T, AT_W), jnp.float32),
        compiler_params=pltpu.CompilerParams(dimension_semantics=("parallel", "arbitrary"),
                                             vmem_limit_bytes=VMEM_LIMIT),
        name="window_attention",
    )(q, k, k, k, v, v, v, kx, vx, sink, cq, sq, ck, sk, ck, sk, ck, sk)


def context_attention(qc, kc, vc, sink):
    B, Lc = qc.shape[:2]
    scale = AT_HD ** -0.5
    s = jnp.einsum('bqgjd,bkgd->bgjqk', qc, kc) * scale
    sink_b = jnp.broadcast_to(sink[None, :, :, None, None], (B, AT_HKV, AT_G, Lc, 1))
    p = jax.nn.softmax(jnp.concatenate([s, sink_b], axis=-1), axis=-1)[..., :Lc]
    return jnp.einsum('bgjqk,bkgd->bqgjd', p, vc).reshape(B, Lc, AT_W)


INPROJ_ROWS = 256
SUBLANE = 8


def _inproj_kernel(h_ref, hp_ref, hn_ref, w_ref, cw_ref, cb_ref, x0_ref, u_ref, rkv_ref, lora_ref, q_ref, k_ref, v_ref,
                   *, tiles_per_seq):
    i = pl.program_id(0)
    tm = h_ref.shape[0]
    bf = jnp.bfloat16
    z = jnp.dot(h_ref[...].astype(bf), w_ref[...], preferred_element_type=jnp.float32)
    wc = w_ref[:, :CONV_COLS]
    zp = jnp.dot(hp_ref[...].astype(bf), wc, preferred_element_type=jnp.float32)[SUBLANE - 1:SUBLANE]
    zn = jnp.dot(hn_ref[...].astype(bf), wc, preferred_element_type=jnp.float32)[0:1]
    pos = i % tiles_per_seq
    zp = jnp.where(pos == 0, 0.0, zp)
    zn = jnp.where(pos == tiles_per_seq - 1, 0.0, zn)
    zc = z[:, :CONV_COLS]
    row = lax.broadcasted_iota(jnp.int32, zc.shape, 0)
    up = jnp.where(row == 0, zp, pltpu.roll(zc, 1, 0))
    dn = jnp.where(row == tm - 1, zn, pltpu.roll(zc, tm - 1, 0))
    s = up * cw_ref[0:1] + zc * cw_ref[1:2] + dn * cw_ref[2:3] + cb_ref[...]
    x0_ref[...] = s[:, :HY_CH]
    u_ref[...] = s[:, HY_CH:2 * HY_CH] * s[:, 2 * HY_CH:OFF_RK]
    rkv_ref[...] = s[:, OFF_RK:]
    lora_ref[...] = z[:, OFF_LORA:OFF_AT]
    q_ref[...] = z[:, OFF_AT:OFF_AT + AT_W]
    k_ref[...] = z[:, OFF_AT + AT_W:OFF_AT + AT_W + AT_KVW]
    v_ref[...] = z[:, OFF_AT + AT_W + AT_KVW:]


def in_proj(h3, w_in, conv_w, conv_b):
    nseq, seq_len, d = h3.shape
    n = nseq * seq_len
    tm = INPROJ_ROWS
    assert seq_len % tm == 0 and w_in.shape == (d, OFF_AT + AT_W + 2 * AT_KVW)
    tps = seq_len // tm
    nb8 = n // SUBLANE
    r8 = tm // SUBLANE
    h = h3.reshape(n, d)

    def out(w):
        return pl.BlockSpec((tm, w), lambda i: (i, 0))

    widths = (HY_CH, HY_CH, 3 * RW_W, RW_LORA_W, AT_W, AT_KVW, AT_KVW)
    outs = pl.pallas_call(
        functools.partial(_inproj_kernel, tiles_per_seq=tps),
        grid=(n // tm,),
        in_specs=[pl.BlockSpec((tm, d), lambda i: (i, 0)),
                  pl.BlockSpec((SUBLANE, d), lambda i: (jnp.maximum(i * r8 - 1, 0), 0)),
                  pl.BlockSpec((SUBLANE, d), lambda i: (jnp.minimum((i + 1) * r8, nb8 - 1), 0)),
                  pl.BlockSpec(w_in.shape, lambda i: (0, 0)),
                  pl.BlockSpec((3, CONV_COLS), lambda i: (0, 0)),
                  pl.BlockSpec((1, CONV_COLS), lambda i: (0, 0))],
        out_specs=[out(w) for w in widths],
        out_shape=[jax.ShapeDtypeStruct((n, w), jnp.float32) for w in widths],
        compiler_params=pltpu.CompilerParams(dimension_semantics=("parallel",), vmem_limit_bytes=VMEM_LIMIT),
        name="in_proj",
    )(h, h, h, w_in.astype(jnp.bfloat16), conv_w, conv_b.reshape(1, -1))
    return tuple(o.reshape(nseq, seq_len, o.shape[-1]) for o in outs)


def token_mixer(hl, hc, lp, need_ctx):
    B, T, _ = hl.shape
    Lc = hc.shape[1]
    x0_l, u_l, rkv_l, lora_l, q_l, k_l, v_l = in_proj(hl, lp['w_in'], lp['conv_w'], lp['conv_b'])
    x0_c, u_c, rkv_c, lora_c, q_c, k_c, v_c = in_proj(hc, lp['w_in'], lp['conv_w'], lp['conv_b'])
    filt_p = (lp['hy_w1'], lp['hy_b1'], lp['hy_w2'], lp['hy_b2'], lp['hy_w3'], lp['hy_b3'], lp['hy_freq'])
    hy_l = x0_l * (hyena_conv(u_l, hyena_filter(T, *filt_p)) + u_l * lp['hy_bias'])
    rw_c, rw_l = rwkv7_mix(rkv_c, rkv_l, lora_c, lora_l, lp)
    at_l = windowed_attention(q_l, k_l, v_l, k_c, v_c, lp['attn_sink'])
    parts_l = (hy_l, rw_l, at_l)
    if not need_ctx:
        return parts_l, None
    hy_c = x0_c * (hyena_conv(u_c, hyena_filter(Lc, *filt_p)) + u_c * lp['hy_bias'])
    at_c = context_attention(q_c.reshape(B, Lc, AT_HKV, AT_G, AT_HD), k_c.reshape(B, Lc, AT_HKV, AT_HD),
                             v_c.reshape(B, Lc, AT_HKV, AT_HD), lp['attn_sink'].reshape(AT_HKV, AT_G))
    return parts_l, (hy_c, rw_c, at_c)


NORM_EPS = 1e-6
MIXOUT_ROWS = 512


def _rms(x, g):
    return (x * lax.rsqrt(jnp.mean(x * x, axis=-1, keepdims=True) + NORM_EPS)) * g


def _mixout_kernel(hy_ref, rw_ref, at_ref, x_ref, g1_ref, sc2_ref, sh2_ref, hyg_ref, atg_ref, n2g_ref, w_ref,
                   xo_ref, h2_ref):
    mix = jnp.concatenate([_rms(hy_ref[...], hyg_ref[...]), rw_ref[...], _rms(at_ref[...], atg_ref[...])], axis=1)
    m = jnp.dot(mix.astype(jnp.bfloat16), w_ref[...], preferred_element_type=jnp.float32)
    xn = x_ref[...] + g1_ref[0] * m
    xo_ref[...] = xn
    h2_ref[...] = (_rms(xn, n2g_ref[...]) * (1.0 + sc2_ref[0]) + sh2_ref[0]).astype(h2_ref.dtype)


def mix_out(parts, x, gate1, scale2, shift2, hy_g, at_g, norm2_g, w_out, rows_per_mod):
    hy, rw, at = parts
    n, d = x.shape
    tm = MIXOUT_ROWS
    assert n % tm == 0 and rows_per_mod % tm == 0
    tpm = rows_per_mod // tm

    def row(w):
        return pl.BlockSpec((tm, w), lambda i: (i, 0))

    def vec(w):
        return pl.BlockSpec((1, w), lambda i: (0, 0))

    mod = pl.BlockSpec((1, 1, d), lambda i: (i // tpm, 0, 0))
    return pl.pallas_call(
        _mixout_kernel,
        grid=(n // tm,),
        in_specs=[row(hy.shape[1]), row(rw.shape[1]), row(at.shape[1]), row(d), mod, mod, mod,
                  vec(hy.shape[1]), vec(at.shape[1]), vec(d), pl.BlockSpec(w_out.shape, lambda i: (0, 0))],
        out_specs=[row(d), row(d)],
        out_shape=[jax.ShapeDtypeStruct((n, d), jnp.float32), jax.ShapeDtypeStruct((n, d), jnp.bfloat16)],
        compiler_params=pltpu.CompilerParams(dimension_semantics=("parallel",), vmem_limit_bytes=VMEM_LIMIT),
        name="mix_out",
    )(hy, rw, at, x, gate1.reshape(-1, 1, d), scale2.reshape(-1, 1, d), shift2.reshape(-1, 1, d),
      hy_g.reshape(1, -1), at_g.reshape(1, -1), norm2_g.reshape(1, -1), w_out.astype(jnp.bfloat16))


def _top16_rows(s, n_rows):
    rio = lax.broadcasted_iota(jnp.int32, s.shape, 0)
    vals, idxs = [], []
    for _ in range(PEER_TOPK):
        m = jnp.max(s, axis=0, keepdims=True)
        idx = jnp.min(jnp.where(s == m, rio, n_rows), axis=0, keepdims=True)
        vals.append(m)
        idxs.append(idx)
        s = jnp.where(rio == idx, -jnp.inf, s)
    return jnp.concatenate(vals, axis=0), jnp.concatenate(idxs, axis=0)


LANE = 128
ROUTE_UNROLL = 4


def _cand_layout():
    rows = [(0, b) for b in range(PEER_TOPK)]
    for a in range(1, 8):
        rows += [(a, b) for b in range(8)]
    rows += [(a, 0) for a in range(8, PEER_TOPK)]
    a = np.array([r[0] for r in rows])
    b = np.array([r[1] for r in rows])
    return a, b, (a + 1) * (b + 1) <= PEER_TOPK


def cand_table():
    a, b, valid = _cand_layout()
    cio = np.where(valid, a * PEER_TOPK + b, PEER_TOPK * PEER_TOPK).astype(np.int32)
    return jnp.asarray(np.broadcast_to(cio[:, None], (cio.shape[0], LANE)))


def _route_kernel(x_ref, wq_ref, k1_ref, k2_ref, cio_ref, i1_ref, i2_ref, g_ref, q_ref, *, tt):
    q = jnp.dot(x_ref[...].astype(jnp.bfloat16), wq_ref[...], preferred_element_type=jnp.float32)
    q_ref[...] = q.astype(jnp.bfloat16)
    nt = (((1,), (1,)), ((), ()))
    cio = cio_ref[...]
    ncand = PEER_TOPK * PEER_TOPK

    def chunk(c, carry):
        t0 = pl.multiple_of(c * LANE, LANE)
        qc = q_ref[pl.ds(t0, LANE), :]
        s1 = lax.dot_general(k1_ref[0], qc[:, :PEER_DQ // 2], nt, preferred_element_type=jnp.float32)
        s2 = lax.dot_general(k2_ref[0], qc[:, PEER_DQ // 2:], nt, preferred_element_type=jnp.float32)
        v1, j1 = _top16_rows(s1, PEER_NKEYS)
        v2, j2 = _top16_rows(s2, PEER_NKEYS)
        j1 = j1 * PEER_NKEYS
        cand = jnp.concatenate([v1[0:1] + v2] + [v1[a:a + 1] + v2[0:8] for a in range(1, 8)]
                               + [v1[8:16] + v2[0:1]], axis=0)
        ecand = jnp.concatenate([j1[0:1] + j2] + [j1[a:a + 1] + j2[0:8] for a in range(1, 8)]
                                + [j1[8:16] + j2[0:1]], axis=0)
        cand = jnp.where(cio < ncand, cand, -jnp.inf)
        scs, es = [], []
        for _ in range(PEER_TOPK):
            m = jnp.max(cand, axis=0, keepdims=True)
            cidx = jnp.min(jnp.where(cand == m, cio, ncand), axis=0, keepdims=True)
            sel = cio == cidx
            es.append(jnp.max(jnp.where(sel, ecand, -1), axis=0, keepdims=True))
            scs.append(m)
            cand = jnp.where(sel, -jnp.inf, cand)
        sc = jnp.concatenate(scs, axis=0)
        e = jnp.concatenate(es, axis=0)
        p = jnp.exp(sc - sc[0:1])
        i1_ref[0, :, pl.ds(t0, LANE)] = e >> 7
        i2_ref[0, :, pl.ds(t0, LANE)] = e & (PEER_NKEYS - 1)
        g_ref[0, :, pl.ds(t0, LANE)] = p / jnp.sum(p, axis=0, keepdims=True)
        return carry

    lax.fori_loop(0, tt // LANE, chunk, 0, unroll=ROUTE_UNROLL)


def peer_route(h, wq_bf, k1_bf, k2_bf, tt):
    n, d = h.shape
    assert n % tt == 0 and tt % (ROUTE_UNROLL * LANE) == 0
    cio = cand_table()
    out_sds = [jax.ShapeDtypeStruct((PEER_HEADS, PEER_TOPK, n), jnp.int32),
               jax.ShapeDtypeStruct((PEER_HEADS, PEER_TOPK, n), jnp.int32),
               jax.ShapeDtypeStruct((PEER_HEADS, PEER_TOPK, n), jnp.float32)]
    ospec = pl.BlockSpec((1, PEER_TOPK, tt), lambda i, hh: (hh, 0, i))
    return pl.pallas_call(
        functools.partial(_route_kernel, tt=tt),
        grid=(n // tt, PEER_HEADS),
        in_specs=[pl.BlockSpec((tt, d), lambda i, hh: (i, 0)),
                  pl.BlockSpec((d, PEER_DQ), lambda i, hh: (0, hh)),
                  pl.BlockSpec((1, PEER_NKEYS, PEER_DQ // 2), lambda i, hh: (hh, 0, 0)),
                  pl.BlockSpec((1, PEER_NKEYS, PEER_DQ // 2), lambda i, hh: (hh, 0, 0)),
                  pl.BlockSpec(cio.shape, lambda i, hh: (0, 0))],
        out_specs=[ospec, ospec, ospec],
        out_shape=out_sds,
        scratch_shapes=[pltpu.VMEM((tt, PEER_DQ), jnp.bfloat16)],
        compiler_params=pltpu.CompilerParams(dimension_semantics=("parallel", "arbitrary"),
                                             vmem_limit_bytes=VMEM_LIMIT),
        name="peer_route",
    )(h, wq_bf, k1_bf, k2_bf, cio)


def _gelu(x):
    return 0.5 * x * (1.0 + lax.erf(x * (2.0 ** -0.5)))


PEER_BUILD_GROUP = 16
PEER_VMEM_LIMIT = 56 * 1024 * 1024


def _expert_kernel(x_ref, i1_ref, i2_ref, g_ref, ut_ref, v_ref, res_ref, g2_ref, ng_ref, nsc_ref, nsh_ref,
                   o_ref, hn_ref, w_ref, rows_ref, xb_ref, *, tt, eb):
    j = pl.program_id(1)
    nk = PEER_NKEYS
    grp = PEER_BUILD_GROUP

    @pl.when(j == 0)
    def _build():
        xb_ref[...] = x_ref[...].astype(jnp.bfloat16)
        rows_ref[0] = i1_ref[...].reshape(nk, tt).astype(jnp.float32).T
        rows_ref[1] = i2_ref[...].reshape(nk, tt).astype(jnp.float32).T
        rows_ref[2] = g_ref[...].reshape(nk, tt).T
        kio = lax.broadcasted_iota(jnp.int32, (nk, nk), 0).astype(jnp.float32)
        nt = (((1,), (1,)), ((), ()))

        def body(gidx, carry):
            t0 = pl.multiple_of(gidx * grp, grp)
            tiles = []
            for s in range(grp):
                i1row = jnp.broadcast_to(rows_ref[0, pl.ds(t0 + s, 1), :], (nk, nk))
                i2row = jnp.broadcast_to(rows_ref[1, pl.ds(t0 + s, 1), :], (nk, nk))
                grow = jnp.broadcast_to(rows_ref[2, pl.ds(t0 + s, 1), :], (nk, nk))
                m1 = jnp.where(i1row == kio, grow, 0.0).astype(jnp.bfloat16)
                p2 = jnp.where(i2row == kio, 1.0, 0.0).astype(jnp.bfloat16)
                tiles.append(lax.dot_general(m1, p2, nt, preferred_element_type=jnp.float32))
            w_ref[:, pl.ds(t0, grp), :] = jnp.swapaxes(jnp.stack(tiles, axis=0), 0, 1).astype(jnp.bfloat16)
            return carry

        lax.fori_loop(0, tt // grp, body, 0)

    a = jnp.dot(xb_ref[...], ut_ref[...], preferred_element_type=jnp.float32)
    nrow = eb // nk
    wj = jnp.concatenate([w_ref[j * nrow + r] for r in range(nrow)], axis=1)
    cmat = (wj.astype(jnp.float32) * _gelu(a)).astype(jnp.bfloat16)
    contrib = jnp.dot(cmat, v_ref[...], preferred_element_type=jnp.float32)

    last = pl.num_programs(1) - 1

    @pl.when(j == 0)
    def _first():
        o_ref[...] = contrib

    @pl.when((j > 0) & (j < last))
    def _middle():
        o_ref[...] += contrib

    @pl.when(j == last)
    def _finish():
        xn = res_ref[...] + g2_ref[0] * (o_ref[...] + contrib)
        o_ref[...] = xn
        hn_ref[...] = _rms(xn, ng_ref[...]) * (1.0 + nsc_ref[0]) + nsh_ref[0]


def peer_experts(h, i1, i2, g, ut_bf, v_bf, res, gate2, next_g, next_scale, next_shift, rows_per_mod, tt, eb):
    n, d = h.shape
    nexp = v_bf.shape[0]
    nblk = nexp // eb
    assert n % tt == 0 and nexp % eb == 0 and nblk >= 2 and nexp == PEER_NKEYS * PEER_NKEYS
    assert rows_per_mod % tt == 0
    tpm = rows_per_mod // tt
    rspec = pl.BlockSpec((PEER_HEADS, PEER_TOPK, tt), lambda i, j: (0, 0, i))
    tile = pl.BlockSpec((tt, d), lambda i, j: (i, 0))
    mod = pl.BlockSpec((1, 1, d), lambda i, j: (i // tpm, 0, 0))
    sds = jax.ShapeDtypeStruct((n, d), jnp.float32)
    return pl.pallas_call(
        functools.partial(_expert_kernel, tt=tt, eb=eb),
        grid=(n // tt, nblk),
        in_specs=[tile, rspec, rspec, rspec,
                  pl.BlockSpec((d, eb), lambda i, j: (0, j)),
                  pl.BlockSpec((eb, d), lambda i, j: (j, 0)),
                  tile, mod, pl.BlockSpec((1, d), lambda i, j: (0, 0)), mod, mod],
        out_specs=[tile, tile],
        out_shape=[sds, sds],
        scratch_shapes=[pltpu.VMEM((PEER_NKEYS, tt, PEER_NKEYS), jnp.bfloat16),
                        pltpu.VMEM((3, tt, PEER_NKEYS), jnp.float32),
                        pltpu.VMEM((tt, d), jnp.bfloat16)],
        compiler_params=pltpu.CompilerParams(dimension_semantics=("parallel", "arbitrary"),
                                             vmem_limit_bytes=PEER_VMEM_LIMIT),
        name="peer_experts",
    )(h, i1, i2, g, ut_bf, v_bf, res, gate2.reshape(-1, 1, d), next_g.reshape(1, d),
      next_scale.reshape(-1, 1, d), next_shift.reshape(-1, 1, d))


PEER_ROUTE_TILE = 512
PEER_TOKEN_TILE = 512
PEER_EXPERT_BLOCK = 1024


def peer_ffn(h, wq, k1, k2, u, v, res, gate2, next_g, next_scale, next_shift, rows_per_mod):
    bf = jnp.bfloat16
    i1, i2, g = peer_route(h, wq.astype(bf), k1.astype(bf), k2.astype(bf), PEER_ROUTE_TILE)
    return peer_experts(h, i1, i2, g, u.T.astype(bf), v.astype(bf), res, gate2, next_g, next_scale, next_shift,
                        rows_per_mod, PEER_TOKEN_TILE, PEER_EXPERT_BLOCK)


def kernel(x, c, ctx, c_ctx, mod_w, mod_b, norm1_g, norm2_g, w_in, conv_w, conv_b, hy_w1, hy_b1, hy_w2, hy_b2, hy_w3, hy_b3, hy_freq, hy_bias, rw_w0, rw_w_up, rw_a0, rw_a_up, rw_g_up, rw_k_k, rw_k_a, rw_r_k, rw_ln_w, rw_ln_b, attn_sink, hy_out_g, at_out_g, w_out, peer_wq, peer_k1, peer_k2, peer_u, peer_v, final_g):
    B, T, D = x.shape
    Lc = ctx.shape[1]
    c_s = jax.nn.silu(c)
    cc_s = jax.nn.silu(c_ctx)
    mods = [jnp.split(c_s @ mod_w[i] + mod_b[i], 6, axis=-1) for i in range(DEPTH)]
    cmods = [jnp.split(cc_s @ mod_w[i] + mod_b[i], 6, axis=-1) for i in range(DEPTH)]
    xl, xc = x.reshape(B * T, D), ctx.reshape(B * Lc, D)
    hl = rmsnorm(x, norm1_g[0]) * (1.0 + mods[0][1][:, None, :]) + mods[0][0][:, None, :]
    hc = rmsnorm(ctx, norm1_g[0]) * (1.0 + cmods[0][1]) + cmods[0][0]
    zeros = jnp.zeros((B, D), jnp.float32)
    for i in range(DEPTH):
        last = i == DEPTH - 1
        lp = dict(w_in=w_in[i], conv_w=conv_w[i], conv_b=conv_b[i],
                  hy_w1=hy_w1[i], hy_b1=hy_b1[i], hy_w2=hy_w2[i], hy_b2=hy_b2[i],
                  hy_w3=hy_w3[i], hy_b3=hy_b3[i], hy_freq=hy_freq[i], hy_bias=hy_bias[i],
                  rw_w0=rw_w0[i], rw_w_up=rw_w_up[i], rw_a0=rw_a0[i], rw_a_up=rw_a_up[i],
                  rw_g_up=rw_g_up[i], rw_k_k=rw_k_k[i], rw_k_a=rw_k_a[i], rw_r_k=rw_r_k[i],
                  rw_ln_w=rw_ln_w[i], rw_ln_b=rw_ln_b[i], attn_sink=attn_sink[i],
                  hy_out_g=hy_out_g[i], at_out_g=at_out_g[i], w_out=w_out[i])
        _, _, g1, sh2, sc2, g2 = mods[i]
        _, _, cg1, csh2, csc2, cg2 = cmods[i]
        parts_l, parts_c = token_mixer(hl, hc, lp, not last)
        peer_w = (peer_wq[i], peer_k1[i], peer_k2[i], peer_u[i], peer_v[i])
        out_w = (lp['hy_out_g'], lp['at_out_g'], norm2_g[i], lp['w_out'])
        flat = lambda parts: tuple(p.reshape(-1, p.shape[-1]) for p in parts)
        if last:
            next_l = (final_g, zeros, zeros)
        else:
            next_l = (norm1_g[i + 1], mods[i + 1][1], mods[i + 1][0])
        xl2, h2l = mix_out(flat(parts_l), xl, g1, sc2, sh2, *out_w, rows_per_mod=T)
        xl, hl = peer_ffn(h2l, *peer_w, xl2, g2, *next_l, rows_per_mod=T)
        hl = hl.reshape(B, T, D)
        if not last:
            xc2, h2c = mix_out(flat(parts_c), xc, cg1[None], csc2[None], csh2[None], *out_w, rows_per_mod=B * Lc)
            xc, hc = peer_ffn(h2c, *peer_w, xc2, cg2[None], norm1_g[i + 1], cmods[i + 1][1][None],
                              cmods[i + 1][0][None], rows_per_mod=B * Lc)
            hc = hc.reshape(B, Lc, D)
    return hl
```

```python
import functools
import math

import jax
import jax.numpy as jnp
import numpy as np
from jax import lax
from jax.experimental import pallas as pl
from jax.experimental.pallas import tpu as pltpu

D_MODEL = 1024
DEPTH = 4
GRID_W = 64

HY_CH = 256
RW_HEADS = 4
RW_HD = 64
RW_W = RW_HEADS * RW_HD
AT_HQ = 8
AT_HKV = 2
AT_G = AT_HQ // AT_HKV
AT_HD = 64
AT_W = AT_HQ * AT_HD
AT_KVW = AT_HKV * AT_HD

RW_DECAY_R = 64
RW_ICLR_R = 64
RW_GATE_R = 128
RW_LORA_W = 2 * RW_DECAY_R + 2 * RW_ICLR_R + RW_GATE_R
RW_GN_EPS = RW_HD * 1e-5

HY_EMB = 33
HY_BANDS = (HY_EMB - 1) // 2
HY_SHIFT = 0.05
HY_MIN_DECAY = math.log(1e-2) / 1.5
HY_MAX_DECAY = math.log(1e-2) / 0.3

ATT_WINDOW = 128
ATT_BLOCK = 128
ROPE_BASE = 10000.0
NEG_INF = -1e30

PEER_HEADS = 8
PEER_NKEYS = 128
PEER_DQ = 256
PEER_TOPK = 16
PEER_CHUNK = 128

OFF_HY = 0
OFF_RK = OFF_HY + 3 * HY_CH
OFF_LORA = OFF_RK + 3 * RW_W
OFF_AT = OFF_LORA + RW_LORA_W
CONV_COLS = OFF_LORA

VMEM_LIMIT = 48 * 1024 * 1024


def rmsnorm(x, g, eps=1e-6):
    y = x * lax.rsqrt(jnp.mean(x * x, axis=-1, keepdims=True) + eps)
    return y * g


def hyena_filter(L, w1, b1, w2, b2, w3, b3, freq):
    f32 = jnp.float32
    j = jnp.arange(L, dtype=f32)
    t = j / max(L - 1, 1)
    bands = jnp.linspace(1e-4, HY_BANDS - 1, HY_BANDS, dtype=f32)
    ang = 2.0 * math.pi * j[:, None] * bands[None, :] / L
    z = jnp.concatenate([t[:, None], jnp.cos(ang), -jnp.sin(ang)], axis=-1)
    h = jnp.sin(freq[0] * (z @ w1 + b1))
    h = jnp.sin(freq[1] * (h @ w2 + b2))
    h = (h @ w3 + b3).astype(f32).reshape(L, 2, HY_CH)
    deltas = jnp.abs(jnp.linspace(HY_MIN_DECAY, HY_MAX_DECAY, HY_CH, dtype=f32))
    window = jnp.exp(-t[:, None] * deltas[None, :]) + HY_SHIFT
    h = h * window[:, None, :]
    k = jnp.concatenate([h[:, 0], jnp.zeros((1, HY_CH), f32), h[:0:-1, 1]], axis=0)
    return k * lax.rsqrt(jnp.sum(k * k, axis=0, keepdims=True) + 1e-6)


HY_BLOCK = 256


def _hyena_kernel(fr_ref, u_ref, y_ref, *, nblk, nb):
    P = HY_BLOCK
    ncol = nblk * nb
    two_l = 2 * nblk * P
    u = u_ref[0].astype(jnp.bfloat16)
    col = lax.broadcasted_iota(jnp.int32, (P, ncol), 1)
    acc = jnp.zeros((P, ncol), jnp.float32)
    for m in range(-(nblk - 1), nblk):
        start = (-P * m) % two_l
        gr = fr_ref[0, :, start:start + 2 * P]
        rolled = pltpu.roll(jnp.broadcast_to(gr, (P, 2 * P)), P + 1, 1, stride=1, stride_axis=0)
        t_m = rolled[:, :P].astype(jnp.bfloat16)
        prod = jnp.dot(t_m, u, preferred_element_type=jnp.float32)
        if m == 0:
            acc = acc + prod
        elif m > 0:
            acc = acc + jnp.where(col >= nb * m, pltpu.roll(prod, nb * m, 1), 0.0)
        else:
            acc = acc + jnp.where(col < ncol + nb * m, pltpu.roll(prod, ncol + nb * m, 1), 0.0)
    y_ref[0] = acc


def hyena_conv(u, filt):
    B, L, C = u.shape
    P = HY_BLOCK
    assert L % P == 0 and filt.shape == (2 * L, C)
    nblk = L // P
    idx = (P - 1 - np.arange(2 * L + 2 * P)) % (2 * L)
    fr = filt[idx].T.reshape(C, 1, 2 * L + 2 * P)
    uc = u.reshape(B, nblk, P, C).transpose(3, 2, 1, 0).reshape(C, P, nblk * B)
    y = pl.pallas_call(
        functools.partial(_hyena_kernel, nblk=nblk, nb=B),
        grid=(C,),
        in_specs=[pl.BlockSpec((1, 1, 2 * L + 2 * P), lambda c: (c, 0, 0)),
                  pl.BlockSpec((1, P, nblk * B), lambda c: (c, 0, 0))],
        out_specs=pl.BlockSpec((1, P, nblk * B), lambda c: (c, 0, 0)),
        out_shape=jax.ShapeDtypeStruct((C, P, nblk * B), jnp.float32),
        compiler_params=pltpu.CompilerParams(dimension_semantics=("parallel",), vmem_limit_bytes=VMEM_LIMIT),
        name="hyena_conv",
    )(fr, uc)
    return y.reshape(C, P, nblk, B).transpose(3, 2, 1, 0).reshape(B, L, C)


RW_LANES = 128
RW_TCHUNK = 32


RW_SUBLANE = 8


def _rwkv_scan_kernel(rf, wf, kf, vf, af, bf_, rb, wb, kb, vb, ab, bb, yf_ref, yb_ref, s_ref, rows_ref, *, tc):
    @pl.when(pl.program_id(0) == 0)
    def _init():
        s_ref[...] = jnp.zeros_like(s_ref)

    fwd = lax.broadcasted_iota(jnp.int32, (RW_HD, RW_LANES), 1) < RW_LANES // 2
    vregs = [slice(g * RW_SUBLANE, (g + 1) * RW_SUBLANE) for g in range(RW_HD // RW_SUBLANE)]

    def step(t, carry):
        tb = tc - 1 - t
        rows_ref[0] = jnp.where(fwd, af[t], ab[tb])
        rows_ref[1] = jnp.where(fwd, wf[t], wb[tb])
        rows_ref[2] = jnp.where(fwd, bf_[t], bb[tb])
        rows_ref[3] = jnp.where(fwd, kf[t], kb[tb])
        rows_ref[4] = jnp.where(fwd, rf[t], rb[tb])
        v_t = jnp.where(fwd, vf[t], vb[tb])
        sa = [jnp.zeros((RW_SUBLANE, RW_LANES), jnp.float32) for _ in vregs]
        for k in range(RW_HD):
            a_b = rows_ref[0, pl.ds(k, 1), :]
            for g, sl in enumerate(vregs):
                sa[g] = sa[g] + s_ref[k, sl, :] * a_b
        y = [jnp.zeros((RW_SUBLANE, RW_LANES), jnp.float32) for _ in vregs]
        for k in range(RW_HD):
            w_b = rows_ref[1, pl.ds(k, 1), :]
            b_b = rows_ref[2, pl.ds(k, 1), :]
            k_b = rows_ref[3, pl.ds(k, 1), :]
            r_b = rows_ref[4, pl.ds(k, 1), :]
            for g, sl in enumerate(vregs):
                s_new = s_ref[k, sl, :] * w_b + sa[g] * b_b + v_t[sl] * k_b
                s_ref[k, sl, :] = s_new
                y[g] = y[g] + s_new * r_b
        for g, sl in enumerate(vregs):
            yf_ref[t, sl, :] = y[g]
            yb_ref[tb, sl, :] = y[g]
        return carry

    lax.fori_loop(0, tc, step, 0)


def rwkv_scan_lanes(r, w, k, v, a, b, n_ctx):
    L = r.shape[0]
    tc = RW_TCHUNK
    assert L % tc == 0 and n_ctx % tc == 0 and r.shape[1:] == (RW_HD, RW_LANES)
    nblk, cblk = L // tc, n_ctx // tc

    def fmap(i):
        return (i, 0, 0)

    def bmap(i):
        return (jnp.where(i < cblk, cblk - 1 - i, nblk - 1 + cblk - i), 0, 0)

    fspec = pl.BlockSpec((tc, RW_HD, RW_LANES), fmap)
    bspec = pl.BlockSpec((tc, RW_HD, RW_LANES), bmap)
    sds = jax.ShapeDtypeStruct((L, RW_HD, RW_LANES), jnp.float32)
    return pl.pallas_call(
        functools.partial(_rwkv_scan_kernel, tc=tc),
        grid=(nblk,),
        in_specs=[fspec] * 6 + [bspec] * 6,
        out_specs=[fspec, bspec],
        out_shape=[sds, sds],
        scratch_shapes=[pltpu.VMEM((RW_HD, RW_HD, RW_LANES), jnp.float32),
                        pltpu.VMEM((5, RW_HD, RW_LANES), jnp.float32)],
        compiler_params=pltpu.CompilerParams(dimension_semantics=("arbitrary",),
                                             vmem_limit_bytes=VMEM_LIMIT),
        name="rwkv_scan",
    )(r, w, k, v, a, b, r, w, k, v, a, b)


def rwkv7_mix(rkv_c, rkv_l, lora_c, lora_l, lp):
    f32 = jnp.float32
    B, Lc, _ = rkv_c.shape
    T = rkv_l.shape[1]
    Lt = Lc + T
    hs = (B, Lt, RW_HEADS, RW_HD)
    rkv = jnp.concatenate([rkv_c, rkv_l], axis=1)
    lora = jnp.concatenate([lora_c, lora_l], axis=1)
    r = rkv[..., :RW_W].reshape(hs)
    k = rkv[..., RW_W:2 * RW_W].reshape(hs)
    v = rkv[..., 2 * RW_W:].reshape(hs)
    kk = k * lp['rw_k_k'].reshape(RW_HEADS, RW_HD)
    kk = kk * lax.rsqrt(jnp.sum(kk * kk, axis=-1, keepdims=True) + 1e-12)
    k_a = lp['rw_k_a'].reshape(RW_HEADS, RW_HD)
    r_k = lp['rw_r_k']
    assert 2 * B * RW_HEADS == RW_LANES
    bonus = jnp.zeros(hs, f32)
    per_dir = []
    for d in range(2):
        wd = lora[..., d * RW_DECAY_R:(d + 1) * RW_DECAY_R]
        ad = lora[..., 2 * RW_DECAY_R + d * RW_ICLR_R:2 * RW_DECAY_R + (d + 1) * RW_ICLR_R]
        w_log = -jax.nn.softplus(-(lp['rw_w0'][d] + jnp.tanh(wd) @ lp['rw_w_up'][d])) - 0.5
        decay = jnp.exp(-jnp.exp(w_log)).reshape(hs)
        a = jax.nn.sigmoid(lp['rw_a0'][d] + ad @ lp['rw_a_up'][d]).reshape(hs)
        k_d = k * (1.0 + (a - 1.0) * k_a)
        per_dir.append((r, decay, k_d, v, -kk, kk * a))
        bonus = bonus + jnp.sum(r * k_d * r_k, axis=-1, keepdims=True) * v

    def lanes(xf, xb):
        return jnp.stack([xf, xb], axis=0).transpose(2, 4, 0, 1, 3).reshape(Lt, RW_HD, RW_LANES)

    yf, yb = rwkv_scan_lanes(*(lanes(xf, xb) for xf, xb in zip(*per_dir)), n_ctx=Lc)
    half = RW_LANES // 2
    y_sum = (yf[..., :half] + yb[..., half:]).reshape(Lt, RW_HD, B, RW_HEADS).transpose(2, 0, 3, 1)
    mu = jnp.mean(y_sum, axis=-1, keepdims=True)
    var = jnp.mean(jnp.square(y_sum - mu), axis=-1, keepdims=True)
    yn = (y_sum - mu) * lax.rsqrt(var + RW_GN_EPS)
    yn = yn * lp['rw_ln_w'].reshape(RW_HEADS, RW_HD) + lp['rw_ln_b'].reshape(RW_HEADS, RW_HD)
    g = jax.nn.sigmoid(lora[..., 2 * RW_DECAY_R + 2 * RW_ICLR_R:]) @ lp['rw_g_up']
    out = (yn + bonus).reshape(B, Lt, RW_W) * g
    return out[:, :Lc], out[:, Lc:]


def rope_tables(T):
    f32 = jnp.float32
    quarter = AT_HD // 4
    inv = ROPE_BASE ** (-jnp.arange(quarter, dtype=f32) / quarter)
    rows = T // GRID_W
    row = jnp.repeat(jnp.arange(rows), GRID_W).astype(f32)
    col = jnp.tile(jnp.arange(GRID_W), rows).astype(f32)
    out_c, out_s = [], []
    for pos in (row, col):
        ang = pos[:, None] * inv[None, :]
        c, s = jnp.cos(ang), jnp.sin(ang)
        out_c += [c, c]
        out_s += [-s, s]
    return jnp.concatenate(out_c, axis=1), jnp.concatenate(out_s, axis=1)


def _rope(x, c, s):
    lane = lax.broadcasted_iota(jnp.int32, x.shape, 1)
    first = (lane % 32) < 16
    partner = jnp.where(first, pltpu.roll(x, x.shape[1] - 16, axis=1), pltpu.roll(x, 16, axis=1))
    return x * c + partner * s


def _attn_kernel(q_ref, kp_ref, kc_ref, kn_ref, vp_ref, vc_ref, vn_ref, kx_ref, vx_ref, sink_ref,
                 cq_ref, sq_ref, ckp_ref, skp_ref, ckc_ref, skc_ref, ckn_ref, skn_ref, o_ref, *, nblk):
    i = pl.program_id(1)
    blk = ATT_BLOCK
    scale = AT_HD ** -0.5
    q = _rope(q_ref[0], cq_ref[...], sq_ref[...])
    kspan = jnp.concatenate([_rope(kp_ref[0], ckp_ref[...], skp_ref[...]),
                             _rope(kc_ref[0], ckc_ref[...], skc_ref[...]),
                             _rope(kn_ref[0], ckn_ref[...], skn_ref[...])], axis=0)
    vspan = jnp.concatenate([vp_ref[0], vc_ref[0], vn_ref[0]], axis=0)
    kx = kx_ref[0]
    vx = vx_ref[0]
    qpos = lax.broadcasted_iota(jnp.int32, (blk, 3 * blk), 0) + blk
    kpos = lax.broadcasted_iota(jnp.int32, (blk, 3 * blk), 1)
    valid = jnp.abs(qpos - kpos) <= ATT_WINDOW
    valid = valid & ((kpos >= blk) | (i > 0)) & ((kpos < 2 * blk) | (i < nblk - 1))
    valid4 = jnp.concatenate([valid] * AT_G, axis=0)
    nt = (((1,), (1,)), ((), ()))
    bf = jnp.bfloat16
    outs = []
    for g in range(AT_HKV):
        kg = kspan[:, g * AT_HD:(g + 1) * AT_HD].astype(bf)
        vg = vspan[:, g * AT_HD:(g + 1) * AT_HD].astype(bf)
        kxg = kx[:, g * AT_HD:(g + 1) * AT_HD].astype(bf)
        vxg = vx[:, g * AT_HD:(g + 1) * AT_HD].astype(bf)
        qg = jnp.concatenate([q[:, (g * AT_G + j) * AT_HD:(g * AT_G + j + 1) * AT_HD] for j in range(AT_G)],
                             axis=0).astype(bf)
        sk = jnp.concatenate([jnp.full((blk, 1), sink_ref[g * AT_G + j], jnp.float32) for j in range(AT_G)], axis=0)
        s_lat = lax.dot_general(qg, kg, nt, preferred_element_type=jnp.float32) * scale
        s_lat = jnp.where(valid4, s_lat, NEG_INF)
        s_ctx = lax.dot_general(qg, kxg, nt, preferred_element_type=jnp.float32) * scale
        m = jnp.maximum(jnp.maximum(jnp.max(s_lat, axis=1, keepdims=True), jnp.max(s_ctx, axis=1, keepdims=True)), sk)
        p_lat = jnp.exp(s_lat - m)
        p_ctx = jnp.exp(s_ctx - m)
        den = jnp.sum(p_lat, axis=1, keepdims=True) + jnp.sum(p_ctx, axis=1, keepdims=True) + jnp.exp(sk - m)
        inv = 1.0 / den
        o = (jnp.dot((p_lat * inv).astype(bf), vg, preferred_element_type=jnp.float32)
             + jnp.dot((p_ctx * inv).astype(bf), vxg, preferred_element_type=jnp.float32))
        outs += [o[j * blk:(j + 1) * blk] for j in range(AT_G)]
    o_ref[0] = jnp.concatenate(outs, axis=1)


def windowed_attention(q, k, v, kx, vx, sink):
    B, T, _ = q.shape
    Lc = kx.shape[1]
    assert T % ATT_BLOCK == 0 and ATT_WINDOW == ATT_BLOCK
    nblk = T // ATT_BLOCK
    c1, s1 = rope_tables(T)
    cq, sq = jnp.tile(c1, (1, AT_HQ)), jnp.tile(s1, (1, AT_HQ))
    ck, sk = jnp.tile(c1, (1, AT_HKV)), jnp.tile(s1, (1, AT_HKV))

    def prev(b, i):
        return (b, jnp.maximum(i - 1, 0), 0)

    def cur(b, i):
        return (b, i, 0)

    def nxt(b, i):
        return (b, jnp.minimum(i + 1, nblk - 1), 0)

    def kb(f):
        return pl.BlockSpec((1, ATT_BLOCK, AT_KVW), f)

    def tb(f):
        return pl.BlockSpec((ATT_BLOCK, AT_KVW), lambda b, i: f(b, i)[1:])

    return pl.pallas_call(
        functools.partial(_attn_kernel, nblk=nblk),
        grid=(B, nblk),
        in_specs=[pl.BlockSpec((1, ATT_BLOCK, AT_W), cur), kb(prev), kb(cur), kb(nxt), kb(prev), kb(cur), kb(nxt),
                  pl.BlockSpec((1, Lc, AT_KVW), lambda b, i: (b, 0, 0)),
                  pl.BlockSpec((1, Lc, AT_KVW), lambda b, i: (b, 0, 0)),
                  pl.BlockSpec(memory_space=pltpu.SMEM),
                  pl.BlockSpec((ATT_BLOCK, AT_W), lambda b, i: (i, 0)),
                  pl.BlockSpec((ATT_BLOCK, AT_W), lambda b, i: (i, 0)),
                  tb(prev), tb(prev), tb(cur), tb(cur), tb(nxt), tb(nxt)],
        out_specs=pl.BlockSpec((1, ATT_BLOCK, AT_W), cur),
        out_shape=jax.ShapeDtypeStruct((B, T, AT_W), jnp.float32),
        compiler_params=pltpu.CompilerParams(dimension_semantics=("parallel", "arbitrary"),
                                             vmem_limit_bytes=VMEM_LIMIT),
        name="window_attention",
    )(q, k, k, k, v, v, v, kx, vx, sink, cq, sq, ck, sk, ck, sk, ck, sk)


def context_attention(qc, kc, vc, sink):
    B, Lc = qc.shape[:2]
    scale = AT_HD ** -0.5
    s = jnp.einsum('bqgjd,bkgd->bgjqk', qc, kc) * scale
    sink_b = jnp.broadcast_to(sink[None, :, :, None, None], (B, AT_HKV, AT_G, Lc, 1))
    p = jax.nn.softmax(jnp.concatenate([s, sink_b], axis=-1), axis=-1)[..., :Lc]
    return jnp.einsum('bgjqk,bkgd->bqgjd', p, vc).reshape(B, Lc, AT_W)


INPROJ_ROWS = 256
SUBLANE = 8


def _inproj_kernel(h_ref, hp_ref, hn_ref, w_ref, cw_ref, cb_ref, x0_ref, u_ref, rkv_ref, lora_ref, q_ref, k_ref, v_ref,
                   *, tiles_per_seq):
    i = pl.program_id(0)
    tm = h_ref.shape[0]
    bf = jnp.bfloat16
    z = jnp.dot(h_ref[...].astype(bf), w_ref[...], preferred_element_type=jnp.float32)
    wc = w_ref[:, :CONV_COLS]
    zp = jnp.dot(hp_ref[...].astype(bf), wc, preferred_element_type=jnp.float32)[SUBLANE - 1:SUBLANE]
    zn = jnp.dot(hn_ref[...].astype(bf), wc, preferred_element_type=jnp.float32)[0:1]
    pos = i % tiles_per_seq
    zp = jnp.where(pos == 0, 0.0, zp)
    zn = jnp.where(pos == tiles_per_seq - 1, 0.0, zn)
    zc = z[:, :CONV_COLS]
    row = lax.broadcasted_iota(jnp.int32, zc.shape, 0)
    up = jnp.where(row == 0, zp, pltpu.roll(zc, 1, 0))
    dn = jnp.where(row == tm - 1, zn, pltpu.roll(zc, tm - 1, 0))
    s = up * cw_ref[0:1] + zc * cw_ref[1:2] + dn * cw_ref[2:3] + cb_ref[...]
    x0_ref[...] = s[:, :HY_CH]
    u_ref[...] = s[:, HY_CH:2 * HY_CH] * s[:, 2 * HY_CH:OFF_RK]
    rkv_ref[...] = s[:, OFF_RK:]
    lora_ref[...] = z[:, OFF_LORA:OFF_AT]
    q_ref[...] = z[:, OFF_AT:OFF_AT + AT_W]
    k_ref[...] = z[:, OFF_AT + AT_W:OFF_AT + AT_W + AT_KVW]
    v_ref[...] = z[:, OFF_AT + AT_W + AT_KVW:]


def in_proj(h3, w_in, conv_w, conv_b):
    nseq, seq_len, d = h3.shape
    n = nseq * seq_len
    tm = INPROJ_ROWS
    assert seq_len % tm == 0 and w_in.shape == (d, OFF_AT + AT_W + 2 * AT_KVW)
    tps = seq_len // tm
    nb8 = n // SUBLANE
    r8 = tm // SUBLANE
    h = h3.reshape(n, d)

    def out(w):
        return pl.BlockSpec((tm, w), lambda i: (i, 0))

    widths = (HY_CH, HY_CH, 3 * RW_W, RW_LORA_W, AT_W, AT_KVW, AT_KVW)
    outs = pl.pallas_call(
        functools.partial(_inproj_kernel, tiles_per_seq=tps),
        grid=(n // tm,),
        in_specs=[pl.BlockSpec((tm, d), lambda i: (i, 0)),
                  pl.BlockSpec((SUBLANE, d), lambda i: (jnp.maximum(i * r8 - 1, 0), 0)),
                  pl.BlockSpec((SUBLANE, d), lambda i: (jnp.minimum((i + 1) * r8, nb8 - 1), 0)),
                  pl.BlockSpec(w_in.shape, lambda i: (0, 0)),
                  pl.BlockSpec((3, CONV_COLS), lambda i: (0, 0)),
                  pl.BlockSpec((1, CONV_COLS), lambda i: (0, 0))],
        out_specs=[out(w) for w in widths],
        out_shape=[jax.ShapeDtypeStruct((n, w), jnp.float32) for w in widths],
        compiler_params=pltpu.CompilerParams(dimension_semantics=("parallel",), vmem_limit_bytes=VMEM_LIMIT),
        name="in_proj",
    )(h, h, h, w_in.astype(jnp.bfloat16), conv_w, conv_b.reshape(1, -1))
    return tuple(o.reshape(nseq, seq_len, o.shape[-1]) for o in outs)


def token_mixer(hl, hc, lp, need_ctx):
    B, T, _ = hl.shape
    Lc = hc.shape[1]
    x0_l, u_l, rkv_l, lora_l, q_l, k_l, v_l = in_proj(hl, lp['w_in'], lp['conv_w'], lp['conv_b'])
    x0_c, u_c, rkv_c, lora_c, q_c, k_c, v_c = in_proj(hc, lp['w_in'], lp['conv_w'], lp['conv_b'])
    filt_p = (lp['hy_w1'], lp['hy_b1'], lp['hy_w2'], lp['hy_b2'], lp['hy_w3'], lp['hy_b3'], lp['hy_freq'])
    hy_l = x0_l * (hyena_conv(u_l, hyena_filter(T, *filt_p)) + u_l * lp['hy_bias'])
    rw_c, rw_l = rwkv7_mix(rkv_c, rkv_l, lora_c, lora_l, lp)
    at_l = windowed_attention(q_l, k_l, v_l, k_c, v_c, lp['attn_sink'])
    parts_l = (hy_l, rw_l, at_l)
    if not need_ctx:
        return parts_l, None
    hy_c = x0_c * (hyena_conv(u_c, hyena_filter(Lc, *filt_p)) + u_c * lp['hy_bias'])
    at_c = context_attention(q_c.reshape(B, Lc, AT_HKV, AT_G, AT_HD), k_c.reshape(B, Lc, AT_HKV, AT_HD),
                             v_c.reshape(B, Lc, AT_HKV, AT_HD), lp['attn_sink'].reshape(AT_HKV, AT_G))
    return parts_l, (hy_c, rw_c, at_c)


NORM_EPS = 1e-6
MIXOUT_ROWS = 512


def _rms(x, g):
    return (x * lax.rsqrt(jnp.mean(x * x, axis=-1, keepdims=True) + NORM_EPS)) * g


def _mixout_kernel(hy_ref, rw_ref, at_ref, x_ref, g1_ref, sc2_ref, sh2_ref, hyg_ref, atg_ref, n2g_ref, w_ref,
                   xo_ref, h2_ref):
    mix = jnp.concatenate([_rms(hy_ref[...], hyg_ref[...]), rw_ref[...], _rms(at_ref[...], atg_ref[...])], axis=1)
    m = jnp.dot(mix.astype(jnp.bfloat16), w_ref[...], preferred_element_type=jnp.float32)
    xn = x_ref[...] + g1_ref[0] * m
    xo_ref[...] = xn
    h2_ref[...] = (_rms(xn, n2g_ref[...]) * (1.0 + sc2_ref[0]) + sh2_ref[0]).astype(h2_ref.dtype)


def mix_out(parts, x, gate1, scale2, shift2, hy_g, at_g, norm2_g, w_out, rows_per_mod):
    hy, rw, at = parts
    n, d = x.shape
    tm = MIXOUT_ROWS
    assert n % tm == 0 and rows_per_mod % tm == 0
    tpm = rows_per_mod // tm

    def row(w):
        return pl.BlockSpec((tm, w), lambda i: (i, 0))

    def vec(w):
        return pl.BlockSpec((1, w), lambda i: (0, 0))

    mod = pl.BlockSpec((1, 1, d), lambda i: (i // tpm, 0, 0))
    return pl.pallas_call(
        _mixout_kernel,
        grid=(n // tm,),
        in_specs=[row(hy.shape[1]), row(rw.shape[1]), row(at.shape[1]), row(d), mod, mod, mod,
                  vec(hy.shape[1]), vec(at.shape[1]), vec(d), pl.BlockSpec(w_out.shape, lambda i: (0, 0))],
        out_specs=[row(d), row(d)],
        out_shape=[jax.ShapeDtypeStruct((n, d), jnp.float32), jax.ShapeDtypeStruct((n, d), jnp.bfloat16)],
        compiler_params=pltpu.CompilerParams(dimension_semantics=("parallel",), vmem_limit_bytes=VMEM_LIMIT),
        name="mix_out",
    )(hy, rw, at, x, gate1.reshape(-1, 1, d), scale2.reshape(-1, 1, d), shift2.reshape(-1, 1, d),
      hy_g.reshape(1, -1), at_g.reshape(1, -1), norm2_g.reshape(1, -1), w_out.astype(jnp.bfloat16))


def _top16_rows(s, n_rows):
    rio = lax.broadcasted_iota(jnp.int32, s.shape, 0)
    vals, idxs = [], []
    for _ in range(PEER_TOPK):
        m = jnp.max(s, axis=0, keepdims=True)
        idx = jnp.min(jnp.where(s == m, rio, n_rows), axis=0, keepdims=True)
        vals.append(m)
        idxs.append(idx)
        s = jnp.where(rio == idx, -jnp.inf, s)
    return jnp.concatenate(vals, axis=0), jnp.concatenate(idxs, axis=0)


LANE = 128
ROUTE_UNROLL = 4


def _cand_layout():
    rows = [(0, b) for b in range(PEER_TOPK)]
    for a in range(1, 8):
        rows += [(a, b) for b in range(8)]
    rows += [(a, 0) for a in range(8, PEER_TOPK)]
    a = np.array([r[0] for r in rows])
    b = np.array([r[1] for r in rows])
    return a, b, (a + 1) * (b + 1) <= PEER_TOPK


def cand_table():
    a, b, valid = _cand_layout()
    cio = np.where(valid, a * PEER_TOPK + b, PEER_TOPK * PEER_TOPK).astype(np.int32)
    return jnp.asarray(np.broadcast_to(cio[:, None], (cio.shape[0], LANE)))


def _route_kernel(x_ref, wq_ref, k1_ref, k2_ref, cio_ref, i1_ref, i2_ref, g_ref, q_ref, *, tt):
    q = jnp.dot(x_ref[...].astype(jnp.bfloat16), wq_ref[...], preferred_element_type=jnp.float32)
    q_ref[...] = q.astype(jnp.bfloat16)
    nt = (((1,), (1,)), ((), ()))
    cio = cio_ref[...]
    ncand = PEER_TOPK * PEER_TOPK

    def chunk(c, carry):
        t0 = pl.multiple_of(c * LANE, LANE)
        qc = q_ref[pl.ds(t0, LANE), :]
        s1 = lax.dot_general(k1_ref[0], qc[:, :PEER_DQ // 2], nt, preferred_element_type=jnp.float32)
        s2 = lax.dot_general(k2_ref[0], qc[:, PEER_DQ // 2:], nt, preferred_element_type=jnp.float32)
        v1, j1 = _top16_rows(s1, PEER_NKEYS)
        v2, j2 = _top16_rows(s2, PEER_NKEYS)
        j1 = j1 * PEER_NKEYS
        cand = jnp.concatenate([v1[0:1] + v2] + [v1[a:a + 1] + v2[0:8] for a in range(1, 8)]
                               + [v1[8:16] + v2[0:1]], axis=0)
        ecand = jnp.concatenate([j1[0:1] + j2] + [j1[a:a + 1] + j2[0:8] for a in range(1, 8)]
                                + [j1[8:16] + j2[0:1]], axis=0)
        cand = jnp.where(cio < ncand, cand, -jnp.inf)
        scs, es = [], []
        for _ in range(PEER_TOPK):
            m = jnp.max(cand, axis=0, keepdims=True)
            cidx = jnp.min(jnp.where(cand == m, cio, ncand), axis=0, keepdims=True)
            sel = cio == cidx
            es.append(jnp.max(jnp.where(sel, ecand, -1), axis=0, keepdims=True))
            scs.append(m)
            cand = jnp.where(sel, -jnp.inf, cand)
        sc = jnp.concatenate(scs, axis=0)
        e = jnp.concatenate(es, axis=0)
        p = jnp.exp(sc - sc[0:1])
        i1_ref[0, :, pl.ds(t0, LANE)] = e >> 7
        i2_ref[0, :, pl.ds(t0, LANE)] = e & (PEER_NKEYS - 1)
        g_ref[0, :, pl.ds(t0, LANE)] = p / jnp.sum(p, axis=0, keepdims=True)
        return carry

    lax.fori_loop(0, tt // LANE, chunk, 0, unroll=ROUTE_UNROLL)


def peer_route(h, wq_bf, k1_bf, k2_bf, tt):
    n, d = h.shape
    assert n % tt == 0 and tt % (ROUTE_UNROLL * LANE) == 0
    cio = cand_table()
    out_sds = [jax.ShapeDtypeStruct((PEER_HEADS, PEER_TOPK, n), jnp.int32),
               jax.ShapeDtypeStruct((PEER_HEADS, PEER_TOPK, n), jnp.int32),
               jax.ShapeDtypeStruct((PEER_HEADS, PEER_TOPK, n), jnp.float32)]
    ospec = pl.BlockSpec((1, PEER_TOPK, tt), lambda i, hh: (hh, 0, i))
    return pl.pallas_call(
        functools.partial(_route_kernel, tt=tt),
        grid=(n // tt, PEER_HEADS),
        in_specs=[pl.BlockSpec((tt, d), lambda i, hh: (i, 0)),
                  pl.BlockSpec((d, PEER_DQ), lambda i, hh: (0, hh)),
                  pl.BlockSpec((1, PEER_NKEYS, PEER_DQ // 2), lambda i, hh: (hh, 0, 0)),
                  pl.BlockSpec((1, PEER_NKEYS, PEER_DQ // 2), lambda i, hh: (hh, 0, 0)),
                  pl.BlockSpec(cio.shape, lambda i, hh: (0, 0))],
        out_specs=[ospec, ospec, ospec],
        out_shape=out_sds,
        scratch_shapes=[pltpu.VMEM((tt, PEER_DQ), jnp.bfloat16)],
        compiler_params=pltpu.CompilerParams(dimension_semantics=("parallel", "arbitrary"),
                                             vmem_limit_bytes=VMEM_LIMIT),
        name="peer_route",
    )(h, wq_bf, k1_bf, k2_bf, cio)


def _gelu(x):
    return 0.5 * x * (1.0 + lax.erf(x * (2.0 ** -0.5)))


PEER_BUILD_GROUP = 16
PEER_VMEM_LIMIT = 56 * 1024 * 1024


def _expert_kernel(x_ref, i1_ref, i2_ref, g_ref, ut_ref, v_ref, res_ref, g2_ref, ng_ref, nsc_ref, nsh_ref,
                   o_ref, hn_ref, w_ref, rows_ref, xb_ref, *, tt, eb):
    j = pl.program_id(1)
    nk = PEER_NKEYS
    grp = PEER_BUILD_GROUP

    @pl.when(j == 0)
    def _build():
        xb_ref[...] = x_ref[...].astype(jnp.bfloat16)
        rows_ref[0] = i1_ref[...].reshape(nk, tt).astype(jnp.float32).T
        rows_ref[1] = i2_ref[...].reshape(nk, tt).astype(jnp.float32).T
        rows_ref[2] = g_ref[...].reshape(nk, tt).T
        kio = lax.broadcasted_iota(jnp.int32, (nk, nk), 0).astype(jnp.float32)
        nt = (((1,), (1,)), ((), ()))

        def body(gidx, carry):
            t0 = pl.multiple_of(gidx * grp, grp)
            tiles = []
            for s in range(grp):
                i1row = jnp.broadcast_to(rows_ref[0, pl.ds(t0 + s, 1), :], (nk, nk))
                i2row = jnp.broadcast_to(rows_ref[1, pl.ds(t0 + s, 1), :], (nk, nk))
                grow = jnp.broadcast_to(rows_ref[2, pl.ds(t0 + s, 1), :], (nk, nk))
                m1 = jnp.where(i1row == kio, grow, 0.0).astype(jnp.bfloat16)
                p2 = jnp.where(i2row == kio, 1.0, 0.0).astype(jnp.bfloat16)
                tiles.append(lax.dot_general(m1, p2, nt, preferred_element_type=jnp.float32))
            w_ref[:, pl.ds(t0, grp), :] = jnp.swapaxes(jnp.stack(tiles, axis=0), 0, 1).astype(jnp.bfloat16)
            return carry

        lax.fori_loop(0, tt // grp, body, 0)

    a = jnp.dot(xb_ref[...], ut_ref[...], preferred_element_type=jnp.float32)
    nrow = eb // nk
    wj = jnp.concatenate([w_ref[j * nrow + r] for r in range(nrow)], axis=1)
    cmat = (wj.astype(jnp.float32) * _gelu(a)).astype(jnp.bfloat16)
    contrib = jnp.dot(cmat, v_ref[...], preferred_element_type=jnp.float32)

    last = pl.num_programs(1) - 1

    @pl.when(j == 0)
    def _first():
        o_ref[...] = contrib

    @pl.when((j > 0) & (j < last))
    def _middle():
        o_ref[...] += contrib

    @pl.when(j == last)
    def _finish():
        xn = res_ref[...] + g2_ref[0] * (o_ref[...] + contrib)
        o_ref[...] = xn
        hn_ref[...] = _rms(xn, ng_ref[...]) * (1.0 + nsc_ref[0]) + nsh_ref[0]


def peer_experts(h, i1, i2, g, ut_bf, v_bf, res, gate2, next_g, next_scale, next_shift, rows_per_mod, tt, eb):
    n, d = h.shape
    nexp = v_bf.shape[0]
    nblk = nexp // eb
    assert n % tt == 0 and nexp % eb == 0 and nblk >= 2 and nexp == PEER_NKEYS * PEER_NKEYS
    assert rows_per_mod % tt == 0
    tpm = rows_per_mod // tt
    rspec = pl.BlockSpec((PEER_HEADS, PEER_TOPK, tt), lambda i, j: (0, 0, i))
    tile = pl.BlockSpec((tt, d), lambda i, j: (i, 0))
    mod = pl.BlockSpec((1, 1, d), lambda i, j: (i // tpm, 0, 0))
    sds = jax.ShapeDtypeStruct((n, d), jnp.float32)
    return pl.pallas_call(
        functools.partial(_expert_kernel, tt=tt, eb=eb),
        grid=(n // tt, nblk),
        in_specs=[tile, rspec, rspec, rspec,
                  pl.BlockSpec((d, eb), lambda i, j: (0, j)),
                  pl.BlockSpec((eb, d), lambda i, j: (j, 0)),
                  tile, mod, pl.BlockSpec((1, d), lambda i, j: (0, 0)), mod, mod],
        out_specs=[tile, tile],
        out_shape=[sds, sds],
        scratch_shapes=[pltpu.VMEM((PEER_NKEYS, tt, PEER_NKEYS), jnp.bfloat16),
                        pltpu.VMEM((3, tt, PEER_NKEYS), jnp.float32),
                        pltpu.VMEM((tt, d), jnp.bfloat16)],
        compiler_params=pltpu.CompilerParams(dimension_semantics=("parallel", "arbitrary"),
                                             vmem_limit_bytes=PEER_VMEM_LIMIT),
        name="peer_experts",
    )(h, i1, i2, g, ut_bf, v_bf, res, gate2.reshape(-1, 1, d), next_g.reshape(1, d),
      next_scale.reshape(-1, 1, d), next_shift.reshape(-1, 1, d))


PEER_ROUTE_TILE = 512
PEER_TOKEN_TILE = 512
PEER_EXPERT_BLOCK = 1024


def peer_ffn(h, wq, k1, k2, u, v, res, gate2, next_g, next_scale, next_shift, rows_per_mod):
    bf = jnp.bfloat16
    i1, i2, g = peer_route(h, wq.astype(bf), k1.astype(bf), k2.astype(bf), PEER_ROUTE_TILE)
    return peer_experts(h, i1, i2, g, u.T.astype(bf), v.astype(bf), res, gate2, next_g, next_scale, next_shift,
                        rows_per_mod, PEER_TOKEN_TILE, PEER_EXPERT_BLOCK)


def kernel(x, c, ctx, c_ctx, mod_w, mod_b, norm1_g, norm2_g, w_in, conv_w, conv_b, hy_w1, hy_b1, hy_w2, hy_b2, hy_w3, hy_b3, hy_freq, hy_bias, rw_w0, rw_w_up, rw_a0, rw_a_up, rw_g_up, rw_k_k, rw_k_a, rw_r_k, rw_ln_w, rw_ln_b, attn_sink, hy_out_g, at_out_g, w_out, peer_wq, peer_k1, peer_k2, peer_u, peer_v, final_g):
    B, T, D = x.shape
    Lc = ctx.shape[1]
    c_s = jax.nn.silu(c)
    cc_s = jax.nn.silu(c_ctx)
    mods = [jnp.split(c_s @ mod_w[i] + mod_b[i], 6, axis=-1) for i in range(DEPTH)]
    cmods = [jnp.split(cc_s @ mod_w[i] + mod_b[i], 6, axis=-1) for i in range(DEPTH)]
    xl, xc = x.reshape(B * T, D), ctx.reshape(B * Lc, D)
    hl = rmsnorm(x, norm1_g[0]) * (1.0 + mods[0][1][:, None, :]) + mods[0][0][:, None, :]
    hc = rmsnorm(ctx, norm1_g[0]) * (1.0 + cmods[0][1]) + cmods[0][0]
    zeros = jnp.zeros((B, D), jnp.float32)
    for i in range(DEPTH):
        last = i == DEPTH - 1
        lp = dict(w_in=w_in[i], conv_w=conv_w[i], conv_b=conv_b[i],
                  hy_w1=hy_w1[i], hy_b1=hy_b1[i], hy_w2=hy_w2[i], hy_b2=hy_b2[i],
                  hy_w3=hy_w3[i], hy_b3=hy_b3[i], hy_freq=hy_freq[i], hy_bias=hy_bias[i],
                  rw_w0=rw_w0[i], rw_w_up=rw_w_up[i], rw_a0=rw_a0[i], rw_a_up=rw_a_up[i],
                  rw_g_up=rw_g_up[i], rw_k_k=rw_k_k[i], rw_k_a=rw_k_a[i], rw_r_k=rw_r_k[i],
                  rw_ln_w=rw_ln_w[i], rw_ln_b=rw_ln_b[i], attn_sink=attn_sink[i],
                  hy_out_g=hy_out_g[i], at_out_g=at_out_g[i], w_out=w_out[i])
        _, _, g1, sh2, sc2, g2 = mods[i]
        _, _, cg1, csh2, csc2, cg2 = cmods[i]
        parts_l, parts_c = token_mixer(hl, hc, lp, not last)
        peer_w = (peer_wq[i], peer_k1[i], peer_k2[i], peer_u[i], peer_v[i])
        out_w = (lp['hy_out_g'], lp['at_out_g'], norm2_g[i], lp['w_out'])
        flat = lambda parts: tuple(p.reshape(-1, p.shape[-1]) for p in parts)
        if last:
            next_l = (final_g, zeros, zeros)
        else:
            next_l = (norm1_g[i + 1], mods[i + 1][1], mods[i + 1][0])
        xl2, h2l = mix_out(flat(parts_l), xl, g1, sc2, sh2, *out_w, rows_per_mod=T)
        xl, hl = peer_ffn(h2l, *peer_w, xl2, g2, *next_l, rows_per_mod=T)
        hl = hl.reshape(B, T, D)
        if not last:
            xc2, h2c = mix_out(flat(parts_c), xc, cg1[None], csc2[None], csh2[None], *out_w, rows_per_mod=B * Lc)
            xc, hc = peer_ffn(h2c, *peer_w, xc2, cg2[None], norm1_g[i + 1], cmods[i + 1][1][None],
                              cmods[i + 1][0][None], rows_per_mod=B * Lc)
            hc = hc.reshape(B, Lc, D)
    return hl
```

```python
import functools
import math

import jax
import jax.numpy as jnp
import numpy as np
from jax import lax
from jax.experimental import pallas as pl
from jax.experimental.pallas import tpu as pltpu

D_MODEL = 1024
DEPTH = 4
GRID_W = 64

HY_CH = 256
RW_HEADS = 4
RW_HD = 64
RW_W = RW_HEADS * RW_HD
AT_HQ = 8
AT_HKV = 2
AT_G = AT_HQ // AT_HKV
AT_HD = 64
AT_W = AT_HQ * AT_HD
AT_KVW = AT_HKV * AT_HD

RW_DECAY_R = 64
RW_ICLR_R = 64
RW_GATE_R = 128
RW_LORA_W = 2 * RW_DECAY_R + 2 * RW_ICLR_R + RW_GATE_R
RW_GN_EPS = RW_HD * 1e-5

HY_EMB = 33
HY_BANDS = (HY_EMB - 1) // 2
HY_SHIFT = 0.05
HY_MIN_DECAY = math.log(1e-2) / 1.5
HY_MAX_DECAY = math.log(1e-2) / 0.3

ATT_WINDOW = 128
ATT_BLOCK = 128
ROPE_BASE = 10000.0
NEG_INF = -1e30

PEER_HEADS = 8
PEER_NKEYS = 128
PEER_DQ = 256
PEER_TOPK = 16
PEER_CHUNK = 128

OFF_HY = 0
OFF_RK = OFF_HY + 3 * HY_CH
OFF_LORA = OFF_RK + 3 * RW_W
OFF_AT = OFF_LORA + RW_LORA_W
CONV_COLS = OFF_LORA

VMEM_LIMIT = 48 * 1024 * 1024


def rmsnorm(x, g, eps=1e-6):
    y = x * lax.rsqrt(jnp.mean(x * x, axis=-1, keepdims=True) + eps)
    return y * g


def hyena_filter(L, w1, b1, w2, b2, w3, b3, freq):
    f32 = jnp.float32
    j = jnp.arange(L, dtype=f32)
    t = j / max(L - 1, 1)
    bands = jnp.linspace(1e-4, HY_BANDS - 1, HY_BANDS, dtype=f32)
    ang = 2.0 * math.pi * j[:, None] * bands[None, :] / L
    z = jnp.concatenate([t[:, None], jnp.cos(ang), -jnp.sin(ang)], axis=-1)
    h = jnp.sin(freq[0] * (z @ w1 + b1))
    h = jnp.sin(freq[1] * (h @ w2 + b2))
    h = (h @ w3 + b3).astype(f32).reshape(L, 2, HY_CH)
    deltas = jnp.abs(jnp.linspace(HY_MIN_DECAY, HY_MAX_DECAY, HY_CH, dtype=f32))
    window = jnp.exp(-t[:, None] * deltas[None, :]) + HY_SHIFT
    h = h * window[:, None, :]
    k = jnp.concatenate([h[:, 0], jnp.zeros((1, HY_CH), f32), h[:0:-1, 1]], axis=0)
    return k * lax.rsqrt(jnp.sum(k * k, axis=0, keepdims=True) + 1e-6)


HY_BLOCK = 256


def _hyena_kernel(fr_ref, u_ref, y_ref, *, nblk, nb):
    P = HY_BLOCK
    ncol = nblk * nb
    two_l = 2 * nblk * P
    u = u_ref[0].astype(jnp.bfloat16)
    col = lax.broadcasted_iota(jnp.int32, (P, ncol), 1)
    acc = jnp.zeros((P, ncol), jnp.float32)
    for m in range(-(nblk - 1), nblk):
        start = (-P * m) % two_l
        gr = fr_ref[0, :, start:start + 2 * P]
        rolled = pltpu.roll(jnp.broadcast_to(gr, (P, 2 * P)), P + 1, 1, stride=1, stride_axis=0)
        t_m = rolled[:, :P].astype(jnp.bfloat16)
        prod = jnp.dot(t_m, u, preferred_element_type=jnp.float32)
        if m == 0:
            acc = acc + prod
        elif m > 0:
            acc = acc + jnp.where(col >= nb * m, pltpu.roll(prod, nb * m, 1), 0.0)
        else:
            acc = acc + jnp.where(col < ncol + nb * m, pltpu.roll(prod, ncol + nb * m, 1), 0.0)
    y_ref[0] = acc


def hyena_conv(u, filt):
    B, L, C = u.shape
    P = HY_BLOCK
    assert L % P == 0 and filt.shape == (2 * L, C)
    nblk = L // P
    idx = (P - 1 - np.arange(2 * L + 2 * P)) % (2 * L)
    fr = filt[idx].T.reshape(C, 1, 2 * L + 2 * P)
    uc = u.reshape(B, nblk, P, C).transpose(3, 2, 1, 0).reshape(C, P, nblk * B)
    y = pl.pallas_call(
        functools.partial(_hyena_kernel, nblk=nblk, nb=B),
        grid=(C,),
        in_specs=[pl.BlockSpec((1, 1, 2 * L + 2 * P), lambda c: (c, 0, 0)),
                  pl.BlockSpec((1, P, nblk * B), lambda c: (c, 0, 0))],
        out_specs=pl.BlockSpec((1, P, nblk * B), lambda c: (c, 0, 0)),
        out_shape=jax.ShapeDtypeStruct((C, P, nblk * B), jnp.float32),
        compiler_params=pltpu.CompilerParams(dimension_semantics=("parallel",), vmem_limit_bytes=VMEM_LIMIT),
        name="hyena_conv",
    )(fr, uc)
    return y.reshape(C, P, nblk, B).transpose(3, 2, 1, 0).reshape(B, L, C)


RW_LANES = 128
RW_TCHUNK = 32


RW_SUBLANE = 8


def _rwkv_scan_kernel(rf, wf, kf, vf, af, bf_, rb, wb, kb, vb, ab, bb, yf_ref, yb_ref, s_ref, rows_ref, *, tc):
    @pl.when(pl.program_id(0) == 0)
    def _init():
        s_ref[...] = jnp.zeros_like(s_ref)

    fwd = lax.broadcasted_iota(jnp.int32, (RW_HD, RW_LANES), 1) < RW_LANES // 2
    vregs = [slice(g * RW_SUBLANE, (g + 1) * RW_SUBLANE) for g in range(RW_HD // RW_SUBLANE)]

    def step(t, carry):
        tb = tc - 1 - t
        rows_ref[0] = jnp.where(fwd, af[t], ab[tb])
        rows_ref[1] = jnp.where(fwd, wf[t], wb[tb])
        rows_ref[2] = jnp.where(fwd, bf_[t], bb[tb])
        rows_ref[3] = jnp.where(fwd, kf[t], kb[tb])
        rows_ref[4] = jnp.where(fwd, rf[t], rb[tb])
        v_t = jnp.where(fwd, vf[t], vb[tb])
        sa = [jnp.zeros((RW_SUBLANE, RW_LANES), jnp.float32) for _ in vregs]
        for k in range(RW_HD):
            a_b = rows_ref[0, pl.ds(k, 1), :]
            for g, sl in enumerate(vregs):
                sa[g] = sa[g] + s_ref[k, sl, :] * a_b
        y = [jnp.zeros((RW_SUBLANE, RW_LANES), jnp.float32) for _ in vregs]
        for k in range(RW_HD):
            w_b = rows_ref[1, pl.ds(k, 1), :]
            b_b = rows_ref[2, pl.ds(k, 1), :]
            k_b = rows_ref[3, pl.ds(k, 1), :]
            r_b = rows_ref[4, pl.ds(k, 1), :]
            for g, sl in enumerate(vregs):
                s_new = s_ref[k, sl, :] * w_b + sa[g] * b_b + v_t[sl] * k_b
                s_ref[k, sl, :] = s_new
                y[g] = y[g] + s_new * r_b
        for g, sl in enumerate(vregs):
            yf_ref[t, sl, :] = y[g]
            yb_ref[tb, sl, :] = y[g]
        return carry

    lax.fori_loop(0, tc, step, 0)


def rwkv_scan_lanes(r, w, k, v, a, b, n_ctx):
    L = r.shape[0]
    tc = RW_TCHUNK
    assert L % tc == 0 and n_ctx % tc == 0 and r.shape[1:] == (RW_HD, RW_LANES)
    nblk, cblk = L // tc, n_ctx // tc

    def fmap(i):
        return (i, 0, 0)

    def bmap(i):
        return (jnp.where(i < cblk, cblk - 1 - i, nblk - 1 + cblk - i), 0, 0)

    fspec = pl.BlockSpec((tc, RW_HD, RW_LANES), fmap)
    bspec = pl.BlockSpec((tc, RW_HD, RW_LANES), bmap)
    sds = jax.ShapeDtypeStruct((L, RW_HD, RW_LANES), jnp.float32)
    return pl.pallas_call(
        functools.partial(_rwkv_scan_kernel, tc=tc),
        grid=(nblk,),
        in_specs=[fspec] * 6 + [bspec] * 6,
        out_specs=[fspec, bspec],
        out_shape=[sds, sds],
        scratch_shapes=[pltpu.VMEM((RW_HD, RW_HD, RW_LANES), jnp.float32),
                        pltpu.VMEM((5, RW_HD, RW_LANES), jnp.float32)],
        compiler_params=pltpu.CompilerParams(dimension_semantics=("arbitrary",),
                                             vmem_limit_bytes=VMEM_LIMIT),
        name="rwkv_scan",
    )(r, w, k, v, a, b, r, w, k, v, a, b)


def rwkv7_mix(rkv_c, rkv_l, lora_c, lora_l, lp):
    f32 = jnp.float32
    B, Lc, _ = rkv_c.shape
    T = rkv_l.shape[1]
    Lt = Lc + T
    hs = (B, Lt, RW_HEADS, RW_HD)
    rkv = jnp.concatenate([rkv_c, rkv_l], axis=1)
    lora = jnp.concatenate([lora_c, lora_l], axis=1)
    r = rkv[..., :RW_W].reshape(hs)
    k = rkv[..., RW_W:2 * RW_W].reshape(hs)
    v = rkv[..., 2 * RW_W:].reshape(hs)
    kk = k * lp['rw_k_k'].reshape(RW_HEADS, RW_HD)
    kk = kk * lax.rsqrt(jnp.sum(kk * kk, axis=-1, keepdims=True) + 1e-12)
    k_a = lp['rw_k_a'].reshape(RW_HEADS, RW_HD)
    r_k = lp['rw_r_k']
    assert 2 * B * RW_HEADS == RW_LANES
    bonus = jnp.zeros(hs, f32)
    per_dir = []
    for d in range(2):
        wd = lora[..., d * RW_DECAY_R:(d + 1) * RW_DECAY_R]
        ad = lora[..., 2 * RW_DECAY_R + d * RW_ICLR_R:2 * RW_DECAY_R + (d + 1) * RW_ICLR_R]
        w_log = -jax.nn.softplus(-(lp['rw_w0'][d] + jnp.tanh(wd) @ lp['rw_w_up'][d])) - 0.5
        decay = jnp.exp(-jnp.exp(w_log)).reshape(hs)
        a = jax.nn.sigmoid(lp['rw_a0'][d] + ad @ lp['rw_a_up'][d]).reshape(hs)
        k_d = k * (1.0 + (a - 1.0) * k_a)
        per_dir.append((r, decay, k_d, v, -kk, kk * a))
        bonus = bonus + jnp.sum(r * k_d * r_k, axis=-1, keepdims=True) * v

    def lanes(xf, xb):
        return jnp.concatenate([x.transpose(1, 3, 0, 2).reshape(Lt, RW_HD, B * RW_HEADS) for x in (xf, xb)], axis=-1)

    yf, yb = rwkv_scan_lanes(*(lanes(xf, xb) for xf, xb in zip(*per_dir)), n_ctx=Lc)
    half = RW_LANES // 2
    y_sum = (yf[..., :half] + yb[..., half:]).reshape(Lt, RW_HD, B, RW_HEADS).transpose(2, 0, 3, 1)
    mu = jnp.mean(y_sum, axis=-1, keepdims=True)
    var = jnp.mean(jnp.square(y_sum - mu), axis=-1, keepdims=True)
    yn = (y_sum - mu) * lax.rsqrt(var + RW_GN_EPS)
    yn = yn * lp['rw_ln_w'].reshape(RW_HEADS, RW_HD) + lp['rw_ln_b'].reshape(RW_HEADS, RW_HD)
    g = jax.nn.sigmoid(lora[..., 2 * RW_DECAY_R + 2 * RW_ICLR_R:]) @ lp['rw_g_up']
    out = (yn + bonus).reshape(B, Lt, RW_W) * g
    return out[:, :Lc], out[:, Lc:]


def rope_tables(T):
    f32 = jnp.float32
    quarter = AT_HD // 4
    inv = ROPE_BASE ** (-jnp.arange(quarter, dtype=f32) / quarter)
    rows = T // GRID_W
    row = jnp.repeat(jnp.arange(rows), GRID_W).astype(f32)
    col = jnp.tile(jnp.arange(GRID_W), rows).astype(f32)
    out_c, out_s = [], []
    for pos in (row, col):
        ang = pos[:, None] * inv[None, :]
        c, s = jnp.cos(ang), jnp.sin(ang)
        out_c += [c, c]
        out_s += [-s, s]
    return jnp.concatenate(out_c, axis=1), jnp.concatenate(out_s, axis=1)


def _rope(x, c, s):
    lane = lax.broadcasted_iota(jnp.int32, x.shape, 1)
    first = (lane % 32) < 16
    partner = jnp.where(first, pltpu.roll(x, x.shape[1] - 16, axis=1), pltpu.roll(x, 16, axis=1))
    return x * c + partner * s


def _attn_kernel(q_ref, kp_ref, kc_ref, kn_ref, vp_ref, vc_ref, vn_ref, kx_ref, vx_ref, sink_ref,
                 cq_ref, sq_ref, ckp_ref, skp_ref, ckc_ref, skc_ref, ckn_ref, skn_ref, o_ref, *, nblk):
    i = pl.program_id(1)
    blk = ATT_BLOCK
    scale = AT_HD ** -0.5
    q = _rope(q_ref[0], cq_ref[...], sq_ref[...])
    kspan = jnp.concatenate([_rope(kp_ref[0], ckp_ref[...], skp_ref[...]),
                             _rope(kc_ref[0], ckc_ref[...], skc_ref[...]),
                             _rope(kn_ref[0], ckn_ref[...], skn_ref[...])], axis=0)
    vspan = jnp.concatenate([vp_ref[0], vc_ref[0], vn_ref[0]], axis=0)
    kx = kx_ref[0]
    vx = vx_ref[0]
    qpos = lax.broadcasted_iota(jnp.int32, (blk, 3 * blk), 0) + blk
    kpos = lax.broadcasted_iota(jnp.int32, (blk, 3 * blk), 1)
    valid = jnp.abs(qpos - kpos) <= ATT_WINDOW
    valid = valid & ((kpos >= blk) | (i > 0)) & ((kpos < 2 * blk) | (i < nblk - 1))
    valid8 = jnp.concatenate([valid] * AT_HQ, axis=0)
    nt = (((1,), (1,)), ((), ()))
    bf = jnp.bfloat16
    s_lat, s_ctx, sks = [], [], []
    for g in range(AT_HKV):
        kg = kspan[:, g * AT_HD:(g + 1) * AT_HD].astype(bf)
        kxg = kx[:, g * AT_HD:(g + 1) * AT_HD].astype(bf)
        qg = jnp.concatenate([q[:, (g * AT_G + j) * AT_HD:(g * AT_G + j + 1) * AT_HD] for j in range(AT_G)],
                             axis=0).astype(bf)
        s_lat.append(lax.dot_general(qg, kg, nt, preferred_element_type=jnp.float32))
        s_ctx.append(lax.dot_general(qg, kxg, nt, preferred_element_type=jnp.float32))
        sks += [jnp.full((blk, 1), sink_ref[g * AT_G + j], jnp.float32) for j in range(AT_G)]
    s_lat = jnp.where(valid8, jnp.concatenate(s_lat, axis=0) * scale, NEG_INF)
    s_ctx = jnp.concatenate(s_ctx, axis=0) * scale
    sk = jnp.concatenate(sks, axis=0)
    m = jnp.maximum(jnp.maximum(jnp.max(s_lat, axis=1, keepdims=True), jnp.max(s_ctx, axis=1, keepdims=True)), sk)
    p_lat = jnp.exp(s_lat - m)
    p_ctx = jnp.exp(s_ctx - m)
    den = jnp.sum(p_lat, axis=1, keepdims=True) + jnp.sum(p_ctx, axis=1, keepdims=True) + jnp.exp(sk - m)
    inv = 1.0 / den
    p_lat = (p_lat * inv).astype(bf)
    p_ctx = (p_ctx * inv).astype(bf)
    outs = []
    rows = AT_G * blk
    for g in range(AT_HKV):
        vg = vspan[:, g * AT_HD:(g + 1) * AT_HD].astype(bf)
        vxg = vx[:, g * AT_HD:(g + 1) * AT_HD].astype(bf)
        o = (jnp.dot(p_lat[g * rows:(g + 1) * rows], vg, preferred_element_type=jnp.float32)
             + jnp.dot(p_ctx[g * rows:(g + 1) * rows], vxg, preferred_element_type=jnp.float32))
        outs += [o[j * blk:(j + 1) * blk] for j in range(AT_G)]
    o_ref[0] = jnp.concatenate(outs, axis=1)


def windowed_attention(q, k, v, kx, vx, sink):
    B, T, _ = q.shape
    Lc = kx.shape[1]
    assert T % ATT_BLOCK == 0 and ATT_WINDOW == ATT_BLOCK
    nblk = T // ATT_BLOCK
    c1, s1 = rope_tables(T)
    cq, sq = jnp.tile(c1, (1, AT_HQ)), jnp.tile(s1, (1, AT_HQ))
    ck, sk = jnp.tile(c1, (1, AT_HKV)), jnp.tile(s1, (1, AT_HKV))

    def prev(b, i):
        return (b, jnp.maximum(i - 1, 0), 0)

    def cur(b, i):
        return (b, i, 0)

    def nxt(b, i):
        return (b, jnp.minimum(i + 1, nblk - 1), 0)

    def kb(f):
        return pl.BlockSpec((1, ATT_BLOCK, AT_KVW), f)

    def tb(f):
        return pl.BlockSpec((ATT_BLOCK, AT_KVW), lambda b, i: f(b, i)[1:])

    return pl.pallas_call(
        functools.partial(_attn_kernel, nblk=nblk),
        grid=(B, nblk),
        in_specs=[pl.BlockSpec((1, ATT_BLOCK, AT_W), cur), kb(prev), kb(cur), kb(nxt), kb(prev), kb(cur), kb(nxt),
                  pl.BlockSpec((1, Lc, AT_KVW), lambda b, i: (b, 0, 0)),
                  pl.BlockSpec((1, Lc, AT_KVW), lambda b, i: (b, 0, 0)),
                  pl.BlockSpec(memory_space=pltpu.SMEM),
                  pl.BlockSpec((ATT_BLOCK, AT_W), lambda b, i: (i, 0)),
                  pl.BlockSpec((ATT_BLOCK, AT_W), lambda b, i: (i, 0)),
                  tb(prev), tb(prev), tb(cur), tb(cur), tb(nxt), tb(nxt)],
        out_specs=pl.BlockSpec((1, ATT_BLOCK, AT_W), cur),
        out_shape=jax.ShapeDtypeStruct((B, T, AT_W), jnp.float32),
        compiler_params=pltpu.CompilerParams(dimension_semantics=("parallel", "arbitrary"),
                                             vmem_limit_bytes=VMEM_LIMIT),
        name="window_attention",
    )(q, k, k, k, v, v, v, kx, vx, sink, cq, sq, ck, sk, ck, sk, ck, sk)


def context_attention(qc, kc, vc, sink):
    B, Lc = qc.shape[:2]
    scale = AT_HD ** -0.5
    s = jnp.einsum('bqgjd,bkgd->bgjqk', qc, kc) * scale
    sink_b = jnp.broadcast_to(sink[None, :, :, None, None], (B, AT_HKV, AT_G, Lc, 1))
    p = jax.nn.softmax(jnp.concatenate([s, sink_b], axis=-1), axis=-1)[..., :Lc]
    return jnp.einsum('bgjqk,bkgd->bqgjd', p, vc).reshape(B, Lc, AT_W)


INPROJ_ROWS = 256
SUBLANE = 8


def _inproj_kernel(h_ref, hp_ref, hn_ref, w_ref, cw_ref, cb_ref, x0_ref, u_ref, rkv_ref, lora_ref, q_ref, k_ref, v_ref,
                   *, tiles_per_seq):
    i = pl.program_id(0)
    tm = h_ref.shape[0]
    bf = jnp.bfloat16
    z = jnp.dot(h_ref[...].astype(bf), w_ref[...], preferred_element_type=jnp.float32)
    wc = w_ref[:, :CONV_COLS]
    zp = jnp.dot(hp_ref[...].astype(bf), wc, preferred_element_type=jnp.float32)[SUBLANE - 1:SUBLANE]
    zn = jnp.dot(hn_ref[...].astype(bf), wc, preferred_element_type=jnp.float32)[0:1]
    pos = i % tiles_per_seq
    zp = jnp.where(pos == 0, 0.0, zp)
    zn = jnp.where(pos == tiles_per_seq - 1, 0.0, zn)
    zc = z[:, :CONV_COLS]
    row = lax.broadcasted_iota(jnp.int32, zc.shape, 0)
    up = jnp.where(row == 0, zp, pltpu.roll(zc, 1, 0))
    dn = jnp.where(row == tm - 1, zn, pltpu.roll(zc, tm - 1, 0))
    s = up * cw_ref[0:1] + zc * cw_ref[1:2] + dn * cw_ref[2:3] + cb_ref[...]
    x0_ref[...] = s[:, :HY_CH]
    u_ref[...] = s[:, HY_CH:2 * HY_CH] * s[:, 2 * HY_CH:OFF_RK]
    rkv_ref[...] = s[:, OFF_RK:]
    lora_ref[...] = z[:, OFF_LORA:OFF_AT]
    q_ref[...] = z[:, OFF_AT:OFF_AT + AT_W]
    k_ref[...] = z[:, OFF_AT + AT_W:OFF_AT + AT_W + AT_KVW]
    v_ref[...] = z[:, OFF_AT + AT_W + AT_KVW:]


def in_proj(h3, w_in, conv_w, conv_b):
    nseq, seq_len, d = h3.shape
    n = nseq * seq_len
    tm = INPROJ_ROWS
    assert seq_len % tm == 0 and w_in.shape == (d, OFF_AT + AT_W + 2 * AT_KVW)
    tps = seq_len // tm
    nb8 = n // SUBLANE
    r8 = tm // SUBLANE
    h = h3.reshape(n, d)

    def out(w):
        return pl.BlockSpec((tm, w), lambda i: (i, 0))

    widths = (HY_CH, HY_CH, 3 * RW_W, RW_LORA_W, AT_W, AT_KVW, AT_KVW)
    outs = pl.pallas_call(
        functools.partial(_inproj_kernel, tiles_per_seq=tps),
        grid=(n // tm,),
        in_specs=[pl.BlockSpec((tm, d), lambda i: (i, 0)),
                  pl.BlockSpec((SUBLANE, d), lambda i: (jnp.maximum(i * r8 - 1, 0), 0)),
                  pl.BlockSpec((SUBLANE, d), lambda i: (jnp.minimum((i + 1) * r8, nb8 - 1), 0)),
                  pl.BlockSpec(w_in.shape, lambda i: (0, 0)),
                  pl.BlockSpec((3, CONV_COLS), lambda i: (0, 0)),
                  pl.BlockSpec((1, CONV_COLS), lambda i: (0, 0))],
        out_specs=[out(w) for w in widths],
        out_shape=[jax.ShapeDtypeStruct((n, w), jnp.float32) for w in widths],
        compiler_params=pltpu.CompilerParams(dimension_semantics=("parallel",), vmem_limit_bytes=VMEM_LIMIT),
        name="in_proj",
    )(h, h, h, w_in.astype(jnp.bfloat16), conv_w, conv_b.reshape(1, -1))
    return tuple(o.reshape(nseq, seq_len, o.shape[-1]) for o in outs)


def token_mixer(hl, hc, lp, need_ctx):
    B, T, _ = hl.shape
    Lc = hc.shape[1]
    x0_l, u_l, rkv_l, lora_l, q_l, k_l, v_l = in_proj(hl, lp['w_in'], lp['conv_w'], lp['conv_b'])
    x0_c, u_c, rkv_c, lora_c, q_c, k_c, v_c = in_proj(hc, lp['w_in'], lp['conv_w'], lp['conv_b'])
    filt_p = (lp['hy_w1'], lp['hy_b1'], lp['hy_w2'], lp['hy_b2'], lp['hy_w3'], lp['hy_b3'], lp['hy_freq'])
    hy_l = x0_l * (hyena_conv(u_l, hyena_filter(T, *filt_p)) + u_l * lp['hy_bias'])
    rw_c, rw_l = rwkv7_mix(rkv_c, rkv_l, lora_c, lora_l, lp)
    at_l = windowed_attention(q_l, k_l, v_l, k_c, v_c, lp['attn_sink'])
    parts_l = (hy_l, rw_l, at_l)
    if not need_ctx:
        return parts_l, None
    hy_c = x0_c * (hyena_conv(u_c, hyena_filter(Lc, *filt_p)) + u_c * lp['hy_bias'])
    at_c = context_attention(q_c.reshape(B, Lc, AT_HKV, AT_G, AT_HD), k_c.reshape(B, Lc, AT_HKV, AT_HD),
                             v_c.reshape(B, Lc, AT_HKV, AT_HD), lp['attn_sink'].reshape(AT_HKV, AT_G))
    return parts_l, (hy_c, rw_c, at_c)


NORM_EPS = 1e-6
MIXOUT_ROWS = 512


def _rms(x, g):
    return (x * lax.rsqrt(jnp.mean(x * x, axis=-1, keepdims=True) + NORM_EPS)) * g


def _mixout_kernel(hy_ref, rw_ref, at_ref, x_ref, g1_ref, sc2_ref, sh2_ref, hyg_ref, atg_ref, n2g_ref, w_ref,
                   xo_ref, h2_ref):
    mix = jnp.concatenate([_rms(hy_ref[...], hyg_ref[...]), rw_ref[...], _rms(at_ref[...], atg_ref[...])], axis=1)
    m = jnp.dot(mix.astype(jnp.bfloat16), w_ref[...], preferred_element_type=jnp.float32)
    xn = x_ref[...] + g1_ref[0] * m
    xo_ref[...] = xn
    h2_ref[...] = (_rms(xn, n2g_ref[...]) * (1.0 + sc2_ref[0]) + sh2_ref[0]).astype(h2_ref.dtype)


def mix_out(parts, x, gate1, scale2, shift2, hy_g, at_g, norm2_g, w_out, rows_per_mod):
    hy, rw, at = parts
    n, d = x.shape
    tm = MIXOUT_ROWS
    assert n % tm == 0 and rows_per_mod % tm == 0
    tpm = rows_per_mod // tm

    def row(w):
        return pl.BlockSpec((tm, w), lambda i: (i, 0))

    def vec(w):
        return pl.BlockSpec((1, w), lambda i: (0, 0))

    mod = pl.BlockSpec((1, 1, d), lambda i: (i // tpm, 0, 0))
    return pl.pallas_call(
        _mixout_kernel,
        grid=(n // tm,),
        in_specs=[row(hy.shape[1]), row(rw.shape[1]), row(at.shape[1]), row(d), mod, mod, mod,
                  vec(hy.shape[1]), vec(at.shape[1]), vec(d), pl.BlockSpec(w_out.shape, lambda i: (0, 0))],
        out_specs=[row(d), row(d)],
        out_shape=[jax.ShapeDtypeStruct((n, d), jnp.float32), jax.ShapeDtypeStruct((n, d), jnp.bfloat16)],
        compiler_params=pltpu.CompilerParams(dimension_semantics=("parallel",), vmem_limit_bytes=VMEM_LIMIT),
        name="mix_out",
    )(hy, rw, at, x, gate1.reshape(-1, 1, d), scale2.reshape(-1, 1, d), shift2.reshape(-1, 1, d),
      hy_g.reshape(1, -1), at_g.reshape(1, -1), norm2_g.reshape(1, -1), w_out.astype(jnp.bfloat16))


def _top16_rows(s, n_rows):
    rio = lax.broadcasted_iota(jnp.int32, s.shape, 0)
    vals, idxs = [], []
    for _ in range(PEER_TOPK):
        m = jnp.max(s, axis=0, keepdims=True)
        idx = jnp.min(jnp.where(s == m, rio, n_rows), axis=0, keepdims=True)
        vals.append(m)
        idxs.append(idx)
        s = jnp.where(rio == idx, -jnp.inf, s)
    return jnp.concatenate(vals, axis=0), jnp.concatenate(idxs, axis=0)


LANE = 128
ROUTE_UNROLL = 4


def _cand_layout():
    rows = [(0, b) for b in range(PEER_TOPK)]
    for a in range(1, 8):
        rows += [(a, b) for b in range(8)]
    rows += [(a, 0) for a in range(8, PEER_TOPK)]
    a = np.array([r[0] for r in rows])
    b = np.array([r[1] for r in rows])
    return a, b, (a + 1) * (b + 1) <= PEER_TOPK


def cand_table():
    a, b, valid = _cand_layout()
    cio = np.where(valid, a * PEER_TOPK + b, PEER_TOPK * PEER_TOPK).astype(np.int32)
    return jnp.asarray(np.broadcast_to(cio[:, None], (cio.shape[0], LANE)))


def _route_kernel(x_ref, wq_ref, k1_ref, k2_ref, cio_ref, i1_ref, i2_ref, g_ref, q_ref, *, tt):
    q = jnp.dot(x_ref[...].astype(jnp.bfloat16), wq_ref[...], preferred_element_type=jnp.float32)
    q_ref[...] = q.astype(jnp.bfloat16)
    nt = (((1,), (1,)), ((), ()))
    cio = cio_ref[...]
    ncand = PEER_TOPK * PEER_TOPK

    def chunk(c, carry):
        t0 = pl.multiple_of(c * LANE, LANE)
        qc = q_ref[pl.ds(t0, LANE), :]
        s1 = lax.dot_general(k1_ref[0], qc[:, :PEER_DQ // 2], nt, preferred_element_type=jnp.float32)
        s2 = lax.dot_general(k2_ref[0], qc[:, PEER_DQ // 2:], nt, preferred_element_type=jnp.float32)
        v1, j1 = _top16_rows(s1, PEER_NKEYS)
        v2, j2 = _top16_rows(s2, PEER_NKEYS)
        j1 = j1 * PEER_NKEYS
        cand = jnp.concatenate([v1[0:1] + v2] + [v1[a:a + 1] + v2[0:8] for a in range(1, 8)]
                               + [v1[8:16] + v2[0:1]], axis=0)
        ecand = jnp.concatenate([j1[0:1] + j2] + [j1[a:a + 1] + j2[0:8] for a in range(1, 8)]
                                + [j1[8:16] + j2[0:1]], axis=0)
        cand = jnp.where(cio < ncand, cand, -jnp.inf)
        scs, es = [], []
        for _ in range(PEER_TOPK):
            m = jnp.max(cand, axis=0, keepdims=True)
            cidx = jnp.min(jnp.where(cand == m, cio, ncand), axis=0, keepdims=True)
            sel = cio == cidx
            es.append(jnp.max(jnp.where(sel, ecand, -1), axis=0, keepdims=True))
            scs.append(m)
            cand = jnp.where(sel, -jnp.inf, cand)
        sc = jnp.concatenate(scs, axis=0)
        e = jnp.concatenate(es, axis=0)
        p = jnp.exp(sc - sc[0:1])
        i1_ref[0, :, pl.ds(t0, LANE)] = e >> 7
        i2_ref[0, :, pl.ds(t0, LANE)] = e & (PEER_NKEYS - 1)
        g_ref[0, :, pl.ds(t0, LANE)] = p / jnp.sum(p, axis=0, keepdims=True)
        return carry

    lax.fori_loop(0, tt // LANE, chunk, 0, unroll=ROUTE_UNROLL)


def peer_route(h, wq_bf, k1_bf, k2_bf, tt):
    n, d = h.shape
    assert n % tt == 0 and tt % (ROUTE_UNROLL * LANE) == 0
    cio = cand_table()
    out_sds = [jax.ShapeDtypeStruct((PEER_HEADS, PEER_TOPK, n), jnp.int32),
               jax.ShapeDtypeStruct((PEER_HEADS, PEER_TOPK, n), jnp.int32),
               jax.ShapeDtypeStruct((PEER_HEADS, PEER_TOPK, n), jnp.float32)]
    ospec = pl.BlockSpec((1, PEER_TOPK, tt), lambda i, hh: (hh, 0, i))
    return pl.pallas_call(
        functools.partial(_route_kernel, tt=tt),
        grid=(n // tt, PEER_HEADS),
        in_specs=[pl.BlockSpec((tt, d), lambda i, hh: (i, 0)),
                  pl.BlockSpec((d, PEER_DQ), lambda i, hh: (0, hh)),
                  pl.BlockSpec((1, PEER_NKEYS, PEER_DQ // 2), lambda i, hh: (hh, 0, 0)),
                  pl.BlockSpec((1, PEER_NKEYS, PEER_DQ // 2), lambda i, hh: (hh, 0, 0)),
                  pl.BlockSpec(cio.shape, lambda i, hh: (0, 0))],
        out_specs=[ospec, ospec, ospec],
        out_shape=out_sds,
        scratch_shapes=[pltpu.VMEM((tt, PEER_DQ), jnp.bfloat16)],
        compiler_params=pltpu.CompilerParams(dimension_semantics=("parallel", "arbitrary"),
                                             vmem_limit_bytes=VMEM_LIMIT),
        name="peer_route",
    )(h, wq_bf, k1_bf, k2_bf, cio)


def _gelu(x):
    return 0.5 * x * (1.0 + lax.erf(x * (2.0 ** -0.5)))


PEER_BUILD_GROUP = 16
PEER_VMEM_LIMIT = 56 * 1024 * 1024


def _expert_kernel(x_ref, i1_ref, i2_ref, g_ref, ut_ref, v_ref, res_ref, g2_ref, ng_ref, nsc_ref, nsh_ref,
                   o_ref, hn_ref, w_ref, rows_ref, xb_ref, *, tt, eb):
    j = pl.program_id(1)
    nk = PEER_NKEYS
    grp = PEER_BUILD_GROUP

    @pl.when(j == 0)
    def _build():
        xb_ref[...] = x_ref[...].astype(jnp.bfloat16)
        rows_ref[0] = i1_ref[...].reshape(nk, tt).astype(jnp.float32).T
        rows_ref[1] = i2_ref[...].reshape(nk, tt).astype(jnp.float32).T
        rows_ref[2] = g_ref[...].reshape(nk, tt).T
        kio = lax.broadcasted_iota(jnp.int32, (nk, nk), 0).astype(jnp.float32)
        nt = (((1,), (1,)), ((), ()))

        def body(gidx, carry):
            t0 = pl.multiple_of(gidx * grp, grp)
            tiles = []
            for s in range(grp):
                i1row = jnp.broadcast_to(rows_ref[0, pl.ds(t0 + s, 1), :], (nk, nk))
                i2row = jnp.broadcast_to(rows_ref[1, pl.ds(t0 + s, 1), :], (nk, nk))
                grow = jnp.broadcast_to(rows_ref[2, pl.ds(t0 + s, 1), :], (nk, nk))
                m1 = jnp.where(i1row == kio, grow, 0.0).astype(jnp.bfloat16)
                p2 = jnp.where(i2row == kio, 1.0, 0.0).astype(jnp.bfloat16)
                tiles.append(lax.dot_general(m1, p2, nt, preferred_element_type=jnp.float32))
            w_ref[:, pl.ds(t0, grp), :] = jnp.swapaxes(jnp.stack(tiles, axis=0), 0, 1).astype(jnp.bfloat16)
            return carry

        lax.fori_loop(0, tt // grp, body, 0)

    a = jnp.dot(xb_ref[...], ut_ref[...], preferred_element_type=jnp.float32)
    nrow = eb // nk
    wj = jnp.concatenate([w_ref[j * nrow + r] for r in range(nrow)], axis=1)
    cmat = (wj.astype(jnp.float32) * _gelu(a)).astype(jnp.bfloat16)
    contrib = jnp.dot(cmat, v_ref[...], preferred_element_type=jnp.float32)

    last = pl.num_programs(1) - 1

    @pl.when(j == 0)
    def _first():
        o_ref[...] = contrib

    @pl.when((j > 0) & (j < last))
    def _middle():
        o_ref[...] += contrib

    @pl.when(j == last)
    def _finish():
        xn = res_ref[...] + g2_ref[0] * (o_ref[...] + contrib)
        o_ref[...] = xn
        hn_ref[...] = _rms(xn, ng_ref[...]) * (1.0 + nsc_ref[0]) + nsh_ref[0]


def peer_experts(h, i1, i2, g, ut_bf, v_bf, res, gate2, next_g, next_scale, next_shift, rows_per_mod, tt, eb):
    n, d = h.shape
    nexp = v_bf.shape[0]
    nblk = nexp // eb
    assert n % tt == 0 and nexp % eb == 0 and nblk >= 2 and nexp == PEER_NKEYS * PEER_NKEYS
    assert rows_per_mod % tt == 0
    tpm = rows_per_mod // tt
    rspec = pl.BlockSpec((PEER_HEADS, PEER_TOPK, tt), lambda i, j: (0, 0, i))
    tile = pl.BlockSpec((tt, d), lambda i, j: (i, 0))
    mod = pl.BlockSpec((1, 1, d), lambda i, j: (i // tpm, 0, 0))
    sds = jax.ShapeDtypeStruct((n, d), jnp.float32)
    return pl.pallas_call(
        functools.partial(_expert_kernel, tt=tt, eb=eb),
        grid=(n // tt, nblk),
        in_specs=[tile, rspec, rspec, rspec,
                  pl.BlockSpec((d, eb), lambda i, j: (0, j)),
                  pl.BlockSpec((eb, d), lambda i, j: (j, 0)),
                  tile, mod, pl.BlockSpec((1, d), lambda i, j: (0, 0)), mod, mod],
        out_specs=[tile, tile],
        out_shape=[sds, sds],
        scratch_shapes=[pltpu.VMEM((PEER_NKEYS, tt, PEER_NKEYS), jnp.bfloat16),
                        pltpu.VMEM((3, tt, PEER_NKEYS), jnp.float32),
                        pltpu.VMEM((tt, d), jnp.bfloat16)],
        compiler_params=pltpu.CompilerParams(dimension_semantics=("parallel", "arbitrary"),
                                             vmem_limit_bytes=PEER_VMEM_LIMIT),
        name="peer_experts",
    )(h, i1, i2, g, ut_bf, v_bf, res, gate2.reshape(-1, 1, d), next_g.reshape(1, d),
      next_scale.reshape(-1, 1, d), next_shift.reshape(-1, 1, d))


PEER_ROUTE_TILE = 512
PEER_TOKEN_TILE = 512
PEER_EXPERT_BLOCK = 1024


def peer_ffn(h, wq, k1, k2, u, v, res, gate2, next_g, next_scale, next_shift, rows_per_mod):
    bf = jnp.bfloat16
    i1, i2, g = peer_route(h, wq.astype(bf), k1.astype(bf), k2.astype(bf), PEER_ROUTE_TILE)
    return peer_experts(h, i1, i2, g, u.T.astype(bf), v.astype(bf), res, gate2, next_g, next_scale, next_shift,
                        rows_per_mod, PEER_TOKEN_TILE, PEER_EXPERT_BLOCK)


def kernel(x, c, ctx, c_ctx, mod_w, mod_b, norm1_g, norm2_g, w_in, conv_w, conv_b, hy_w1, hy_b1, hy_w2, hy_b2, hy_w3, hy_b3, hy_freq, hy_bias, rw_w0, rw_w_up, rw_a0, rw_a_up, rw_g_up, rw_k_k, rw_k_a, rw_r_k, rw_ln_w, rw_ln_b, attn_sink, hy_out_g, at_out_g, w_out, peer_wq, peer_k1, peer_k2, peer_u, peer_v, final_g):
    B, T, D = x.shape
    Lc = ctx.shape[1]
    c_s = jax.nn.silu(c)
    cc_s = jax.nn.silu(c_ctx)
    mods = [jnp.split(c_s @ mod_w[i] + mod_b[i], 6, axis=-1) for i in range(DEPTH)]
    cmods = [jnp.split(cc_s @ mod_w[i] + mod_b[i], 6, axis=-1) for i in range(DEPTH)]
    xl, xc = x.reshape(B * T, D), ctx.reshape(B * Lc, D)
    hl = rmsnorm(x, norm1_g[0]) * (1.0 + mods[0][1][:, None, :]) + mods[0][0][:, None, :]
    hc = rmsnorm(ctx, norm1_g[0]) * (1.0 + cmods[0][1]) + cmods[0][0]
    zeros = jnp.zeros((B, D), jnp.float32)
    for i in range(DEPTH):
        last = i == DEPTH - 1
        lp = dict(w_in=w_in[i], conv_w=conv_w[i], conv_b=conv_b[i],
                  hy_w1=hy_w1[i], hy_b1=hy_b1[i], hy_w2=hy_w2[i], hy_b2=hy_b2[i],
                  hy_w3=hy_w3[i], hy_b3=hy_b3[i], hy_freq=hy_freq[i], hy_bias=hy_bias[i],
                  rw_w0=rw_w0[i], rw_w_up=rw_w_up[i], rw_a0=rw_a0[i], rw_a_up=rw_a_up[i],
                  rw_g_up=rw_g_up[i], rw_k_k=rw_k_k[i], rw_k_a=rw_k_a[i], rw_r_k=rw_r_k[i],
                  rw_ln_w=rw_ln_w[i], rw_ln_b=rw_ln_b[i], attn_sink=attn_sink[i],
                  hy_out_g=hy_out_g[i], at_out_g=at_out_g[i], w_out=w_out[i])
        _, _, g1, sh2, sc2, g2 = mods[i]
        _, _, cg1, csh2, csc2, cg2 = cmods[i]
        parts_l, parts_c = token_mixer(hl, hc, lp, not last)
        peer_w = (peer_wq[i], peer_k1[i], peer_k2[i], peer_u[i], peer_v[i])
        out_w = (lp['hy_out_g'], lp['at_out_g'], norm2_g[i], lp['w_out'])
        flat = lambda parts: tuple(p.reshape(-1, p.shape[-1]) for p in parts)
        if last:
            next_l = (final_g, zeros, zeros)
        else:
            next_l = (norm1_g[i + 1], mods[i + 1][1], mods[i + 1][0])
        xl2, h2l = mix_out(flat(parts_l), xl, g1, sc2, sh2, *out_w, rows_per_mod=T)
        xl, hl = peer_ffn(h2l, *peer_w, xl2, g2, *next_l, rows_per_mod=T)
        hl = hl.reshape(B, T, D)
        if not last:
            xc2, h2c = mix_out(flat(parts_c), xc, cg1[None], csc2[None], csh2[None], *out_w, rows_per_mod=B * Lc)
            xc, hc = peer_ffn(h2c, *peer_w, xc2, cg2[None], norm1_g[i + 1], cmods[i + 1][1][None],
                              cmods[i + 1][0][None], rows_per_mod=B * Lc)
            hc = hc.reshape(B, Lc, D)
    return hl
```

```python
import functools
import math

import jax
import jax.numpy as jnp
import numpy as np
from jax import lax
from jax.experimental import pallas as pl
from jax.experimental.pallas import tpu as pltpu

D_MODEL = 1024
DEPTH = 4
GRID_W = 64

HY_CH = 256
RW_HEADS = 4
RW_HD = 64
RW_W = RW_HEADS * RW_HD
AT_HQ = 8
AT_HKV = 2
AT_G = AT_HQ // AT_HKV
AT_HD = 64
AT_W = AT_HQ * AT_HD
AT_KVW = AT_HKV * AT_HD

RW_DECAY_R = 64
RW_ICLR_R = 64
RW_GATE_R = 128
RW_LORA_W = 2 * RW_DECAY_R + 2 * RW_ICLR_R + RW_GATE_R
RW_GN_EPS = RW_HD * 1e-5

HY_EMB = 33
HY_BANDS = (HY_EMB - 1) // 2
HY_SHIFT = 0.05
HY_MIN_DECAY = math.log(1e-2) / 1.5
HY_MAX_DECAY = math.log(1e-2) / 0.3

ATT_WINDOW = 128
ATT_BLOCK = 128
ROPE_BASE = 10000.0
NEG_INF = -1e30

PEER_HEADS = 8
PEER_NKEYS = 128
PEER_DQ = 256
PEER_TOPK = 16
PEER_CHUNK = 128

OFF_HY = 0
OFF_RK = OFF_HY + 3 * HY_CH
OFF_LORA = OFF_RK + 3 * RW_W
OFF_AT = OFF_LORA + RW_LORA_W
CONV_COLS = OFF_LORA

VMEM_LIMIT = 48 * 1024 * 1024


def rmsnorm(x, g, eps=1e-6):
    y = x * lax.rsqrt(jnp.mean(x * x, axis=-1, keepdims=True) + eps)
    return y * g


def hyena_filter(L, w1, b1, w2, b2, w3, b3, freq):
    f32 = jnp.float32
    j = jnp.arange(L, dtype=f32)
    t = j / max(L - 1, 1)
    bands = jnp.linspace(1e-4, HY_BANDS - 1, HY_BANDS, dtype=f32)
    ang = 2.0 * math.pi * j[:, None] * bands[None, :] / L
    z = jnp.concatenate([t[:, None], jnp.cos(ang), -jnp.sin(ang)], axis=-1)
    h = jnp.sin(freq[0] * (z @ w1 + b1))
    h = jnp.sin(freq[1] * (h @ w2 + b2))
    h = (h @ w3 + b3).astype(f32).reshape(L, 2, HY_CH)
    deltas = jnp.abs(jnp.linspace(HY_MIN_DECAY, HY_MAX_DECAY, HY_CH, dtype=f32))
    window = jnp.exp(-t[:, None] * deltas[None, :]) + HY_SHIFT
    h = h * window[:, None, :]
    k = jnp.concatenate([h[:, 0], jnp.zeros((1, HY_CH), f32), h[:0:-1, 1]], axis=0)
    return k * lax.rsqrt(jnp.sum(k * k, axis=0, keepdims=True) + 1e-6)


HY_BLOCK = 256


def _hyena_kernel(fr_ref, u_ref, y_ref, *, nblk, nb):
    P = HY_BLOCK
    ncol = nblk * nb
    two_l = 2 * nblk * P
    width = two_l + 2 * P
    rolled = pltpu.roll(jnp.broadcast_to(fr_ref[0], (P, width)), 1, 1, stride=1, stride_axis=0)
    u = u_ref[0]
    col = lax.broadcasted_iota(jnp.int32, (P, ncol), 1)
    lhs, rhs = [], []
    for m in range(-(nblk - 1), nblk):
        start = (-P * m) % two_l
        lhs.append(rolled[:, start + P:start + 2 * P].astype(jnp.bfloat16))
        if m == 0:
            um = u
        elif m > 0:
            um = jnp.where(col >= nb * m, pltpu.roll(u, nb * m, 1), 0.0)
        else:
            um = jnp.where(col < ncol + nb * m, pltpu.roll(u, ncol + nb * m, 1), 0.0)
        rhs.append(um.astype(jnp.bfloat16))
    y_ref[0] = jnp.dot(jnp.concatenate(lhs, axis=1), jnp.concatenate(rhs, axis=0), preferred_element_type=jnp.float32)


def hyena_conv(u, filt):
    B, L, C = u.shape
    P = HY_BLOCK
    assert L % P == 0 and filt.shape == (2 * L, C)
    nblk = L // P
    idx = (P - 1 - np.arange(2 * L + 2 * P)) % (2 * L)
    fr = filt[idx].T.reshape(C, 1, 2 * L + 2 * P)
    uc = u.reshape(B, nblk, P, C).transpose(3, 2, 1, 0).reshape(C, P, nblk * B)
    y = pl.pallas_call(
        functools.partial(_hyena_kernel, nblk=nblk, nb=B),
        grid=(C,),
        in_specs=[pl.BlockSpec((1, 1, 2 * L + 2 * P), lambda c: (c, 0, 0)),
                  pl.BlockSpec((1, P, nblk * B), lambda c: (c, 0, 0))],
        out_specs=pl.BlockSpec((1, P, nblk * B), lambda c: (c, 0, 0)),
        out_shape=jax.ShapeDtypeStruct((C, P, nblk * B), jnp.float32),
        compiler_params=pltpu.CompilerParams(dimension_semantics=("parallel",), vmem_limit_bytes=VMEM_LIMIT),
        name="hyena_conv",
    )(fr, uc)
    return y.reshape(C, P, nblk, B).transpose(3, 2, 1, 0).reshape(B, L, C)


RW_LANES = 128
RW_TCHUNK = 32


RW_SUBLANE = 8


def _rwkv_scan_kernel(rf, wf, kf, vf, af, bf_, rb, wb, kb, vb, ab, bb, yf_ref, yb_ref, s_ref, rows_ref, *, tc):
    @pl.when(pl.program_id(0) == 0)
    def _init():
        s_ref[...] = jnp.zeros_like(s_ref)

    fwd = lax.broadcasted_iota(jnp.int32, (RW_HD, RW_LANES), 1) < RW_LANES // 2
    vregs = [slice(g * RW_SUBLANE, (g + 1) * RW_SUBLANE) for g in range(RW_HD // RW_SUBLANE)]

    def step(t, carry):
        tb = tc - 1 - t
        rows_ref[0] = jnp.where(fwd, af[t], ab[tb])
        rows_ref[1] = jnp.where(fwd, wf[t], wb[tb])
        rows_ref[2] = jnp.where(fwd, bf_[t], bb[tb])
        rows_ref[3] = jnp.where(fwd, kf[t], kb[tb])
        rows_ref[4] = jnp.where(fwd, rf[t], rb[tb])
        v_t = jnp.where(fwd, vf[t], vb[tb])
        sa = [jnp.zeros((RW_SUBLANE, RW_LANES), jnp.float32) for _ in vregs]
        for k in range(RW_HD):
            a_b = rows_ref[0, pl.ds(k, 1), :]
            for g, sl in enumerate(vregs):
                sa[g] = sa[g] + s_ref[k, sl, :] * a_b
        y = [jnp.zeros((RW_SUBLANE, RW_LANES), jnp.float32) for _ in vregs]
        for k in range(RW_HD):
            w_b = rows_ref[1, pl.ds(k, 1), :]
            b_b = rows_ref[2, pl.ds(k, 1), :]
            k_b = rows_ref[3, pl.ds(k, 1), :]
            r_b = rows_ref[4, pl.ds(k, 1), :]
            for g, sl in enumerate(vregs):
                s_new = s_ref[k, sl, :] * w_b + sa[g] * b_b + v_t[sl] * k_b
                s_ref[k, sl, :] = s_new
                y[g] = y[g] + s_new * r_b
        for g, sl in enumerate(vregs):
            yf_ref[t, sl, :] = y[g]
            yb_ref[tb, sl, :] = y[g]
        return carry

    lax.fori_loop(0, tc, step, 0)


def rwkv_scan_lanes(r, w, k, v, a, b, n_ctx):
    L = r.shape[0]
    tc = RW_TCHUNK
    assert L % tc == 0 and n_ctx % tc == 0 and r.shape[1:] == (RW_HD, RW_LANES)
    nblk, cblk = L // tc, n_ctx // tc

    def fmap(i):
        return (i, 0, 0)

    def bmap(i):
        return (jnp.where(i < cblk, cblk - 1 - i, nblk - 1 + cblk - i), 0, 0)

    fspec = pl.BlockSpec((tc, RW_HD, RW_LANES), fmap)
    bspec = pl.BlockSpec((tc, RW_HD, RW_LANES), bmap)
    sds = jax.ShapeDtypeStruct((L, RW_HD, RW_LANES), jnp.float32)
    return pl.pallas_call(
        functools.partial(_rwkv_scan_kernel, tc=tc),
        grid=(nblk,),
        in_specs=[fspec] * 6 + [bspec] * 6,
        out_specs=[fspec, bspec],
        out_shape=[sds, sds],
        scratch_shapes=[pltpu.VMEM((RW_HD, RW_HD, RW_LANES), jnp.float32),
                        pltpu.VMEM((5, RW_HD, RW_LANES), jnp.float32)],
        compiler_params=pltpu.CompilerParams(dimension_semantics=("arbitrary",),
                                             vmem_limit_bytes=VMEM_LIMIT),
        name="rwkv_scan",
    )(r, w, k, v, a, b, r, w, k, v, a, b)


def rwkv7_mix(rkv_c, rkv_l, lora_c, lora_l, lp):
    f32 = jnp.float32
    B, Lc, _ = rkv_c.shape
    T = rkv_l.shape[1]
    Lt = Lc + T
    hs = (B, Lt, RW_HEADS, RW_HD)
    rkv = jnp.concatenate([rkv_c, rkv_l], axis=1)
    lora = jnp.concatenate([lora_c, lora_l], axis=1)
    r = rkv[..., :RW_W].reshape(hs)
    k = rkv[..., RW_W:2 * RW_W].reshape(hs)
    v = rkv[..., 2 * RW_W:].reshape(hs)
    kk = k * lp['rw_k_k'].reshape(RW_HEADS, RW_HD)
    kk = kk * lax.rsqrt(jnp.sum(kk * kk, axis=-1, keepdims=True) + 1e-12)
    k_a = lp['rw_k_a'].reshape(RW_HEADS, RW_HD)
    r_k = lp['rw_r_k']
    assert 2 * B * RW_HEADS == RW_LANES
    bonus = jnp.zeros(hs, f32)
    per_dir = []
    for d in range(2):
        wd = lora[..., d * RW_DECAY_R:(d + 1) * RW_DECAY_R]
        ad = lora[..., 2 * RW_DECAY_R + d * RW_ICLR_R:2 * RW_DECAY_R + (d + 1) * RW_ICLR_R]
        w_log = -jax.nn.softplus(-(lp['rw_w0'][d] + jnp.tanh(wd) @ lp['rw_w_up'][d])) - 0.5
        decay = jnp.exp(-jnp.exp(w_log)).reshape(hs)
        a = jax.nn.sigmoid(lp['rw_a0'][d] + ad @ lp['rw_a_up'][d]).reshape(hs)
        k_d = k * (1.0 + (a - 1.0) * k_a)
        per_dir.append((r, decay, k_d, v, -kk, kk * a))
        bonus = bonus + jnp.sum(r * k_d * r_k, axis=-1, keepdims=True) * v

    def lanes(xf, xb):
        return jnp.concatenate([x.transpose(1, 3, 0, 2).reshape(Lt, RW_HD, B * RW_HEADS) for x in (xf, xb)], axis=-1)

    yf, yb = rwkv_scan_lanes(*(lanes(xf, xb) for xf, xb in zip(*per_dir)), n_ctx=Lc)
    half = RW_LANES // 2
    y_sum = (yf[..., :half] + yb[..., half:]).reshape(Lt, RW_HD, B, RW_HEADS).transpose(2, 0, 3, 1)
    mu = jnp.mean(y_sum, axis=-1, keepdims=True)
    var = jnp.mean(jnp.square(y_sum - mu), axis=-1, keepdims=True)
    yn = (y_sum - mu) * lax.rsqrt(var + RW_GN_EPS)
    yn = yn * lp['rw_ln_w'].reshape(RW_HEADS, RW_HD) + lp['rw_ln_b'].reshape(RW_HEADS, RW_HD)
    g = jax.nn.sigmoid(lora[..., 2 * RW_DECAY_R + 2 * RW_ICLR_R:]) @ lp['rw_g_up']
    out = (yn + bonus).reshape(B, Lt, RW_W) * g
    return out[:, :Lc], out[:, Lc:]


def rope_tables(T):
    f32 = jnp.float32
    quarter = AT_HD // 4
    inv = ROPE_BASE ** (-jnp.arange(quarter, dtype=f32) / quarter)
    rows = T // GRID_W
    row = jnp.repeat(jnp.arange(rows), GRID_W).astype(f32)
    col = jnp.tile(jnp.arange(GRID_W), rows).astype(f32)
    out_c, out_s = [], []
    for pos in (row, col):
        ang = pos[:, None] * inv[None, :]
        c, s = jnp.cos(ang), jnp.sin(ang)
        out_c += [c, c]
        out_s += [-s, s]
    return jnp.concatenate(out_c, axis=1), jnp.concatenate(out_s, axis=1)


def _rope(x, c, s):
    lane = lax.broadcasted_iota(jnp.int32, x.shape, 1)
    first = (lane % 32) < 16
    partner = jnp.where(first, pltpu.roll(x, x.shape[1] - 16, axis=1), pltpu.roll(x, 16, axis=1))
    return x * c + partner * s


def _attn_kernel(q_ref, kp_ref, kc_ref, kn_ref, vp_ref, vc_ref, vn_ref, kx_ref, vx_ref, sink_ref,
                 cq_ref, sq_ref, ckp_ref, skp_ref, ckc_ref, skc_ref, ckn_ref, skn_ref, o_ref, *, nblk):
    i = pl.program_id(1)
    blk = ATT_BLOCK
    scale = AT_HD ** -0.5
    q = _rope(q_ref[0], cq_ref[...], sq_ref[...])
    kspan = jnp.concatenate([_rope(kp_ref[0], ckp_ref[...], skp_ref[...]),
                             _rope(kc_ref[0], ckc_ref[...], skc_ref[...]),
                             _rope(kn_ref[0], ckn_ref[...], skn_ref[...])], axis=0)
    vspan = jnp.concatenate([vp_ref[0], vc_ref[0], vn_ref[0]], axis=0)
    kx = kx_ref[0]
    vx = vx_ref[0]
    qpos = lax.broadcasted_iota(jnp.int32, (blk, 3 * blk), 0) + blk
    kpos = lax.broadcasted_iota(jnp.int32, (blk, 3 * blk), 1)
    valid = jnp.abs(qpos - kpos) <= ATT_WINDOW
    valid = valid & ((kpos >= blk) | (i > 0)) & ((kpos < 2 * blk) | (i < nblk - 1))
    valid8 = jnp.concatenate([valid] * AT_HQ, axis=0)
    nt = (((1,), (1,)), ((), ()))
    bf = jnp.bfloat16
    s_lat, s_ctx, sks = [], [], []
    for g in range(AT_HKV):
        kg = kspan[:, g * AT_HD:(g + 1) * AT_HD].astype(bf)
        kxg = kx[:, g * AT_HD:(g + 1) * AT_HD].astype(bf)
        qg = jnp.concatenate([q[:, (g * AT_G + j) * AT_HD:(g * AT_G + j + 1) * AT_HD] for j in range(AT_G)],
                             axis=0).astype(bf)
        s_lat.append(lax.dot_general(qg, kg, nt, preferred_element_type=jnp.float32))
        s_ctx.append(lax.dot_general(qg, kxg, nt, preferred_element_type=jnp.float32))
        sks += [jnp.full((blk, 1), sink_ref[g * AT_G + j], jnp.float32) for j in range(AT_G)]
    s_lat = jnp.where(valid8, jnp.concatenate(s_lat, axis=0) * scale, NEG_INF)
    s_ctx = jnp.concatenate(s_ctx, axis=0) * scale
    sk = jnp.concatenate(sks, axis=0)
    m = jnp.maximum(jnp.maximum(jnp.max(s_lat, axis=1, keepdims=True), jnp.max(s_ctx, axis=1, keepdims=True)), sk)
    p_lat = jnp.exp(s_lat - m)
    p_ctx = jnp.exp(s_ctx - m)
    den = jnp.sum(p_lat, axis=1, keepdims=True) + jnp.sum(p_ctx, axis=1, keepdims=True) + jnp.exp(sk - m)
    inv = 1.0 / den
    p_lat = (p_lat * inv).astype(bf)
    p_ctx = (p_ctx * inv).astype(bf)
    outs = []
    rows = AT_G * blk
    for g in range(AT_HKV):
        vg = vspan[:, g * AT_HD:(g + 1) * AT_HD].astype(bf)
        vxg = vx[:, g * AT_HD:(g + 1) * AT_HD].astype(bf)
        o = (jnp.dot(p_lat[g * rows:(g + 1) * rows], vg, preferred_element_type=jnp.float32)
             + jnp.dot(p_ctx[g * rows:(g + 1) * rows], vxg, preferred_element_type=jnp.float32))
        outs += [o[j * blk:(j + 1) * blk] for j in range(AT_G)]
    o_ref[0] = jnp.concatenate(outs, axis=1)


def windowed_attention(q, k, v, kx, vx, sink):
    B, T, _ = q.shape
    Lc = kx.shape[1]
    assert T % ATT_BLOCK == 0 and ATT_WINDOW == ATT_BLOCK
    nblk = T // ATT_BLOCK
    c1, s1 = rope_tables(T)
    cq, sq = jnp.tile(c1, (1, AT_HQ)), jnp.tile(s1, (1, AT_HQ))
    ck, sk = jnp.tile(c1, (1, AT_HKV)), jnp.tile(s1, (1, AT_HKV))

    def prev(b, i):
        return (b, jnp.maximum(i - 1, 0), 0)

    def cur(b, i):
        return (b, i, 0)

    def nxt(b, i):
        return (b, jnp.minimum(i + 1, nblk - 1), 0)

    def kb(f):
        return pl.BlockSpec((1, ATT_BLOCK, AT_KVW), f)

    def tb(f):
        return pl.BlockSpec((ATT_BLOCK, AT_KVW), lambda b, i: f(b, i)[1:])

    return pl.pallas_call(
        functools.partial(_attn_kernel, nblk=nblk),
        grid=(B, nblk),
        in_specs=[pl.BlockSpec((1, ATT_BLOCK, AT_W), cur), kb(prev), kb(cur), kb(nxt), kb(prev), kb(cur), kb(nxt),
                  pl.BlockSpec((1, Lc, AT_KVW), lambda b, i: (b, 0, 0)),
                  pl.BlockSpec((1, Lc, AT_KVW), lambda b, i: (b, 0, 0)),
                  pl.BlockSpec(memory_space=pltpu.SMEM),
                  pl.BlockSpec((ATT_BLOCK, AT_W), lambda b, i: (i, 0)),
                  pl.BlockSpec((ATT_BLOCK, AT_W), lambda b, i: (i, 0)),
                  tb(prev), tb(prev), tb(cur), tb(cur), tb(nxt), tb(nxt)],
        out_specs=pl.BlockSpec((1, ATT_BLOCK, AT_W), cur),
        out_shape=jax.ShapeDtypeStruct((B, T, AT_W), jnp.float32),
        compiler_params=pltpu.CompilerParams(dimension_semantics=("parallel", "arbitrary"),
                                             vmem_limit_bytes=VMEM_LIMIT),
        name="window_attention",
    )(q, k, k, k, v, v, v, kx, vx, sink, cq, sq, ck, sk, ck, sk, ck, sk)


def context_attention(qc, kc, vc, sink):
    B, Lc = qc.shape[:2]
    scale = AT_HD ** -0.5
    s = jnp.einsum('bqgjd,bkgd->bgjqk', qc, kc) * scale
    sink_b = jnp.broadcast_to(sink[None, :, :, None, None], (B, AT_HKV, AT_G, Lc, 1))
    p = jax.nn.softmax(jnp.concatenate([s, sink_b], axis=-1), axis=-1)[..., :Lc]
    return jnp.einsum('bgjqk,bkgd->bqgjd', p, vc).reshape(B, Lc, AT_W)


INPROJ_ROWS = 256
SUBLANE = 8


def _inproj_kernel(h_ref, hp_ref, hn_ref, w_ref, cw_ref, cb_ref, x0_ref, u_ref, rkv_ref, lora_ref, q_ref, k_ref, v_ref,
                   *, tiles_per_seq):
    i = pl.program_id(0)
    tm = h_ref.shape[0]
    bf = jnp.bfloat16
    z = jnp.dot(h_ref[...].astype(bf), w_ref[...], preferred_element_type=jnp.float32)
    wc = w_ref[:, :CONV_COLS]
    zp = jnp.dot(hp_ref[...].astype(bf), wc, preferred_element_type=jnp.float32)[SUBLANE - 1:SUBLANE]
    zn = jnp.dot(hn_ref[...].astype(bf), wc, preferred_element_type=jnp.float32)[0:1]
    pos = i % tiles_per_seq
    zp = jnp.where(pos == 0, 0.0, zp)
    zn = jnp.where(pos == tiles_per_seq - 1, 0.0, zn)
    zc = z[:, :CONV_COLS]
    row = lax.broadcasted_iota(jnp.int32, zc.shape, 0)
    up = jnp.where(row == 0, zp, pltpu.roll(zc, 1, 0))
    dn = jnp.where(row == tm - 1, zn, pltpu.roll(zc, tm - 1, 0))
    s = up * cw_ref[0:1] + zc * cw_ref[1:2] + dn * cw_ref[2:3] + cb_ref[...]
    x0_ref[...] = s[:, :HY_CH]
    u_ref[...] = s[:, HY_CH:2 * HY_CH] * s[:, 2 * HY_CH:OFF_RK]
    rkv_ref[...] = s[:, OFF_RK:]
    lora_ref[...] = z[:, OFF_LORA:OFF_AT]
    q_ref[...] = z[:, OFF_AT:OFF_AT + AT_W]
    k_ref[...] = z[:, OFF_AT + AT_W:OFF_AT + AT_W + AT_KVW]
    v_ref[...] = z[:, OFF_AT + AT_W + AT_KVW:]


def in_proj(h3, w_in, conv_w, conv_b):
    nseq, seq_len, d = h3.shape
    n = nseq * seq_len
    tm = INPROJ_ROWS
    assert seq_len % tm == 0 and w_in.shape == (d, OFF_AT + AT_W + 2 * AT_KVW)
    tps = seq_len // tm
    nb8 = n // SUBLANE
    r8 = tm // SUBLANE
    h = h3.reshape(n, d)

    def out(w):
        return pl.BlockSpec((tm, w), lambda i: (i, 0))

    widths = (HY_CH, HY_CH, 3 * RW_W, RW_LORA_W, AT_W, AT_KVW, AT_KVW)
    outs = pl.pallas_call(
        functools.partial(_inproj_kernel, tiles_per_seq=tps),
        grid=(n // tm,),
        in_specs=[pl.BlockSpec((tm, d), lambda i: (i, 0)),
                  pl.BlockSpec((SUBLANE, d), lambda i: (jnp.maximum(i * r8 - 1, 0), 0)),
                  pl.BlockSpec((SUBLANE, d), lambda i: (jnp.minimum((i + 1) * r8, nb8 - 1), 0)),
                  pl.BlockSpec(w_in.shape, lambda i: (0, 0)),
                  pl.BlockSpec((3, CONV_COLS), lambda i: (0, 0)),
                  pl.BlockSpec((1, CONV_COLS), lambda i: (0, 0))],
        out_specs=[out(w) for w in widths],
        out_shape=[jax.ShapeDtypeStruct((n, w), jnp.float32) for w in widths],
        compiler_params=pltpu.CompilerParams(dimension_semantics=("parallel",), vmem_limit_bytes=VMEM_LIMIT),
        name="in_proj",
    )(h, h, h, w_in.astype(jnp.bfloat16), conv_w, conv_b.reshape(1, -1))
    return tuple(o.reshape(nseq, seq_len, o.shape[-1]) for o in outs)


def token_mixer(hl, hc, lp, need_ctx):
    B, T, _ = hl.shape
    Lc = hc.shape[1]
    x0_l, u_l, rkv_l, lora_l, q_l, k_l, v_l = in_proj(hl, lp['w_in'], lp['conv_w'], lp['conv_b'])
    x0_c, u_c, rkv_c, lora_c, q_c, k_c, v_c = in_proj(hc, lp['w_in'], lp['conv_w'], lp['conv_b'])
    filt_p = (lp['hy_w1'], lp['hy_b1'], lp['hy_w2'], lp['hy_b2'], lp['hy_w3'], lp['hy_b3'], lp['hy_freq'])
    hy_l = x0_l * (hyena_conv(u_l, hyena_filter(T, *filt_p)) + u_l * lp['hy_bias'])
    rw_c, rw_l = rwkv7_mix(rkv_c, rkv_l, lora_c, lora_l, lp)
    at_l = windowed_attention(q_l, k_l, v_l, k_c, v_c, lp['attn_sink'])
    parts_l = (hy_l, rw_l, at_l)
    if not need_ctx:
        return parts_l, None
    hy_c = x0_c * (hyena_conv(u_c, hyena_filter(Lc, *filt_p)) + u_c * lp['hy_bias'])
    at_c = context_attention(q_c.reshape(B, Lc, AT_HKV, AT_G, AT_HD), k_c.reshape(B, Lc, AT_HKV, AT_HD),
                             v_c.reshape(B, Lc, AT_HKV, AT_HD), lp['attn_sink'].reshape(AT_HKV, AT_G))
    return parts_l, (hy_c, rw_c, at_c)


NORM_EPS = 1e-6
MIXOUT_ROWS = 512


def _rms(x, g):
    return (x * lax.rsqrt(jnp.mean(x * x, axis=-1, keepdims=True) + NORM_EPS)) * g


def _mixout_kernel(hy_ref, rw_ref, at_ref, x_ref, g1_ref, sc2_ref, sh2_ref, hyg_ref, atg_ref, n2g_ref, w_ref,
                   xo_ref, h2_ref):
    mix = jnp.concatenate([_rms(hy_ref[...], hyg_ref[...]), rw_ref[...], _rms(at_ref[...], atg_ref[...])], axis=1)
    m = jnp.dot(mix.astype(jnp.bfloat16), w_ref[...], preferred_element_type=jnp.float32)
    xn = x_ref[...] + g1_ref[0] * m
    xo_ref[...] = xn
    h2_ref[...] = (_rms(xn, n2g_ref[...]) * (1.0 + sc2_ref[0]) + sh2_ref[0]).astype(h2_ref.dtype)


def mix_out(parts, x, gate1, scale2, shift2, hy_g, at_g, norm2_g, w_out, rows_per_mod):
    hy, rw, at = parts
    n, d = x.shape
    tm = MIXOUT_ROWS
    assert n % tm == 0 and rows_per_mod % tm == 0
    tpm = rows_per_mod // tm

    def row(w):
        return pl.BlockSpec((tm, w), lambda i: (i, 0))

    def vec(w):
        return pl.BlockSpec((1, w), lambda i: (0, 0))

    mod = pl.BlockSpec((1, 1, d), lambda i: (i // tpm, 0, 0))
    return pl.pallas_call(
        _mixout_kernel,
        grid=(n // tm,),
        in_specs=[row(hy.shape[1]), row(rw.shape[1]), row(at.shape[1]), row(d), mod, mod, mod,
                  vec(hy.shape[1]), vec(at.shape[1]), vec(d), pl.BlockSpec(w_out.shape, lambda i: (0, 0))],
        out_specs=[row(d), row(d)],
        out_shape=[jax.ShapeDtypeStruct((n, d), jnp.float32), jax.ShapeDtypeStruct((n, d), jnp.bfloat16)],
        compiler_params=pltpu.CompilerParams(dimension_semantics=("parallel",), vmem_limit_bytes=VMEM_LIMIT),
        name="mix_out",
    )(hy, rw, at, x, gate1.reshape(-1, 1, d), scale2.reshape(-1, 1, d), shift2.reshape(-1, 1, d),
      hy_g.reshape(1, -1), at_g.reshape(1, -1), norm2_g.reshape(1, -1), w_out.astype(jnp.bfloat16))


def _top16_rows(s, n_rows):
    rio = lax.broadcasted_iota(jnp.int32, s.shape, 0)
    vals, idxs = [], []
    for _ in range(PEER_TOPK):
        m = jnp.max(s, axis=0, keepdims=True)
        idx = jnp.min(jnp.where(s == m, rio, n_rows), axis=0, keepdims=True)
        vals.append(m)
        idxs.append(idx)
        s = jnp.where(rio == idx, -jnp.inf, s)
    return jnp.concatenate(vals, axis=0), jnp.concatenate(idxs, axis=0)


LANE = 128
ROUTE_UNROLL = 4


def _cand_layout():
    rows = [(0, b) for b in range(PEER_TOPK)]
    for a in range(1, 8):
        rows += [(a, b) for b in range(8)]
    rows += [(a, 0) for a in range(8, PEER_TOPK)]
    a = np.array([r[0] for r in rows])
    b = np.array([r[1] for r in rows])
    return a, b, (a + 1) * (b + 1) <= PEER_TOPK


def cand_table():
    a, b, valid = _cand_layout()
    cio = np.where(valid, a * PEER_TOPK + b, PEER_TOPK * PEER_TOPK).astype(np.int32)
    return jnp.asarray(np.broadcast_to(cio[:, None], (cio.shape[0], LANE)))


def _route_kernel(x_ref, wq_ref, k1_ref, k2_ref, cio_ref, i1_ref, i2_ref, g_ref, q_ref, *, tt):
    q = jnp.dot(x_ref[...].astype(jnp.bfloat16), wq_ref[...], preferred_element_type=jnp.float32)
    q_ref[...] = q.astype(jnp.bfloat16)
    nt = (((1,), (1,)), ((), ()))
    cio = cio_ref[...]
    ncand = PEER_TOPK * PEER_TOPK

    def chunk(c, carry):
        t0 = pl.multiple_of(c * LANE, LANE)
        qc = q_ref[pl.ds(t0, LANE), :]
        s1 = lax.dot_general(k1_ref[0], qc[:, :PEER_DQ // 2], nt, preferred_element_type=jnp.float32)
        s2 = lax.dot_general(k2_ref[0], qc[:, PEER_DQ // 2:], nt, preferred_element_type=jnp.float32)
        v1, j1 = _top16_rows(s1, PEER_NKEYS)
        v2, j2 = _top16_rows(s2, PEER_NKEYS)
        j1 = j1 * PEER_NKEYS
        cand = jnp.concatenate([v1[0:1] + v2] + [v1[a:a + 1] + v2[0:8] for a in range(1, 8)]
                               + [v1[8:16] + v2[0:1]], axis=0)
        ecand = jnp.concatenate([j1[0:1] + j2] + [j1[a:a + 1] + j2[0:8] for a in range(1, 8)]
                                + [j1[8:16] + j2[0:1]], axis=0)
        cand = jnp.where(cio < ncand, cand, -jnp.inf)
        scs, es = [], []
        for _ in range(PEER_TOPK):
            m = jnp.max(cand, axis=0, keepdims=True)
            cidx = jnp.min(jnp.where(cand == m, cio, ncand), axis=0, keepdims=True)
            sel = cio == cidx
            es.append(jnp.max(jnp.where(sel, ecand, -1), axis=0, keepdims=True))
            scs.append(m)
            cand = jnp.where(sel, -jnp.inf, cand)
        sc = jnp.concatenate(scs, axis=0)
        e = jnp.concatenate(es, axis=0)
        p = jnp.exp(sc - sc[0:1])
        i1_ref[0, :, pl.ds(t0, LANE)] = e >> 7
        i2_ref[0, :, pl.ds(t0, LANE)] = e & (PEER_NKEYS - 1)
        g_ref[0, :, pl.ds(t0, LANE)] = p / jnp.sum(p, axis=0, keepdims=True)
        return carry

    lax.fori_loop(0, tt // LANE, chunk, 0, unroll=ROUTE_UNROLL)


def peer_route(h, wq_bf, k1_bf, k2_bf, tt):
    n, d = h.shape
    assert n % tt == 0 and tt % (ROUTE_UNROLL * LANE) == 0
    cio = cand_table()
    out_sds = [jax.ShapeDtypeStruct((PEER_HEADS, PEER_TOPK, n), jnp.int32),
               jax.ShapeDtypeStruct((PEER_HEADS, PEER_TOPK, n), jnp.int32),
               jax.ShapeDtypeStruct((PEER_HEADS, PEER_TOPK, n), jnp.float32)]
    ospec = pl.BlockSpec((1, PEER_TOPK, tt), lambda i, hh: (hh, 0, i))
    return pl.pallas_call(
        functools.partial(_route_kernel, tt=tt),
        grid=(n // tt, PEER_HEADS),
        in_specs=[pl.BlockSpec((tt, d), lambda i, hh: (i, 0)),
                  pl.BlockSpec((d, PEER_DQ), lambda i, hh: (0, hh)),
                  pl.BlockSpec((1, PEER_NKEYS, PEER_DQ // 2), lambda i, hh: (hh, 0, 0)),
                  pl.BlockSpec((1, PEER_NKEYS, PEER_DQ // 2), lambda i, hh: (hh, 0, 0)),
                  pl.BlockSpec(cio.shape, lambda i, hh: (0, 0))],
        out_specs=[ospec, ospec, ospec],
        out_shape=out_sds,
        scratch_shapes=[pltpu.VMEM((tt, PEER_DQ), jnp.bfloat16)],
        compiler_params=pltpu.CompilerParams(dimension_semantics=("parallel", "arbitrary"),
                                             vmem_limit_bytes=VMEM_LIMIT),
        name="peer_route",
    )(h, wq_bf, k1_bf, k2_bf, cio)


def _gelu(x):
    return 0.5 * x * (1.0 + lax.erf(x * (2.0 ** -0.5)))


PEER_BUILD_GROUP = 16
PEER_VMEM_LIMIT = 56 * 1024 * 1024


def _expert_kernel(x_ref, i1_ref, i2_ref, g_ref, ut_ref, v_ref, res_ref, g2_ref, ng_ref, nsc_ref, nsh_ref,
                   o_ref, hn_ref, w_ref, rows_ref, xb_ref, *, tt, eb):
    j = pl.program_id(1)
    nk = PEER_NKEYS
    grp = PEER_BUILD_GROUP

    @pl.when(j == 0)
    def _build():
        xb_ref[...] = x_ref[...].astype(jnp.bfloat16)
        rows_ref[0] = i1_ref[...].reshape(nk, tt).astype(jnp.float32).T
        rows_ref[1] = i2_ref[...].reshape(nk, tt).astype(jnp.float32).T
        rows_ref[2] = g_ref[...].reshape(nk, tt).T
        kio = lax.broadcasted_iota(jnp.int32, (nk, nk), 0).astype(jnp.float32)
        nt = (((1,), (1,)), ((), ()))

        def body(gidx, carry):
            t0 = pl.multiple_of(gidx * grp, grp)
            tiles = []
            for s in range(grp):
                i1row = jnp.broadcast_to(rows_ref[0, pl.ds(t0 + s, 1), :], (nk, nk))
                i2row = jnp.broadcast_to(rows_ref[1, pl.ds(t0 + s, 1), :], (nk, nk))
                grow = jnp.broadcast_to(rows_ref[2, pl.ds(t0 + s, 1), :], (nk, nk))
                m1 = jnp.where(i1row == kio, grow, 0.0).astype(jnp.bfloat16)
                p2 = jnp.where(i2row == kio, 1.0, 0.0).astype(jnp.bfloat16)
                tiles.append(lax.dot_general(m1, p2, nt, preferred_element_type=jnp.float32))
            w_ref[:, pl.ds(t0, grp), :] = jnp.swapaxes(jnp.stack(tiles, axis=0), 0, 1).astype(jnp.bfloat16)
            return carry

        lax.fori_loop(0, tt // grp, body, 0)

    a = jnp.dot(xb_ref[...], ut_ref[...], preferred_element_type=jnp.float32)
    nrow = eb // nk
    wj = jnp.concatenate([w_ref[j * nrow + r] for r in range(nrow)], axis=1)
    cmat = (wj.astype(jnp.float32) * _gelu(a)).astype(jnp.bfloat16)
    contrib = jnp.dot(cmat, v_ref[...], preferred_element_type=jnp.float32)

    last = pl.num_programs(1) - 1

    @pl.when(j == 0)
    def _first():
        o_ref[...] = contrib

    @pl.when((j > 0) & (j < last))
    def _middle():
        o_ref[...] += contrib

    @pl.when(j == last)
    def _finish():
        xn = res_ref[...] + g2_ref[0] * (o_ref[...] + contrib)
        o_ref[...] = xn
        hn_ref[...] = _rms(xn, ng_ref[...]) * (1.0 + nsc_ref[0]) + nsh_ref[0]


def peer_experts(h, i1, i2, g, ut_bf, v_bf, res, gate2, next_g, next_scale, next_shift, rows_per_mod, tt, eb):
    n, d = h.shape
    nexp = v_bf.shape[0]
    nblk = nexp // eb
    assert n % tt == 0 and nexp % eb == 0 and nblk >= 2 and nexp == PEER_NKEYS * PEER_NKEYS
    assert rows_per_mod % tt == 0
    tpm = rows_per_mod // tt
    rspec = pl.BlockSpec((PEER_HEADS, PEER_TOPK, tt), lambda i, j: (0, 0, i))
    tile = pl.BlockSpec((tt, d), lambda i, j: (i, 0))
    mod = pl.BlockSpec((1, 1, d), lambda i, j: (i // tpm, 0, 0))
    sds = jax.ShapeDtypeStruct((n, d), jnp.float32)
    return pl.pallas_call(
        functools.partial(_expert_kernel, tt=tt, eb=eb),
        grid=(n // tt, nblk),
        in_specs=[tile, rspec, rspec, rspec,
                  pl.BlockSpec((d, eb), lambda i, j: (0, j)),
                  pl.BlockSpec((eb, d), lambda i, j: (j, 0)),
                  tile, mod, pl.BlockSpec((1, d), lambda i, j: (0, 0)), mod, mod],
        out_specs=[tile, tile],
        out_shape=[sds, sds],
        scratch_shapes=[pltpu.VMEM((PEER_NKEYS, tt, PEER_NKEYS), jnp.bfloat16),
                        pltpu.VMEM((3, tt, PEER_NKEYS), jnp.float32),
                        pltpu.VMEM((tt, d), jnp.bfloat16)],
        compiler_params=pltpu.CompilerParams(dimension_semantics=("parallel", "arbitrary"),
                                             vmem_limit_bytes=PEER_VMEM_LIMIT),
        name="peer_experts",
    )(h, i1, i2, g, ut_bf, v_bf, res, gate2.reshape(-1, 1, d), next_g.reshape(1, d),
      next_scale.reshape(-1, 1, d), next_shift.reshape(-1, 1, d))


PEER_ROUTE_TILE = 512
PEER_TOKEN_TILE = 512
PEER_EXPERT_BLOCK = 1024


def peer_ffn(h, wq, k1, k2, u, v, res, gate2, next_g, next_scale, next_shift, rows_per_mod):
    bf = jnp.bfloat16
    i1, i2, g = peer_route(h, wq.astype(bf), k1.astype(bf), k2.astype(bf), PEER_ROUTE_TILE)
    return peer_experts(h, i1, i2, g, u.T.astype(bf), v.astype(bf), res, gate2, next_g, next_scale, next_shift,
                        rows_per_mod, PEER_TOKEN_TILE, PEER_EXPERT_BLOCK)


def kernel(x, c, ctx, c_ctx, mod_w, mod_b, norm1_g, norm2_g, w_in, conv_w, conv_b, hy_w1, hy_b1, hy_w2, hy_b2, hy_w3, hy_b3, hy_freq, hy_bias, rw_w0, rw_w_up, rw_a0, rw_a_up, rw_g_up, rw_k_k, rw_k_a, rw_r_k, rw_ln_w, rw_ln_b, attn_sink, hy_out_g, at_out_g, w_out, peer_wq, peer_k1, peer_k2, peer_u, peer_v, final_g):
    B, T, D = x.shape
    Lc = ctx.shape[1]
    c_s = jax.nn.silu(c)
    cc_s = jax.nn.silu(c_ctx)
    mods = [jnp.split(c_s @ mod_w[i] + mod_b[i], 6, axis=-1) for i in range(DEPTH)]
    cmods = [jnp.split(cc_s @ mod_w[i] + mod_b[i], 6, axis=-1) for i in range(DEPTH)]
    xl, xc = x.reshape(B * T, D), ctx.reshape(B * Lc, D)
    hl = rmsnorm(x, norm1_g[0]) * (1.0 + mods[0][1][:, None, :]) + mods[0][0][:, None, :]
    hc = rmsnorm(ctx, norm1_g[0]) * (1.0 + cmods[0][1]) + cmods[0][0]
    zeros = jnp.zeros((B, D), jnp.float32)
    for i in range(DEPTH):
        last = i == DEPTH - 1
        lp = dict(w_in=w_in[i], conv_w=conv_w[i], conv_b=conv_b[i],
                  hy_w1=hy_w1[i], hy_b1=hy_b1[i], hy_w2=hy_w2[i], hy_b2=hy_b2[i],
                  hy_w3=hy_w3[i], hy_b3=hy_b3[i], hy_freq=hy_freq[i], hy_bias=hy_bias[i],
                  rw_w0=rw_w0[i], rw_w_up=rw_w_up[i], rw_a0=rw_a0[i], rw_a_up=rw_a_up[i],
                  rw_g_up=rw_g_up[i], rw_k_k=rw_k_k[i], rw_k_a=rw_k_a[i], rw_r_k=rw_r_k[i],
                  rw_ln_w=rw_ln_w[i], rw_ln_b=rw_ln_b[i], attn_sink=attn_sink[i],
                  hy_out_g=hy_out_g[i], at_out_g=at_out_g[i], w_out=w_out[i])
        _, _, g1, sh2, sc2, g2 = mods[i]
        _, _, cg1, csh2, csc2, cg2 = cmods[i]
        parts_l, parts_c = token_mixer(hl, hc, lp, not last)
        peer_w = (peer_wq[i], peer_k1[i], peer_k2[i], peer_u[i], peer_v[i])
        out_w = (lp['hy_out_g'], lp['at_out_g'], norm2_g[i], lp['w_out'])
        flat = lambda parts: tuple(p.reshape(-1, p.shape[-1]) for p in parts)
        if last:
            next_l = (final_g, zeros, zeros)
        else:
            next_l = (norm1_g[i + 1], mods[i + 1][1], mods[i + 1][0])
        xl2, h2l = mix_out(flat(parts_l), xl, g1, sc2, sh2, *out_w, rows_per_mod=T)
        xl, hl = peer_ffn(h2l, *peer_w, xl2, g2, *next_l, rows_per_mod=T)
        hl = hl.reshape(B, T, D)
        if not last:
            xc2, h2c = mix_out(flat(parts_c), xc, cg1[None], csc2[None], csh2[None], *out_w, rows_per_mod=B * Lc)
            xc, hc = peer_ffn(h2c, *peer_w, xc2, cg2[None], norm1_g[i + 1], cmods[i + 1][1][None],
                              cmods[i + 1][0][None], rows_per_mod=B * Lc)
            hc = hc.reshape(B, Lc, D)
    return hl
```

```python
import functools
import math

import jax
import jax.numpy as jnp
import numpy as np
from jax import lax
from jax.experimental import pallas as pl
from jax.experimental.pallas import tpu as pltpu

D_MODEL = 1024
DEPTH = 4
GRID_W = 64

HY_CH = 256
RW_HEADS = 4
RW_HD = 64
RW_W = RW_HEADS * RW_HD
AT_HQ = 8
AT_HKV = 2
AT_G = AT_HQ // AT_HKV
AT_HD = 64
AT_W = AT_HQ * AT_HD
AT_KVW = AT_HKV * AT_HD

RW_DECAY_R = 64
RW_ICLR_R = 64
RW_GATE_R = 128
RW_LORA_W = 2 * RW_DECAY_R + 2 * RW_ICLR_R + RW_GATE_R
RW_GN_EPS = RW_HD * 1e-5

HY_EMB = 33
HY_BANDS = (HY_EMB - 1) // 2
HY_SHIFT = 0.05
HY_MIN_DECAY = math.log(1e-2) / 1.5
HY_MAX_DECAY = math.log(1e-2) / 0.3

ATT_WINDOW = 128
ATT_BLOCK = 128
ROPE_BASE = 10000.0
NEG_INF = -1e30

PEER_HEADS = 8
PEER_NKEYS = 128
PEER_DQ = 256
PEER_TOPK = 16
PEER_CHUNK = 128

OFF_HY = 0
OFF_RK = OFF_HY + 3 * HY_CH
OFF_LORA = OFF_RK + 3 * RW_W
OFF_AT = OFF_LORA + RW_LORA_W
CONV_COLS = OFF_LORA

VMEM_LIMIT = 48 * 1024 * 1024


def rmsnorm(x, g, eps=1e-6):
    y = x * lax.rsqrt(jnp.mean(x * x, axis=-1, keepdims=True) + eps)
    return y * g


def hyena_filter(L, w1, b1, w2, b2, w3, b3, freq):
    f32 = jnp.float32
    j = jnp.arange(L, dtype=f32)
    t = j / max(L - 1, 1)
    bands = jnp.linspace(1e-4, HY_BANDS - 1, HY_BANDS, dtype=f32)
    ang = 2.0 * math.pi * j[:, None] * bands[None, :] / L
    z = jnp.concatenate([t[:, None], jnp.cos(ang), -jnp.sin(ang)], axis=-1)
    h = jnp.sin(freq[0] * (z @ w1 + b1))
    h = jnp.sin(freq[1] * (h @ w2 + b2))
    h = (h @ w3 + b3).astype(f32).reshape(L, 2, HY_CH)
    deltas = jnp.abs(jnp.linspace(HY_MIN_DECAY, HY_MAX_DECAY, HY_CH, dtype=f32))
    window = jnp.exp(-t[:, None] * deltas[None, :]) + HY_SHIFT
    h = h * window[:, None, :]
    k = jnp.concatenate([h[:, 0], jnp.zeros((1, HY_CH), f32), h[:0:-1, 1]], axis=0)
    return k * lax.rsqrt(jnp.sum(k * k, axis=0, keepdims=True) + 1e-6)


HY_BLOCK = 256


def _hyena_kernel(fr_ref, u_ref, y_ref, *, nblk, nb):
    P = HY_BLOCK
    ncol = nblk * nb
    two_l = 2 * nblk * P
    width = two_l + 2 * P
    rolled = pltpu.roll(jnp.broadcast_to(fr_ref[0], (P, width)), 1, 1, stride=1, stride_axis=0)
    u = u_ref[0]
    col = lax.broadcasted_iota(jnp.int32, (P, ncol), 1)
    lhs, rhs = [], []
    for m in range(-(nblk - 1), nblk):
        start = (-P * m) % two_l
        lhs.append(rolled[:, start + P:start + 2 * P].astype(jnp.bfloat16))
        if m == 0:
            um = u
        elif m > 0:
            um = jnp.where(col >= nb * m, pltpu.roll(u, nb * m, 1), 0.0)
        else:
            um = jnp.where(col < ncol + nb * m, pltpu.roll(u, ncol + nb * m, 1), 0.0)
        rhs.append(um.astype(jnp.bfloat16))
    y_ref[0] = jnp.dot(jnp.concatenate(lhs, axis=1), jnp.concatenate(rhs, axis=0), preferred_element_type=jnp.float32)


def hyena_conv(u, filt):
    B, L, C = u.shape
    P = HY_BLOCK
    assert L % P == 0 and filt.shape == (2 * L, C)
    nblk = L // P
    idx = (P - 1 - np.arange(2 * L + 2 * P)) % (2 * L)
    fr = filt[idx].T.reshape(C, 1, 2 * L + 2 * P)
    uc = u.reshape(B, nblk, P, C).transpose(3, 2, 1, 0).reshape(C, P, nblk * B)
    y = pl.pallas_call(
        functools.partial(_hyena_kernel, nblk=nblk, nb=B),
        grid=(C,),
        in_specs=[pl.BlockSpec((1, 1, 2 * L + 2 * P), lambda c: (c, 0, 0)),
                  pl.BlockSpec((1, P, nblk * B), lambda c: (c, 0, 0))],
        out_specs=pl.BlockSpec((1, P, nblk * B), lambda c: (c, 0, 0)),
        out_shape=jax.ShapeDtypeStruct((C, P, nblk * B), jnp.float32),
        compiler_params=pltpu.CompilerParams(dimension_semantics=("parallel",), vmem_limit_bytes=VMEM_LIMIT),
        name="hyena_conv",
    )(fr, uc)
    return y.reshape(C, P, nblk, B).transpose(3, 2, 1, 0).reshape(B, L, C)


RW_LANES = 128
RW_TCHUNK = 32


RW_SUBLANE = 8


def _rwkv_scan_kernel(rf, wf, kf, vf, af, bf_, rb, wb, kb, vb, ab, bb, yf_ref, yb_ref, s_ref, rows_ref, *, tc):
    @pl.when(pl.program_id(0) == 0)
    def _init():
        s_ref[...] = jnp.zeros_like(s_ref)

    def both(xf, xb):
        return jnp.concatenate([xf, xb], axis=-1)

    vregs = [slice(g * RW_SUBLANE, (g + 1) * RW_SUBLANE) for g in range(RW_HD // RW_SUBLANE)]

    def step(t, carry):
        tb = tc - 1 - t
        rows_ref[0] = both(af[t], ab[tb])
        rows_ref[1] = both(wf[t], wb[tb])
        rows_ref[2] = both(bf_[t], bb[tb])
        rows_ref[3] = both(kf[t], kb[tb])
        rows_ref[4] = both(rf[t], rb[tb])
        v_t = both(vf[t], vb[tb])
        sa = [jnp.zeros((RW_SUBLANE, RW_LANES), jnp.float32) for _ in vregs]
        for k in range(RW_HD):
            a_b = rows_ref[0, pl.ds(k, 1), :]
            for g, sl in enumerate(vregs):
                sa[g] = sa[g] + s_ref[k, sl, :] * a_b
        y = [jnp.zeros((RW_SUBLANE, RW_LANES), jnp.float32) for _ in vregs]
        for k in range(RW_HD):
            w_b = rows_ref[1, pl.ds(k, 1), :]
            b_b = rows_ref[2, pl.ds(k, 1), :]
            k_b = rows_ref[3, pl.ds(k, 1), :]
            r_b = rows_ref[4, pl.ds(k, 1), :]
            for g, sl in enumerate(vregs):
                s_new = s_ref[k, sl, :] * w_b + sa[g] * b_b + v_t[sl] * k_b
                s_ref[k, sl, :] = s_new
                y[g] = y[g] + s_new * r_b
        for g, sl in enumerate(vregs):
            yf_ref[t, sl, :] = y[g]
            yb_ref[tb, sl, :] = y[g]
        return carry

    lax.fori_loop(0, tc, step, 0)


def rwkv_scan_lanes(fw, bw, n_ctx):
    L = fw[0].shape[0]
    half = RW_LANES // 2
    tc = RW_TCHUNK
    assert L % tc == 0 and n_ctx % tc == 0 and fw[0].shape[1:] == (RW_HD, half)
    nblk, cblk = L // tc, n_ctx // tc

    def fmap(i):
        return (i, 0, 0)

    def bmap(i):
        return (jnp.where(i < cblk, cblk - 1 - i, nblk - 1 + cblk - i), 0, 0)

    fspec = pl.BlockSpec((tc, RW_HD, half), fmap)
    bspec = pl.BlockSpec((tc, RW_HD, half), bmap)
    fout = pl.BlockSpec((tc, RW_HD, RW_LANES), fmap)
    bout = pl.BlockSpec((tc, RW_HD, RW_LANES), bmap)
    sds = jax.ShapeDtypeStruct((L, RW_HD, RW_LANES), jnp.float32)
    return pl.pallas_call(
        functools.partial(_rwkv_scan_kernel, tc=tc),
        grid=(nblk,),
        in_specs=[fspec] * 6 + [bspec] * 6,
        out_specs=[fout, bout],
        out_shape=[sds, sds],
        scratch_shapes=[pltpu.VMEM((RW_HD, RW_HD, RW_LANES), jnp.float32),
                        pltpu.VMEM((5, RW_HD, RW_LANES), jnp.float32)],
        compiler_params=pltpu.CompilerParams(dimension_semantics=("arbitrary",),
                                             vmem_limit_bytes=VMEM_LIMIT),
        name="rwkv_scan",
    )(*fw, *bw)


def rwkv7_mix(rkv_c, rkv_l, lora_c, lora_l, lp):
    f32 = jnp.float32
    B, Lc, _ = rkv_c.shape
    T = rkv_l.shape[1]
    Lt = Lc + T
    hs = (B, Lt, RW_HEADS, RW_HD)
    rkv = jnp.concatenate([rkv_c, rkv_l], axis=1)
    lora = jnp.concatenate([lora_c, lora_l], axis=1)
    r = rkv[..., :RW_W].reshape(hs)
    k = rkv[..., RW_W:2 * RW_W].reshape(hs)
    v = rkv[..., 2 * RW_W:].reshape(hs)
    kk = k * lp['rw_k_k'].reshape(RW_HEADS, RW_HD)
    kk = kk * lax.rsqrt(jnp.sum(kk * kk, axis=-1, keepdims=True) + 1e-12)
    k_a = lp['rw_k_a'].reshape(RW_HEADS, RW_HD)
    r_k = lp['rw_r_k']
    assert 2 * B * RW_HEADS == RW_LANES
    bonus = jnp.zeros(hs, f32)
    per_dir = []
    for d in range(2):
        wd = lora[..., d * RW_DECAY_R:(d + 1) * RW_DECAY_R]
        ad = lora[..., 2 * RW_DECAY_R + d * RW_ICLR_R:2 * RW_DECAY_R + (d + 1) * RW_ICLR_R]
        w_log = -jax.nn.softplus(-(lp['rw_w0'][d] + jnp.tanh(wd) @ lp['rw_w_up'][d])) - 0.5
        decay = jnp.exp(-jnp.exp(w_log)).reshape(hs)
        a = jax.nn.sigmoid(lp['rw_a0'][d] + ad @ lp['rw_a_up'][d]).reshape(hs)
        k_d = k * (1.0 + (a - 1.0) * k_a)
        per_dir.append((r, decay, k_d, v, -kk, kk * a))
        bonus = bonus + jnp.sum(r * k_d * r_k, axis=-1, keepdims=True) * v

    def lanes(x):
        return x.transpose(1, 3, 0, 2).reshape(Lt, RW_HD, B * RW_HEADS)

    yf, yb = rwkv_scan_lanes(tuple(lanes(x) for x in per_dir[0]), tuple(lanes(x) for x in per_dir[1]), n_ctx=Lc)
    half = RW_LANES // 2
    y_sum = (yf[..., :half] + yb[..., half:]).reshape(Lt, RW_HD, B, RW_HEADS).transpose(2, 0, 3, 1)
    mu = jnp.mean(y_sum, axis=-1, keepdims=True)
    var = jnp.mean(jnp.square(y_sum - mu), axis=-1, keepdims=True)
    yn = (y_sum - mu) * lax.rsqrt(var + RW_GN_EPS)
    yn = yn * lp['rw_ln_w'].reshape(RW_HEADS, RW_HD) + lp['rw_ln_b'].reshape(RW_HEADS, RW_HD)
    g = jax.nn.sigmoid(lora[..., 2 * RW_DECAY_R + 2 * RW_ICLR_R:]) @ lp['rw_g_up']
    out = (yn + bonus).reshape(B, Lt, RW_W) * g
    return out[:, :Lc], out[:, Lc:]


def rope_tables(T):
    f32 = jnp.float32
    quarter = AT_HD // 4
    inv = ROPE_BASE ** (-jnp.arange(quarter, dtype=f32) / quarter)
    rows = T // GRID_W
    row = jnp.repeat(jnp.arange(rows), GRID_W).astype(f32)
    col = jnp.tile(jnp.arange(GRID_W), rows).astype(f32)
    out_c, out_s = [], []
    for pos in (row, col):
        ang = pos[:, None] * inv[None, :]
        c, s = jnp.cos(ang), jnp.sin(ang)
        out_c += [c, c]
        out_s += [-s, s]
    return jnp.concatenate(out_c, axis=1), jnp.concatenate(out_s, axis=1)


def _rope(x, c, s):
    lane = lax.broadcasted_iota(jnp.int32, x.shape, 1)
    first = (lane % 32) < 16
    partner = jnp.where(first, pltpu.roll(x, x.shape[1] - 16, axis=1), pltpu.roll(x, 16, axis=1))
    return x * c + partner * s


def _attn_kernel(q_ref, kp_ref, kc_ref, kn_ref, vp_ref, vc_ref, vn_ref, kx_ref, vx_ref, sink_ref,
                 cq_ref, sq_ref, ckp_ref, skp_ref, ckc_ref, skc_ref, ckn_ref, skn_ref, o_ref, *, nblk):
    i = pl.program_id(1)
    blk = ATT_BLOCK
    scale = AT_HD ** -0.5
    q = _rope(q_ref[0], cq_ref[...], sq_ref[...])
    kspan = jnp.concatenate([_rope(kp_ref[0], ckp_ref[...], skp_ref[...]),
                             _rope(kc_ref[0], ckc_ref[...], skc_ref[...]),
                             _rope(kn_ref[0], ckn_ref[...], skn_ref[...])], axis=0)
    vspan = jnp.concatenate([vp_ref[0], vc_ref[0], vn_ref[0]], axis=0)
    kx = kx_ref[0]
    vx = vx_ref[0]
    qpos = lax.broadcasted_iota(jnp.int32, (blk, 3 * blk), 0) + blk
    kpos = lax.broadcasted_iota(jnp.int32, (blk, 3 * blk), 1)
    valid = jnp.abs(qpos - kpos) <= ATT_WINDOW
    valid = valid & ((kpos >= blk) | (i > 0)) & ((kpos < 2 * blk) | (i < nblk - 1))
    valid8 = jnp.concatenate([valid] * AT_HQ, axis=0)
    nt = (((1,), (1,)), ((), ()))
    bf = jnp.bfloat16
    s_lat, s_ctx, sks = [], [], []
    for g in range(AT_HKV):
        kg = kspan[:, g * AT_HD:(g + 1) * AT_HD].astype(bf)
        kxg = kx[:, g * AT_HD:(g + 1) * AT_HD].astype(bf)
        qg = jnp.concatenate([q[:, (g * AT_G + j) * AT_HD:(g * AT_G + j + 1) * AT_HD] for j in range(AT_G)],
                             axis=0).astype(bf)
        s_lat.append(lax.dot_general(qg, kg, nt, preferred_element_type=jnp.float32))
        s_ctx.append(lax.dot_general(qg, kxg, nt, preferred_element_type=jnp.float32))
        sks += [jnp.full((blk, 1), sink_ref[g * AT_G + j], jnp.float32) for j in range(AT_G)]
    s_lat = jnp.where(valid8, jnp.concatenate(s_lat, axis=0) * scale, NEG_INF)
    s_ctx = jnp.concatenate(s_ctx, axis=0) * scale
    sk = jnp.concatenate(sks, axis=0)
    m = jnp.maximum(jnp.maximum(jnp.max(s_lat, axis=1, keepdims=True), jnp.max(s_ctx, axis=1, keepdims=True)), sk)
    p_lat = jnp.exp(s_lat - m)
    p_ctx = jnp.exp(s_ctx - m)
    den = jnp.sum(p_lat, axis=1, keepdims=True) + jnp.sum(p_ctx, axis=1, keepdims=True) + jnp.exp(sk - m)
    inv = 1.0 / den
    p_lat = (p_lat * inv).astype(bf)
    p_ctx = (p_ctx * inv).astype(bf)
    outs = []
    rows = AT_G * blk
    for g in range(AT_HKV):
        vg = vspan[:, g * AT_HD:(g + 1) * AT_HD].astype(bf)
        vxg = vx[:, g * AT_HD:(g + 1) * AT_HD].astype(bf)
        o = (jnp.dot(p_lat[g * rows:(g + 1) * rows], vg, preferred_element_type=jnp.float32)
             + jnp.dot(p_ctx[g * rows:(g + 1) * rows], vxg, preferred_element_type=jnp.float32))
        outs += [o[j * blk:(j + 1) * blk] for j in range(AT_G)]
    o_ref[0] = jnp.concatenate(outs, axis=1)


def windowed_attention(q, k, v, kx, vx, sink):
    B, T, _ = q.shape
    Lc = kx.shape[1]
    assert T % ATT_BLOCK == 0 and ATT_WINDOW == ATT_BLOCK
    nblk = T // ATT_BLOCK
    c1, s1 = rope_tables(T)
    cq, sq = jnp.tile(c1, (1, AT_HQ)), jnp.tile(s1, (1, AT_HQ))
    ck, sk = jnp.tile(c1, (1, AT_HKV)), jnp.tile(s1, (1, AT_HKV))

    def prev(b, i):
        return (b, jnp.maximum(i - 1, 0), 0)

    def cur(b, i):
        return (b, i, 0)

    def nxt(b, i):
        return (b, jnp.minimum(i + 1, nblk - 1), 0)

    def kb(f):
        return pl.BlockSpec((1, ATT_BLOCK, AT_KVW), f)

    def tb(f):
        return pl.BlockSpec((ATT_BLOCK, AT_KVW), lambda b, i: f(b, i)[1:])

    return pl.pallas_call(
        functools.partial(_attn_kernel, nblk=nblk),
        grid=(B, nblk),
        in_specs=[pl.BlockSpec((1, ATT_BLOCK, AT_W), cur), kb(prev), kb(cur), kb(nxt), kb(prev), kb(cur), kb(nxt),
                  pl.BlockSpec((1, Lc, AT_KVW), lambda b, i: (b, 0, 0)),
                  pl.BlockSpec((1, Lc, AT_KVW), lambda b, i: (b, 0, 0)),
                  pl.BlockSpec(memory_space=pltpu.SMEM),
                  pl.BlockSpec((ATT_BLOCK, AT_W), lambda b, i: (i, 0)),
                  pl.BlockSpec((ATT_BLOCK, AT_W), lambda b, i: (i, 0)),
                  tb(prev), tb(prev), tb(cur), tb(cur), tb(nxt), tb(nxt)],
        out_specs=pl.BlockSpec((1, ATT_BLOCK, AT_W), cur),
        out_shape=jax.ShapeDtypeStruct((B, T, AT_W), jnp.float32),
        compiler_params=pltpu.CompilerParams(dimension_semantics=("parallel", "arbitrary"),
                                             vmem_limit_bytes=VMEM_LIMIT),
        name="window_attention",
    )(q, k, k, k, v, v, v, kx, vx, sink, cq, sq, ck, sk, ck, sk, ck, sk)


def context_attention(qc, kc, vc, sink):
    B, Lc = qc.shape[:2]
    scale = AT_HD ** -0.5
    s = jnp.einsum('bqgjd,bkgd->bgjqk', qc, kc) * scale
    sink_b = jnp.broadcast_to(sink[None, :, :, None, None], (B, AT_HKV, AT_G, Lc, 1))
    p = jax.nn.softmax(jnp.concatenate([s, sink_b], axis=-1), axis=-1)[..., :Lc]
    return jnp.einsum('bgjqk,bkgd->bqgjd', p, vc).reshape(B, Lc, AT_W)


INPROJ_ROWS = 512
SUBLANE = 8


def _inproj_kernel(h_ref, hp_ref, hn_ref, w_ref, cw_ref, cb_ref, x0_ref, u_ref, rkv_ref, lora_ref, q_ref, k_ref, v_ref,
                   *, tiles_per_seq):
    i = pl.program_id(0)
    tm = h_ref.shape[0]
    bf = jnp.bfloat16
    z = jnp.dot(h_ref[...].astype(bf), w_ref[...], preferred_element_type=jnp.float32)
    wc = w_ref[:, :CONV_COLS]
    zp = jnp.dot(hp_ref[...].astype(bf), wc, preferred_element_type=jnp.float32)[SUBLANE - 1:SUBLANE]
    zn = jnp.dot(hn_ref[...].astype(bf), wc, preferred_element_type=jnp.float32)[0:1]
    pos = i % tiles_per_seq
    zp = jnp.where(pos == 0, 0.0, zp)
    zn = jnp.where(pos == tiles_per_seq - 1, 0.0, zn)
    zc = z[:, :CONV_COLS]
    row = lax.broadcasted_iota(jnp.int32, zc.shape, 0)
    up = jnp.where(row == 0, zp, pltpu.roll(zc, 1, 0))
    dn = jnp.where(row == tm - 1, zn, pltpu.roll(zc, tm - 1, 0))
    s = up * cw_ref[0:1] + zc * cw_ref[1:2] + dn * cw_ref[2:3] + cb_ref[...]
    x0_ref[...] = s[:, :HY_CH]
    u_ref[...] = s[:, HY_CH:2 * HY_CH] * s[:, 2 * HY_CH:OFF_RK]
    rkv_ref[...] = s[:, OFF_RK:]
    lora_ref[...] = z[:, OFF_LORA:OFF_AT]
    q_ref[...] = z[:, OFF_AT:OFF_AT + AT_W]
    k_ref[...] = z[:, OFF_AT + AT_W:OFF_AT + AT_W + AT_KVW]
    v_ref[...] = z[:, OFF_AT + AT_W + AT_KVW:]


def in_proj(h3, w_in, conv_w, conv_b):
    nseq, seq_len, d = h3.shape
    n = nseq * seq_len
    tm = min(INPROJ_ROWS, seq_len)
    assert seq_len % tm == 0 and w_in.shape == (d, OFF_AT + AT_W + 2 * AT_KVW)
    tps = seq_len // tm
    nb8 = n // SUBLANE
    r8 = tm // SUBLANE
    h = h3.reshape(n, d)

    def out(w):
        return pl.BlockSpec((tm, w), lambda i: (i, 0))

    widths = (HY_CH, HY_CH, 3 * RW_W, RW_LORA_W, AT_W, AT_KVW, AT_KVW)
    outs = pl.pallas_call(
        functools.partial(_inproj_kernel, tiles_per_seq=tps),
        grid=(n // tm,),
        in_specs=[pl.BlockSpec((tm, d), lambda i: (i, 0)),
                  pl.BlockSpec((SUBLANE, d), lambda i: (jnp.maximum(i * r8 - 1, 0), 0)),
                  pl.BlockSpec((SUBLANE, d), lambda i: (jnp.minimum((i + 1) * r8, nb8 - 1), 0)),
                  pl.BlockSpec(w_in.shape, lambda i: (0, 0)),
                  pl.BlockSpec((3, CONV_COLS), lambda i: (0, 0)),
                  pl.BlockSpec((1, CONV_COLS), lambda i: (0, 0))],
        out_specs=[out(w) for w in widths],
        out_shape=[jax.ShapeDtypeStruct((n, w), jnp.float32) for w in widths],
        compiler_params=pltpu.CompilerParams(dimension_semantics=("parallel",), vmem_limit_bytes=VMEM_LIMIT),
        name="in_proj",
    )(h, h, h, w_in.astype(jnp.bfloat16), conv_w, conv_b.reshape(1, -1))
    return tuple(o.reshape(nseq, seq_len, o.shape[-1]) for o in outs)


def token_mixer(hl, hc, lp, need_ctx):
    B, T, _ = hl.shape
    Lc = hc.shape[1]
    x0_l, u_l, rkv_l, lora_l, q_l, k_l, v_l = in_proj(hl, lp['w_in'], lp['conv_w'], lp['conv_b'])
    x0_c, u_c, rkv_c, lora_c, q_c, k_c, v_c = in_proj(hc, lp['w_in'], lp['conv_w'], lp['conv_b'])
    filt_p = (lp['hy_w1'], lp['hy_b1'], lp['hy_w2'], lp['hy_b2'], lp['hy_w3'], lp['hy_b3'], lp['hy_freq'])
    hy_l = x0_l * (hyena_conv(u_l, hyena_filter(T, *filt_p)) + u_l * lp['hy_bias'])
    rw_c, rw_l = rwkv7_mix(rkv_c, rkv_l, lora_c, lora_l, lp)
    at_l = windowed_attention(q_l, k_l, v_l, k_c, v_c, lp['attn_sink'])
    parts_l = (hy_l, rw_l, at_l)
    if not need_ctx:
        return parts_l, None
    hy_c = x0_c * (hyena_conv(u_c, hyena_filter(Lc, *filt_p)) + u_c * lp['hy_bias'])
    at_c = context_attention(q_c.reshape(B, Lc, AT_HKV, AT_G, AT_HD), k_c.reshape(B, Lc, AT_HKV, AT_HD),
                             v_c.reshape(B, Lc, AT_HKV, AT_HD), lp['attn_sink'].reshape(AT_HKV, AT_G))
    return parts_l, (hy_c, rw_c, at_c)


NORM_EPS = 1e-6
MIXOUT_ROWS = 512


def _rms(x, g):
    return (x * lax.rsqrt(jnp.mean(x * x, axis=-1, keepdims=True) + NORM_EPS)) * g


def _mixout_kernel(hy_ref, rw_ref, at_ref, x_ref, g1_ref, sc2_ref, sh2_ref, hyg_ref, atg_ref, n2g_ref, w_ref,
                   xo_ref, h2_ref):
    mix = jnp.concatenate([_rms(hy_ref[...], hyg_ref[...]), rw_ref[...], _rms(at_ref[...], atg_ref[...])], axis=1)
    m = jnp.dot(mix.astype(jnp.bfloat16), w_ref[...], preferred_element_type=jnp.float32)
    xn = x_ref[...] + g1_ref[0] * m
    xo_ref[...] = xn
    h2_ref[...] = (_rms(xn, n2g_ref[...]) * (1.0 + sc2_ref[0]) + sh2_ref[0]).astype(h2_ref.dtype)


def mix_out(parts, x, gate1, scale2, shift2, hy_g, at_g, norm2_g, w_out, rows_per_mod):
    hy, rw, at = parts
    n, d = x.shape
    tm = MIXOUT_ROWS
    assert n % tm == 0 and rows_per_mod % tm == 0
    tpm = rows_per_mod // tm

    def row(w):
        return pl.BlockSpec((tm, w), lambda i: (i, 0))

    def vec(w):
        return pl.BlockSpec((1, w), lambda i: (0, 0))

    mod = pl.BlockSpec((1, 1, d), lambda i: (i // tpm, 0, 0))
    return pl.pallas_call(
        _mixout_kernel,
        grid=(n // tm,),
        in_specs=[row(hy.shape[1]), row(rw.shape[1]), row(at.shape[1]), row(d), mod, mod, mod,
                  vec(hy.shape[1]), vec(at.shape[1]), vec(d), pl.BlockSpec(w_out.shape, lambda i: (0, 0))],
        out_specs=[row(d), row(d)],
        out_shape=[jax.ShapeDtypeStruct((n, d), jnp.float32), jax.ShapeDtypeStruct((n, d), jnp.bfloat16)],
        compiler_params=pltpu.CompilerParams(dimension_semantics=("parallel",), vmem_limit_bytes=VMEM_LIMIT),
        name="mix_out",
    )(hy, rw, at, x, gate1.reshape(-1, 1, d), scale2.reshape(-1, 1, d), shift2.reshape(-1, 1, d),
      hy_g.reshape(1, -1), at_g.reshape(1, -1), norm2_g.reshape(1, -1), w_out.astype(jnp.bfloat16))


def _top16_rows(s, n_rows):
    rio = lax.broadcasted_iota(jnp.int32, s.shape, 0)
    vals, idxs = [], []
    for _ in range(PEER_TOPK):
        m = jnp.max(s, axis=0, keepdims=True)
        idx = jnp.min(jnp.where(s == m, rio, n_rows), axis=0, keepdims=True)
        vals.append(m)
        idxs.append(idx)
        s = jnp.where(rio == idx, -jnp.inf, s)
    return jnp.concatenate(vals, axis=0), jnp.concatenate(idxs, axis=0)


LANE = 128
ROUTE_UNROLL = 4


def _cand_layout():
    rows = [(0, b) for b in range(PEER_TOPK)]
    for a in range(1, 8):
        rows += [(a, b) for b in range(8)]
    rows += [(a, 0) for a in range(8, PEER_TOPK)]
    a = np.array([r[0] for r in rows])
    b = np.array([r[1] for r in rows])
    return a, b, (a + 1) * (b + 1) <= PEER_TOPK


def cand_table():
    a, b, valid = _cand_layout()
    cio = np.where(valid, a * PEER_TOPK + b, PEER_TOPK * PEER_TOPK).astype(np.int32)
    return jnp.asarray(np.broadcast_to(cio[:, None], (cio.shape[0], LANE)))


def _route_kernel(x_ref, wq_ref, k1_ref, k2_ref, cio_ref, i1_ref, i2_ref, g_ref, q_ref, *, tt):
    q = jnp.dot(x_ref[...].astype(jnp.bfloat16), wq_ref[...], preferred_element_type=jnp.float32)
    q_ref[...] = q.astype(jnp.bfloat16)
    nt = (((1,), (1,)), ((), ()))
    cio = cio_ref[...]
    ncand = PEER_TOPK * PEER_TOPK

    def chunk(c, carry):
        t0 = pl.multiple_of(c * LANE, LANE)
        qc = q_ref[pl.ds(t0, LANE), :]
        s1 = lax.dot_general(k1_ref[0], qc[:, :PEER_DQ // 2], nt, preferred_element_type=jnp.float32)
        s2 = lax.dot_general(k2_ref[0], qc[:, PEER_DQ // 2:], nt, preferred_element_type=jnp.float32)
        v1, j1 = _top16_rows(s1, PEER_NKEYS)
        v2, j2 = _top16_rows(s2, PEER_NKEYS)
        j1 = j1 * PEER_NKEYS
        cand = jnp.concatenate([v1[0:1] + v2] + [v1[a:a + 1] + v2[0:8] for a in range(1, 8)]
                               + [v1[8:16] + v2[0:1]], axis=0)
        ecand = jnp.concatenate([j1[0:1] + j2] + [j1[a:a + 1] + j2[0:8] for a in range(1, 8)]
                                + [j1[8:16] + j2[0:1]], axis=0)
        cand = jnp.where(cio < ncand, cand, -jnp.inf)
        scs, es = [], []
        for _ in range(PEER_TOPK):
            m = jnp.max(cand, axis=0, keepdims=True)
            cidx = jnp.min(jnp.where(cand == m, cio, ncand), axis=0, keepdims=True)
            sel = cio == cidx
            es.append(jnp.max(jnp.where(sel, ecand, -1), axis=0, keepdims=True))
            scs.append(m)
            cand = jnp.where(sel, -jnp.inf, cand)
        sc = jnp.concatenate(scs, axis=0)
        e = jnp.concatenate(es, axis=0)
        p = jnp.exp(sc - sc[0:1])
        i1_ref[0, :, pl.ds(t0, LANE)] = e >> 7
        i2_ref[0, :, pl.ds(t0, LANE)] = e & (PEER_NKEYS - 1)
        g_ref[0, :, pl.ds(t0, LANE)] = p / jnp.sum(p, axis=0, keepdims=True)
        return carry

    lax.fori_loop(0, tt // LANE, chunk, 0, unroll=ROUTE_UNROLL)


def peer_route(h, wq_bf, k1_bf, k2_bf, tt):
    n, d = h.shape
    assert n % tt == 0 and tt % (ROUTE_UNROLL * LANE) == 0
    cio = cand_table()
    out_sds = [jax.ShapeDtypeStruct((PEER_HEADS, PEER_TOPK, n), jnp.int32),
               jax.ShapeDtypeStruct((PEER_HEADS, PEER_TOPK, n), jnp.int32),
               jax.ShapeDtypeStruct((PEER_HEADS, PEER_TOPK, n), jnp.float32)]
    ospec = pl.BlockSpec((1, PEER_TOPK, tt), lambda i, hh: (hh, 0, i))
    return pl.pallas_call(
        functools.partial(_route_kernel, tt=tt),
        grid=(n // tt, PEER_HEADS),
        in_specs=[pl.BlockSpec((tt, d), lambda i, hh: (i, 0)),
                  pl.BlockSpec((d, PEER_DQ), lambda i, hh: (0, hh)),
                  pl.BlockSpec((1, PEER_NKEYS, PEER_DQ // 2), lambda i, hh: (hh, 0, 0)),
                  pl.BlockSpec((1, PEER_NKEYS, PEER_DQ // 2), lambda i, hh: (hh, 0, 0)),
                  pl.BlockSpec(cio.shape, lambda i, hh: (0, 0))],
        out_specs=[ospec, ospec, ospec],
        out_shape=out_sds,
        scratch_shapes=[pltpu.VMEM((tt, PEER_DQ), jnp.bfloat16)],
        compiler_params=pltpu.CompilerParams(dimension_semantics=("parallel", "arbitrary"),
                                             vmem_limit_bytes=VMEM_LIMIT),
        name="peer_route",
    )(h, wq_bf, k1_bf, k2_bf, cio)


def _gelu(x):
    return 0.5 * x * (1.0 + lax.erf(x * (2.0 ** -0.5)))


PEER_BUILD_GROUP = 32
PEER_VMEM_LIMIT = 56 * 1024 * 1024


def _expert_kernel(x_ref, i1_ref, i2_ref, g_ref, ut_ref, v_ref, res_ref, g2_ref, ng_ref, nsc_ref, nsh_ref,
                   o_ref, hn_ref, w_ref, rows_ref, xb_ref, *, tt, eb):
    j = pl.program_id(1)
    nk = PEER_NKEYS
    grp = PEER_BUILD_GROUP

    @pl.when(j == 0)
    def _build():
        xb_ref[...] = x_ref[...].astype(jnp.bfloat16)
        rows_ref[0] = i1_ref[...].reshape(nk, tt).astype(jnp.float32).T
        rows_ref[1] = i2_ref[...].reshape(nk, tt).astype(jnp.float32).T
        rows_ref[2] = g_ref[...].reshape(nk, tt).T
        kio = lax.broadcasted_iota(jnp.int32, (nk, nk), 0).astype(jnp.float32)
        nt = (((1,), (1,)), ((), ()))

        def body(gidx, carry):
            t0 = pl.multiple_of(gidx * grp, grp)
            tiles = []
            for s in range(grp):
                i1row = jnp.broadcast_to(rows_ref[0, pl.ds(t0 + s, 1), :], (nk, nk))
                i2row = jnp.broadcast_to(rows_ref[1, pl.ds(t0 + s, 1), :], (nk, nk))
                grow = jnp.broadcast_to(rows_ref[2, pl.ds(t0 + s, 1), :], (nk, nk))
                m1 = jnp.where(i1row == kio, grow, 0.0).astype(jnp.bfloat16)
                p2 = jnp.where(i2row == kio, 1.0, 0.0).astype(jnp.bfloat16)
                tiles.append(lax.dot_general(m1, p2, nt, preferred_element_type=jnp.float32))
            w_ref[:, pl.ds(t0, grp), :] = jnp.swapaxes(jnp.stack(tiles, axis=0), 0, 1).astype(jnp.bfloat16)
            return carry

        lax.fori_loop(0, tt // grp, body, 0)

    a = jnp.dot(xb_ref[...], ut_ref[...], preferred_element_type=jnp.float32)
    nrow = eb // nk
    wj = jnp.concatenate([w_ref[j * nrow + r] for r in range(nrow)], axis=1)
    cmat = (wj.astype(jnp.float32) * _gelu(a)).astype(jnp.bfloat16)
    contrib = jnp.dot(cmat, v_ref[...], preferred_element_type=jnp.float32)

    last = pl.num_programs(1) - 1

    @pl.when(j == 0)
    def _first():
        o_ref[...] = contrib

    @pl.when((j > 0) & (j < last))
    def _middle():
        o_ref[...] += contrib

    @pl.when(j == last)
    def _finish():
        xn = res_ref[...] + g2_ref[0] * (o_ref[...] + contrib)
        o_ref[...] = xn
        hn_ref[...] = _rms(xn, ng_ref[...]) * (1.0 + nsc_ref[0]) + nsh_ref[0]


def peer_experts(h, i1, i2, g, ut_bf, v_bf, res, gate2, next_g, next_scale, next_shift, rows_per_mod, tt, eb):
    n, d = h.shape
    nexp = v_bf.shape[0]
    nblk = nexp // eb
    assert n % tt == 0 and nexp % eb == 0 and nblk >= 2 and nexp == PEER_NKEYS * PEER_NKEYS
    assert rows_per_mod % tt == 0
    tpm = rows_per_mod // tt
    rspec = pl.BlockSpec((PEER_HEADS, PEER_TOPK, tt), lambda i, j: (0, 0, i))
    tile = pl.BlockSpec((tt, d), lambda i, j: (i, 0))
    mod = pl.BlockSpec((1, 1, d), lambda i, j: (i // tpm, 0, 0))
    sds = jax.ShapeDtypeStruct((n, d), jnp.float32)
    return pl.pallas_call(
        functools.partial(_expert_kernel, tt=tt, eb=eb),
        grid=(n // tt, nblk),
        in_specs=[tile, rspec, rspec, rspec,
                  pl.BlockSpec((d, eb), lambda i, j: (0, j)),
                  pl.BlockSpec((eb, d), lambda i, j: (j, 0)),
                  tile, mod, pl.BlockSpec((1, d), lambda i, j: (0, 0)), mod, mod],
        out_specs=[tile, tile],
        out_shape=[sds, sds],
        scratch_shapes=[pltpu.VMEM((PEER_NKEYS, tt, PEER_NKEYS), jnp.bfloat16),
                        pltpu.VMEM((3, tt, PEER_NKEYS), jnp.float32),
                        pltpu.VMEM((tt, d), jnp.bfloat16)],
        compiler_params=pltpu.CompilerParams(dimension_semantics=("parallel", "arbitrary"),
                                             vmem_limit_bytes=PEER_VMEM_LIMIT),
        name="peer_experts",
    )(h, i1, i2, g, ut_bf, v_bf, res, gate2.reshape(-1, 1, d), next_g.reshape(1, d),
      next_scale.reshape(-1, 1, d), next_shift.reshape(-1, 1, d))


PEER_ROUTE_TILE = 512
PEER_TOKEN_TILE = 512
PEER_EXPERT_BLOCK = 1024


def peer_ffn(h, wq, k1, k2, u, v, res, gate2, next_g, next_scale, next_shift, rows_per_mod):
    bf = jnp.bfloat16
    i1, i2, g = peer_route(h, wq.astype(bf), k1.astype(bf), k2.astype(bf), PEER_ROUTE_TILE)
    return peer_experts(h, i1, i2, g, u.T.astype(bf), v.astype(bf), res, gate2, next_g, next_scale, next_shift,
                        rows_per_mod, PEER_TOKEN_TILE, PEER_EXPERT_BLOCK)


def kernel(x, c, ctx, c_ctx, mod_w, mod_b, norm1_g, norm2_g, w_in, conv_w, conv_b, hy_w1, hy_b1, hy_w2, hy_b2, hy_w3, hy_b3, hy_freq, hy_bias, rw_w0, rw_w_up, rw_a0, rw_a_up, rw_g_up, rw_k_k, rw_k_a, rw_r_k, rw_ln_w, rw_ln_b, attn_sink, hy_out_g, at_out_g, w_out, peer_wq, peer_k1, peer_k2, peer_u, peer_v, final_g):
    B, T, D = x.shape
    Lc = ctx.shape[1]
    c_s = jax.nn.silu(c)
    cc_s = jax.nn.silu(c_ctx)
    mods = [jnp.split(c_s @ mod_w[i] + mod_b[i], 6, axis=-1) for i in range(DEPTH)]
    cmods = [jnp.split(cc_s @ mod_w[i] + mod_b[i], 6, axis=-1) for i in range(DEPTH)]
    xl, xc = x.reshape(B * T, D), ctx.reshape(B * Lc, D)
    hl = rmsnorm(x, norm1_g[0]) * (1.0 + mods[0][1][:, None, :]) + mods[0][0][:, None, :]
    hc = rmsnorm(ctx, norm1_g[0]) * (1.0 + cmods[0][1]) + cmods[0][0]
    zeros = jnp.zeros((B, D), jnp.float32)
    for i in range(DEPTH):
        last = i == DEPTH - 1
        lp = dict(w_in=w_in[i], conv_w=conv_w[i], conv_b=conv_b[i],
                  hy_w1=hy_w1[i], hy_b1=hy_b1[i], hy_w2=hy_w2[i], hy_b2=hy_b2[i],
                  hy_w3=hy_w3[i], hy_b3=hy_b3[i], hy_freq=hy_freq[i], hy_bias=hy_bias[i],
                  rw_w0=rw_w0[i], rw_w_up=rw_w_up[i], rw_a0=rw_a0[i], rw_a_up=rw_a_up[i],
                  rw_g_up=rw_g_up[i], rw_k_k=rw_k_k[i], rw_k_a=rw_k_a[i], rw_r_k=rw_r_k[i],
                  rw_ln_w=rw_ln_w[i], rw_ln_b=rw_ln_b[i], attn_sink=attn_sink[i],
                  hy_out_g=hy_out_g[i], at_out_g=at_out_g[i], w_out=w_out[i])
        _, _, g1, sh2, sc2, g2 = mods[i]
        _, _, cg1, csh2, csc2, cg2 = cmods[i]
        parts_l, parts_c = token_mixer(hl, hc, lp, not last)
        peer_w = (peer_wq[i], peer_k1[i], peer_k2[i], peer_u[i], peer_v[i])
        out_w = (lp['hy_out_g'], lp['at_out_g'], norm2_g[i], lp['w_out'])
        flat = lambda parts: tuple(p.reshape(-1, p.shape[-1]) for p in parts)
        if last:
            next_l = (final_g, zeros, zeros)
        else:
            next_l = (norm1_g[i + 1], mods[i + 1][1], mods[i + 1][0])
        xl2, h2l = mix_out(flat(parts_l), xl, g1, sc2, sh2, *out_w, rows_per_mod=T)
        xl, hl = peer_ffn(h2l, *peer_w, xl2, g2, *next_l, rows_per_mod=T)
        hl = hl.reshape(B, T, D)
        if not last:
            xc2, h2c = mix_out(flat(parts_c), xc, cg1[None], csc2[None], csh2[None], *out_w, rows_per_mod=B * Lc)
            xc, hc = peer_ffn(h2c, *peer_w, xc2, cg2[None], norm1_g[i + 1], cmods[i + 1][1][None],
                              cmods[i + 1][0][None], rows_per_mod=B * Lc)
            hc = hc.reshape(B, Lc, D)
    return hl
```

```python
import functools
import math

import jax
import jax.numpy as jnp
import numpy as np
from jax import lax
from jax.experimental import pallas as pl
from jax.experimental.pallas import tpu as pltpu

D_MODEL = 1024
DEPTH = 4
GRID_W = 64

HY_CH = 256
RW_HEADS = 4
RW_HD = 64
RW_W = RW_HEADS * RW_HD
AT_HQ = 8
AT_HKV = 2
AT_G = AT_HQ // AT_HKV
AT_HD = 64
AT_W = AT_HQ * AT_HD
AT_KVW = AT_HKV * AT_HD

RW_DECAY_R = 64
RW_ICLR_R = 64
RW_GATE_R = 128
RW_LORA_W = 2 * RW_DECAY_R + 2 * RW_ICLR_R + RW_GATE_R
RW_GN_EPS = RW_HD * 1e-5

HY_EMB = 33
HY_BANDS = (HY_EMB - 1) // 2
HY_SHIFT = 0.05
HY_MIN_DECAY = math.log(1e-2) / 1.5
HY_MAX_DECAY = math.log(1e-2) / 0.3

ATT_WINDOW = 128
ATT_BLOCK = 128
ROPE_BASE = 10000.0
NEG_INF = -1e30

PEER_HEADS = 8
PEER_NKEYS = 128
PEER_DQ = 256
PEER_TOPK = 16
PEER_CHUNK = 128

OFF_HY = 0
OFF_RK = OFF_HY + 3 * HY_CH
OFF_LORA = OFF_RK + 3 * RW_W
OFF_AT = OFF_LORA + RW_LORA_W
CONV_COLS = OFF_LORA

VMEM_LIMIT = 48 * 1024 * 1024


def rmsnorm(x, g, eps=1e-6):
    y = x * lax.rsqrt(jnp.mean(x * x, axis=-1, keepdims=True) + eps)
    return y * g


def hyena_filter(L, w1, b1, w2, b2, w3, b3, freq):
    f32 = jnp.float32
    j = jnp.arange(L, dtype=f32)
    t = j / max(L - 1, 1)
    bands = jnp.linspace(1e-4, HY_BANDS - 1, HY_BANDS, dtype=f32)
    ang = 2.0 * math.pi * j[:, None] * bands[None, :] / L
    z = jnp.concatenate([t[:, None], jnp.cos(ang), -jnp.sin(ang)], axis=-1)
    h = jnp.sin(freq[0] * (z @ w1 + b1))
    h = jnp.sin(freq[1] * (h @ w2 + b2))
    h = (h @ w3 + b3).astype(f32).reshape(L, 2, HY_CH)
    deltas = jnp.abs(jnp.linspace(HY_MIN_DECAY, HY_MAX_DECAY, HY_CH, dtype=f32))
    window = jnp.exp(-t[:, None] * deltas[None, :]) + HY_SHIFT
    h = h * window[:, None, :]
    k = jnp.concatenate([h[:, 0], jnp.zeros((1, HY_CH), f32), h[:0:-1, 1]], axis=0)
    return k * lax.rsqrt(jnp.sum(k * k, axis=0, keepdims=True) + 1e-6)


HY_BLOCK = 256


def _hyena_kernel(fr_ref, u_ref, y_ref, *, nblk, nb):
    P = HY_BLOCK
    ncol = nblk * nb
    two_l = 2 * nblk * P
    width = two_l + 2 * P
    rolled = pltpu.roll(jnp.broadcast_to(fr_ref[0], (P, width)), 1, 1, stride=1, stride_axis=0)
    u = u_ref[0]
    col = lax.broadcasted_iota(jnp.int32, (P, ncol), 1)
    lhs, rhs = [], []
    for m in range(-(nblk - 1), nblk):
        start = (-P * m) % two_l
        lhs.append(rolled[:, start + P:start + 2 * P].astype(jnp.bfloat16))
        if m == 0:
            um = u
        elif m > 0:
            um = jnp.where(col >= nb * m, pltpu.roll(u, nb * m, 1), 0.0)
        else:
            um = jnp.where(col < ncol + nb * m, pltpu.roll(u, ncol + nb * m, 1), 0.0)
        rhs.append(um.astype(jnp.bfloat16))
    y_ref[0] = jnp.dot(jnp.concatenate(lhs, axis=1), jnp.concatenate(rhs, axis=0), preferred_element_type=jnp.float32)


def hyena_conv(u, filt):
    B, L, C = u.shape
    P = HY_BLOCK
    assert L % P == 0 and filt.shape == (2 * L, C)
    nblk = L // P
    idx = (P - 1 - np.arange(2 * L + 2 * P)) % (2 * L)
    fr = filt[idx].T.reshape(C, 1, 2 * L + 2 * P)
    uc = u.reshape(B, nblk, P, C).transpose(3, 2, 1, 0).reshape(C, P, nblk * B)
    y = pl.pallas_call(
        functools.partial(_hyena_kernel, nblk=nblk, nb=B),
        grid=(C,),
        in_specs=[pl.BlockSpec((1, 1, 2 * L + 2 * P), lambda c: (c, 0, 0)),
                  pl.BlockSpec((1, P, nblk * B), lambda c: (c, 0, 0))],
        out_specs=pl.BlockSpec((1, P, nblk * B), lambda c: (c, 0, 0)),
        out_shape=jax.ShapeDtypeStruct((C, P, nblk * B), jnp.float32),
        compiler_params=pltpu.CompilerParams(dimension_semantics=("parallel",), vmem_limit_bytes=VMEM_LIMIT),
        name="hyena_conv",
    )(fr, uc)
    return y.reshape(C, P, nblk, B).transpose(3, 2, 1, 0).reshape(B, L, C)


RW_LANES = 128
RW_TCHUNK = 32


RW_SUBLANE = 8


def _rwkv_scan_kernel(rf, wf, kf, vf, af, bf_, rb, wb, kb, vb, ab, bb, yf_ref, yb_ref, s_ref, rows_ref, *, tc):
    @pl.when(pl.program_id(0) == 0)
    def _init():
        s_ref[...] = jnp.zeros_like(s_ref)

    def both(xf, xb):
        return jnp.concatenate([xf, xb], axis=-1)

    vregs = [slice(g * RW_SUBLANE, (g + 1) * RW_SUBLANE) for g in range(RW_HD // RW_SUBLANE)]

    def step(t, carry):
        tb = tc - 1 - t
        rows_ref[0] = both(af[t], ab[tb])
        rows_ref[1] = both(wf[t], wb[tb])
        rows_ref[2] = both(bf_[t], bb[tb])
        rows_ref[3] = both(kf[t], kb[tb])
        rows_ref[4] = both(rf[t], rb[tb])
        v_t = both(vf[t], vb[tb])
        sa = [jnp.zeros((RW_SUBLANE, RW_LANES), jnp.float32) for _ in vregs]
        for k in range(RW_HD):
            a_b = rows_ref[0, pl.ds(k, 1), :]
            for g, sl in enumerate(vregs):
                sa[g] = sa[g] + s_ref[k, sl, :] * a_b
        y = [jnp.zeros((RW_SUBLANE, RW_LANES), jnp.float32) for _ in vregs]
        for k in range(RW_HD):
            w_b = rows_ref[1, pl.ds(k, 1), :]
            b_b = rows_ref[2, pl.ds(k, 1), :]
            k_b = rows_ref[3, pl.ds(k, 1), :]
            r_b = rows_ref[4, pl.ds(k, 1), :]
            for g, sl in enumerate(vregs):
                s_new = s_ref[k, sl, :] * w_b + sa[g] * b_b + v_t[sl] * k_b
                s_ref[k, sl, :] = s_new
                y[g] = y[g] + s_new * r_b
        for g, sl in enumerate(vregs):
            yf_ref[t, sl, :] = y[g]
            yb_ref[tb, sl, :] = y[g]
        return carry

    lax.fori_loop(0, tc, step, 0)


def rwkv_scan_lanes(fw, bw, n_ctx):
    L = fw[0].shape[0]
    half = RW_LANES // 2
    tc = RW_TCHUNK
    assert L % tc == 0 and n_ctx % tc == 0 and fw[0].shape[1:] == (RW_HD, half)
    nblk, cblk = L // tc, n_ctx // tc

    def fmap(i):
        return (i, 0, 0)

    def bmap(i):
        return (jnp.where(i < cblk, cblk - 1 - i, nblk - 1 + cblk - i), 0, 0)

    fspec = pl.BlockSpec((tc, RW_HD, half), fmap)
    bspec = pl.BlockSpec((tc, RW_HD, half), bmap)
    fout = pl.BlockSpec((tc, RW_HD, RW_LANES), fmap)
    bout = pl.BlockSpec((tc, RW_HD, RW_LANES), bmap)
    sds = jax.ShapeDtypeStruct((L, RW_HD, RW_LANES), jnp.float32)
    return pl.pallas_call(
        functools.partial(_rwkv_scan_kernel, tc=tc),
        grid=(nblk,),
        in_specs=[fspec] * 6 + [bspec] * 6,
        out_specs=[fout, bout],
        out_shape=[sds, sds],
        scratch_shapes=[pltpu.VMEM((RW_HD, RW_HD, RW_LANES), jnp.float32),
                        pltpu.VMEM((5, RW_HD, RW_LANES), jnp.float32)],
        compiler_params=pltpu.CompilerParams(dimension_semantics=("arbitrary",),
                                             vmem_limit_bytes=VMEM_LIMIT),
        name="rwkv_scan",
    )(*fw, *bw)


def rwkv7_mix(rkv_c, rkv_l, lora_c, lora_l, lp):
    f32 = jnp.float32
    B, Lc, _ = rkv_c.shape
    T = rkv_l.shape[1]
    Lt = Lc + T
    hs = (B, Lt, RW_HEADS, RW_HD)
    rkv = jnp.concatenate([rkv_c, rkv_l], axis=1)
    lora = jnp.concatenate([lora_c, lora_l], axis=1)
    r = rkv[..., :RW_W].reshape(hs)
    k = rkv[..., RW_W:2 * RW_W].reshape(hs)
    v = rkv[..., 2 * RW_W:].reshape(hs)
    kk = k * lp['rw_k_k'].reshape(RW_HEADS, RW_HD)
    kk = kk * lax.rsqrt(jnp.sum(kk * kk, axis=-1, keepdims=True) + 1e-12)
    k_a = lp['rw_k_a'].reshape(RW_HEADS, RW_HD)
    r_k = lp['rw_r_k']
    assert 2 * B * RW_HEADS == RW_LANES
    bonus = jnp.zeros(hs, f32)
    per_dir = []
    for d in range(2):
        wd = lora[..., d * RW_DECAY_R:(d + 1) * RW_DECAY_R]
        ad = lora[..., 2 * RW_DECAY_R + d * RW_ICLR_R:2 * RW_DECAY_R + (d + 1) * RW_ICLR_R]
        w_log = -jax.nn.softplus(-(lp['rw_w0'][d] + jnp.tanh(wd) @ lp['rw_w_up'][d])) - 0.5
        decay = jnp.exp(-jnp.exp(w_log)).reshape(hs)
        a = jax.nn.sigmoid(lp['rw_a0'][d] + ad @ lp['rw_a_up'][d]).reshape(hs)
        k_d = k * (1.0 + (a - 1.0) * k_a)
        per_dir.append((r, decay, k_d, v, -kk, kk * a))
        bonus = bonus + jnp.sum(r * k_d * r_k, axis=-1, keepdims=True) * v

    def lanes(x):
        return x.transpose(1, 3, 0, 2).reshape(Lt, RW_HD, B * RW_HEADS)

    yf, yb = rwkv_scan_lanes(tuple(lanes(x) for x in per_dir[0]), tuple(lanes(x) for x in per_dir[1]), n_ctx=Lc)
    half = RW_LANES // 2
    y_sum = (yf[..., :half] + yb[..., half:]).reshape(Lt, RW_HD, B, RW_HEADS).transpose(2, 0, 3, 1)
    mu = jnp.mean(y_sum, axis=-1, keepdims=True)
    var = jnp.mean(jnp.square(y_sum - mu), axis=-1, keepdims=True)
    yn = (y_sum - mu) * lax.rsqrt(var + RW_GN_EPS)
    yn = yn * lp['rw_ln_w'].reshape(RW_HEADS, RW_HD) + lp['rw_ln_b'].reshape(RW_HEADS, RW_HD)
    g = jax.nn.sigmoid(lora[..., 2 * RW_DECAY_R + 2 * RW_ICLR_R:]) @ lp['rw_g_up']
    out = (yn + bonus).reshape(B, Lt, RW_W) * g
    return out[:, :Lc], out[:, Lc:]


def rope_tables(T):
    f32 = jnp.float32
    quarter = AT_HD // 4
    inv = ROPE_BASE ** (-jnp.arange(quarter, dtype=f32) / quarter)
    rows = T // GRID_W
    row = jnp.repeat(jnp.arange(rows), GRID_W).astype(f32)
    col = jnp.tile(jnp.arange(GRID_W), rows).astype(f32)
    out_c, out_s = [], []
    for pos in (row, col):
        ang = pos[:, None] * inv[None, :]
        c, s = jnp.cos(ang), jnp.sin(ang)
        out_c += [c, c]
        out_s += [-s, s]
    return jnp.concatenate(out_c, axis=1), jnp.concatenate(out_s, axis=1)


def _rope(x, c, s):
    lane = lax.broadcasted_iota(jnp.int32, x.shape, 1)
    first = (lane % 32) < 16
    partner = jnp.where(first, pltpu.roll(x, x.shape[1] - 16, axis=1), pltpu.roll(x, 16, axis=1))
    return x * c + partner * s


def _attn_kernel(q_ref, kp_ref, kc_ref, kn_ref, vp_ref, vc_ref, vn_ref, kx_ref, vx_ref, sink_ref,
                 cq_ref, sq_ref, ckp_ref, skp_ref, ckc_ref, skc_ref, ckn_ref, skn_ref, o_ref, *, nblk):
    i = pl.program_id(1)
    blk = ATT_BLOCK
    scale = AT_HD ** -0.5
    q = _rope(q_ref[0], cq_ref[...], sq_ref[...])
    kspan = jnp.concatenate([_rope(kp_ref[0], ckp_ref[...], skp_ref[...]),
                             _rope(kc_ref[0], ckc_ref[...], skc_ref[...]),
                             _rope(kn_ref[0], ckn_ref[...], skn_ref[...])], axis=0)
    vspan = jnp.concatenate([vp_ref[0], vc_ref[0], vn_ref[0]], axis=0)
    kx = kx_ref[0]
    vx = vx_ref[0]
    qpos = lax.broadcasted_iota(jnp.int32, (blk, 3 * blk), 0) + blk
    kpos = lax.broadcasted_iota(jnp.int32, (blk, 3 * blk), 1)
    valid = jnp.abs(qpos - kpos) <= ATT_WINDOW
    valid = valid & ((kpos >= blk) | (i > 0)) & ((kpos < 2 * blk) | (i < nblk - 1))
    valid8 = jnp.concatenate([valid] * AT_HQ, axis=0)
    nt = (((1,), (1,)), ((), ()))
    bf = jnp.bfloat16
    s_lat, s_ctx, sks = [], [], []
    for g in range(AT_HKV):
        kg = kspan[:, g * AT_HD:(g + 1) * AT_HD].astype(bf)
        kxg = kx[:, g * AT_HD:(g + 1) * AT_HD].astype(bf)
        qg = jnp.concatenate([q[:, (g * AT_G + j) * AT_HD:(g * AT_G + j + 1) * AT_HD] for j in range(AT_G)],
                             axis=0).astype(bf)
        s_lat.append(lax.dot_general(qg, kg, nt, preferred_element_type=jnp.float32))
        s_ctx.append(lax.dot_general(qg, kxg, nt, preferred_element_type=jnp.float32))
        sks += [jnp.full((blk, 1), sink_ref[g * AT_G + j], jnp.float32) for j in range(AT_G)]
    s_lat = jnp.where(valid8, jnp.concatenate(s_lat, axis=0) * scale, NEG_INF)
    s_ctx = jnp.concatenate(s_ctx, axis=0) * scale
    sk = jnp.concatenate(sks, axis=0)
    m = jnp.maximum(jnp.maximum(jnp.max(s_lat, axis=1, keepdims=True), jnp.max(s_ctx, axis=1, keepdims=True)), sk)
    p_lat = jnp.exp(s_lat - m)
    p_ctx = jnp.exp(s_ctx - m)
    den = jnp.sum(p_lat, axis=1, keepdims=True) + jnp.sum(p_ctx, axis=1, keepdims=True) + jnp.exp(sk - m)
    inv = 1.0 / den
    p_lat = (p_lat * inv).astype(bf)
    p_ctx = (p_ctx * inv).astype(bf)
    outs = []
    rows = AT_G * blk
    for g in range(AT_HKV):
        vg = vspan[:, g * AT_HD:(g + 1) * AT_HD].astype(bf)
        vxg = vx[:, g * AT_HD:(g + 1) * AT_HD].astype(bf)
        o = (jnp.dot(p_lat[g * rows:(g + 1) * rows], vg, preferred_element_type=jnp.float32)
             + jnp.dot(p_ctx[g * rows:(g + 1) * rows], vxg, preferred_element_type=jnp.float32))
        outs += [o[j * blk:(j + 1) * blk] for j in range(AT_G)]
    o_ref[0] = jnp.concatenate(outs, axis=1)


def windowed_attention(q, k, v, kx, vx, sink):
    B, T, _ = q.shape
    Lc = kx.shape[1]
    assert T % ATT_BLOCK == 0 and ATT_WINDOW == ATT_BLOCK
    nblk = T // ATT_BLOCK
    c1, s1 = rope_tables(T)
    cq, sq = jnp.tile(c1, (1, AT_HQ)), jnp.tile(s1, (1, AT_HQ))
    ck, sk = jnp.tile(c1, (1, AT_HKV)), jnp.tile(s1, (1, AT_HKV))

    def prev(b, i):
        return (b, jnp.maximum(i - 1, 0), 0)

    def cur(b, i):
        return (b, i, 0)

    def nxt(b, i):
        return (b, jnp.minimum(i + 1, nblk - 1), 0)

    def kb(f):
        return pl.BlockSpec((1, ATT_BLOCK, AT_KVW), f)

    def tb(f):
        return pl.BlockSpec((ATT_BLOCK, AT_KVW), lambda b, i: f(b, i)[1:])

    return pl.pallas_call(
        functools.partial(_attn_kernel, nblk=nblk),
        grid=(B, nblk),
        in_specs=[pl.BlockSpec((1, ATT_BLOCK, AT_W), cur), kb(prev), kb(cur), kb(nxt), kb(prev), kb(cur), kb(nxt),
                  pl.BlockSpec((1, Lc, AT_KVW), lambda b, i: (b, 0, 0)),
                  pl.BlockSpec((1, Lc, AT_KVW), lambda b, i: (b, 0, 0)),
                  pl.BlockSpec(memory_space=pltpu.SMEM),
                  pl.BlockSpec((ATT_BLOCK, AT_W), lambda b, i: (i, 0)),
                  pl.BlockSpec((ATT_BLOCK, AT_W), lambda b, i: (i, 0)),
                  tb(prev), tb(prev), tb(cur), tb(cur), tb(nxt), tb(nxt)],
        out_specs=pl.BlockSpec((1, ATT_BLOCK, AT_W), cur),
        out_shape=jax.ShapeDtypeStruct((B, T, AT_W), jnp.float32),
        compiler_params=pltpu.CompilerParams(dimension_semantics=("parallel", "arbitrary"),
                                             vmem_limit_bytes=VMEM_LIMIT),
        name="window_attention",
    )(q, k, k, k, v, v, v, kx, vx, sink, cq, sq, ck, sk, ck, sk, ck, sk)


def context_attention(qc, kc, vc, sink):
    B, Lc = qc.shape[:2]
    scale = AT_HD ** -0.5
    s = jnp.einsum('bqgjd,bkgd->bgjqk', qc, kc) * scale
    sink_b = jnp.broadcast_to(sink[None, :, :, None, None], (B, AT_HKV, AT_G, Lc, 1))
    p = jax.nn.softmax(jnp.concatenate([s, sink_b], axis=-1), axis=-1)[..., :Lc]
    return jnp.einsum('bgjqk,bkgd->bqgjd', p, vc).reshape(B, Lc, AT_W)


INPROJ_ROWS = 512
SUBLANE = 8


def _inproj_kernel(h_ref, hp_ref, hn_ref, w_ref, cw_ref, cb_ref, x0_ref, u_ref, rkv_ref, lora_ref, q_ref, k_ref, v_ref,
                   *, tiles_per_seq):
    i = pl.program_id(0)
    tm = h_ref.shape[0]
    bf = jnp.bfloat16
    z = jnp.dot(h_ref[...].astype(bf), w_ref[...], preferred_element_type=jnp.float32)
    wc = w_ref[:, :CONV_COLS]
    zp = jnp.dot(hp_ref[...].astype(bf), wc, preferred_element_type=jnp.float32)[SUBLANE - 1:SUBLANE]
    zn = jnp.dot(hn_ref[...].astype(bf), wc, preferred_element_type=jnp.float32)[0:1]
    pos = i % tiles_per_seq
    zp = jnp.where(pos == 0, 0.0, zp)
    zn = jnp.where(pos == tiles_per_seq - 1, 0.0, zn)
    zc = z[:, :CONV_COLS]
    row = lax.broadcasted_iota(jnp.int32, zc.shape, 0)
    up = jnp.where(row == 0, zp, pltpu.roll(zc, 1, 0))
    dn = jnp.where(row == tm - 1, zn, pltpu.roll(zc, tm - 1, 0))
    s = up * cw_ref[0:1] + zc * cw_ref[1:2] + dn * cw_ref[2:3] + cb_ref[...]
    x0_ref[...] = s[:, :HY_CH]
    u_ref[...] = s[:, HY_CH:2 * HY_CH] * s[:, 2 * HY_CH:OFF_RK]
    rkv_ref[...] = s[:, OFF_RK:]
    lora_ref[...] = z[:, OFF_LORA:OFF_AT]
    q_ref[...] = z[:, OFF_AT:OFF_AT + AT_W]
    k_ref[...] = z[:, OFF_AT + AT_W:OFF_AT + AT_W + AT_KVW]
    v_ref[...] = z[:, OFF_AT + AT_W + AT_KVW:]


def in_proj(h3, w_in, conv_w, conv_b):
    nseq, seq_len, d = h3.shape
    n = nseq * seq_len
    tm = min(INPROJ_ROWS, seq_len)
    assert seq_len % tm == 0 and w_in.shape == (d, OFF_AT + AT_W + 2 * AT_KVW)
    tps = seq_len // tm
    nb8 = n // SUBLANE
    r8 = tm // SUBLANE
    h = h3.reshape(n, d)

    def out(w):
        return pl.BlockSpec((tm, w), lambda i: (i, 0))

    widths = (HY_CH, HY_CH, 3 * RW_W, RW_LORA_W, AT_W, AT_KVW, AT_KVW)
    outs = pl.pallas_call(
        functools.partial(_inproj_kernel, tiles_per_seq=tps),
        grid=(n // tm,),
        in_specs=[pl.BlockSpec((tm, d), lambda i: (i, 0)),
                  pl.BlockSpec((SUBLANE, d), lambda i: (jnp.maximum(i * r8 - 1, 0), 0)),
                  pl.BlockSpec((SUBLANE, d), lambda i: (jnp.minimum((i + 1) * r8, nb8 - 1), 0)),
                  pl.BlockSpec(w_in.shape, lambda i: (0, 0)),
                  pl.BlockSpec((3, CONV_COLS), lambda i: (0, 0)),
                  pl.BlockSpec((1, CONV_COLS), lambda i: (0, 0))],
        out_specs=[out(w) for w in widths],
        out_shape=[jax.ShapeDtypeStruct((n, w), jnp.float32) for w in widths],
        compiler_params=pltpu.CompilerParams(dimension_semantics=("parallel",), vmem_limit_bytes=VMEM_LIMIT),
        name="in_proj",
    )(h, h, h, w_in.astype(jnp.bfloat16), conv_w, conv_b.reshape(1, -1))
    return tuple(o.reshape(nseq, seq_len, o.shape[-1]) for o in outs)


def token_mixer(hl, hc, lp, need_ctx):
    B, T, _ = hl.shape
    Lc = hc.shape[1]
    x0_l, u_l, rkv_l, lora_l, q_l, k_l, v_l = in_proj(hl, lp['w_in'], lp['conv_w'], lp['conv_b'])
    x0_c, u_c, rkv_c, lora_c, q_c, k_c, v_c = in_proj(hc, lp['w_in'], lp['conv_w'], lp['conv_b'])
    filt_p = (lp['hy_w1'], lp['hy_b1'], lp['hy_w2'], lp['hy_b2'], lp['hy_w3'], lp['hy_b3'], lp['hy_freq'])
    hy_l = x0_l * (hyena_conv(u_l, hyena_filter(T, *filt_p)) + u_l * lp['hy_bias'])
    rw_c, rw_l = rwkv7_mix(rkv_c, rkv_l, lora_c, lora_l, lp)
    at_l = windowed_attention(q_l, k_l, v_l, k_c, v_c, lp['attn_sink'])
    parts_l = (hy_l, rw_l, at_l)
    if not need_ctx:
        return parts_l, None
    hy_c = x0_c * (hyena_conv(u_c, hyena_filter(Lc, *filt_p)) + u_c * lp['hy_bias'])
    at_c = context_attention(q_c.reshape(B, Lc, AT_HKV, AT_G, AT_HD), k_c.reshape(B, Lc, AT_HKV, AT_HD),
                             v_c.reshape(B, Lc, AT_HKV, AT_HD), lp['attn_sink'].reshape(AT_HKV, AT_G))
    return parts_l, (hy_c, rw_c, at_c)


NORM_EPS = 1e-6
MIXOUT_ROWS = 512


def _rms(x, g):
    return (x * lax.rsqrt(jnp.mean(x * x, axis=-1, keepdims=True) + NORM_EPS)) * g


def _mixout_kernel(hy_ref, rw_ref, at_ref, x_ref, g1_ref, sc2_ref, sh2_ref, hyg_ref, atg_ref, n2g_ref, w_ref,
                   xo_ref, h2_ref):
    mix = jnp.concatenate([_rms(hy_ref[...], hyg_ref[...]), rw_ref[...], _rms(at_ref[...], atg_ref[...])], axis=1)
    m = jnp.dot(mix.astype(jnp.bfloat16), w_ref[...], preferred_element_type=jnp.float32)
    xn = x_ref[...] + g1_ref[0] * m
    xo_ref[...] = xn
    h2_ref[...] = (_rms(xn, n2g_ref[...]) * (1.0 + sc2_ref[0]) + sh2_ref[0]).astype(h2_ref.dtype)


def mix_out(parts, x, gate1, scale2, shift2, hy_g, at_g, norm2_g, w_out, rows_per_mod):
    hy, rw, at = parts
    n, d = x.shape
    tm = MIXOUT_ROWS
    assert n % tm == 0 and rows_per_mod % tm == 0
    tpm = rows_per_mod // tm

    def row(w):
        return pl.BlockSpec((tm, w), lambda i: (i, 0))

    def vec(w):
        return pl.BlockSpec((1, w), lambda i: (0, 0))

    mod = pl.BlockSpec((1, 1, d), lambda i: (i // tpm, 0, 0))
    return pl.pallas_call(
        _mixout_kernel,
        grid=(n // tm,),
        in_specs=[row(hy.shape[1]), row(rw.shape[1]), row(at.shape[1]), row(d), mod, mod, mod,
                  vec(hy.shape[1]), vec(at.shape[1]), vec(d), pl.BlockSpec(w_out.shape, lambda i: (0, 0))],
        out_specs=[row(d), row(d)],
        out_shape=[jax.ShapeDtypeStruct((n, d), jnp.float32), jax.ShapeDtypeStruct((n, d), jnp.bfloat16)],
        compiler_params=pltpu.CompilerParams(dimension_semantics=("parallel",), vmem_limit_bytes=VMEM_LIMIT),
        name="mix_out",
    )(hy, rw, at, x, gate1.reshape(-1, 1, d), scale2.reshape(-1, 1, d), shift2.reshape(-1, 1, d),
      hy_g.reshape(1, -1), at_g.reshape(1, -1), norm2_g.reshape(1, -1), w_out.astype(jnp.bfloat16))


def _top16_rows(s, n_rows):
    rio = lax.broadcasted_iota(jnp.int32, s.shape, 0)
    vals, idxs = [], []
    for _ in range(PEER_TOPK):
        m = jnp.max(s, axis=0, keepdims=True)
        idx = jnp.min(jnp.where(s == m, rio, n_rows), axis=0, keepdims=True)
        vals.append(m)
        idxs.append(idx)
        s = jnp.where(rio == idx, -jnp.inf, s)
    return jnp.concatenate(vals, axis=0), jnp.concatenate(idxs, axis=0)


LANE = 128
ROUTE_UNROLL = 4


def _cand_layout():
    rows = [(0, b) for b in range(PEER_TOPK)]
    for a in range(1, 8):
        rows += [(a, b) for b in range(8)]
    rows += [(a, 0) for a in range(8, PEER_TOPK)]
    a = np.array([r[0] for r in rows])
    b = np.array([r[1] for r in rows])
    return a, b, (a + 1) * (b + 1) <= PEER_TOPK


def cand_table():
    a, b, valid = _cand_layout()
    cio = np.where(valid, a * PEER_TOPK + b, PEER_TOPK * PEER_TOPK).astype(np.int32)
    return jnp.asarray(np.broadcast_to(cio[:, None], (cio.shape[0], LANE)))


def _route_kernel(x_ref, wq_ref, k1_ref, k2_ref, cio_ref, i1_ref, i2_ref, g_ref, q_ref, *, tt):
    q = jnp.dot(x_ref[...].astype(jnp.bfloat16), wq_ref[...], preferred_element_type=jnp.float32)
    q_ref[...] = q.astype(jnp.bfloat16)
    nt = (((1,), (1,)), ((), ()))
    cio = cio_ref[...]
    ncand = PEER_TOPK * PEER_TOPK

    def chunk(c, carry):
        t0 = pl.multiple_of(c * LANE, LANE)
        qc = q_ref[pl.ds(t0, LANE), :]
        s1 = lax.dot_general(k1_ref[0], qc[:, :PEER_DQ // 2], nt, preferred_element_type=jnp.float32)
        s2 = lax.dot_general(k2_ref[0], qc[:, PEER_DQ // 2:], nt, preferred_element_type=jnp.float32)
        v1, j1 = _top16_rows(s1, PEER_NKEYS)
        v2, j2 = _top16_rows(s2, PEER_NKEYS)
        j1 = j1 * PEER_NKEYS
        cand = jnp.concatenate([v1[0:1] + v2] + [v1[a:a + 1] + v2[0:8] for a in range(1, 8)]
                               + [v1[8:16] + v2[0:1]], axis=0)
        ecand = jnp.concatenate([j1[0:1] + j2] + [j1[a:a + 1] + j2[0:8] for a in range(1, 8)]
                                + [j1[8:16] + j2[0:1]], axis=0)
        cand = jnp.where(cio < ncand, cand, -jnp.inf)
        scs, es = [], []
        for _ in range(PEER_TOPK):
            m = jnp.max(cand, axis=0, keepdims=True)
            cidx = jnp.min(jnp.where(cand == m, cio, ncand), axis=0, keepdims=True)
            sel = cio == cidx
            es.append(jnp.max(jnp.where(sel, ecand, -1), axis=0, keepdims=True))
            scs.append(m)
            cand = jnp.where(sel, -jnp.inf, cand)
        sc = jnp.concatenate(scs, axis=0)
        e = jnp.concatenate(es, axis=0)
        p = jnp.exp(sc - sc[0:1])
        i1_ref[0, :, pl.ds(t0, LANE)] = e >> 7
        i2_ref[0, :, pl.ds(t0, LANE)] = e & (PEER_NKEYS - 1)
        g_ref[0, :, pl.ds(t0, LANE)] = p / jnp.sum(p, axis=0, keepdims=True)
        return carry

    lax.fori_loop(0, tt // LANE, chunk, 0, unroll=ROUTE_UNROLL)


def peer_route(h, wq_bf, k1_bf, k2_bf, tt):
    n, d = h.shape
    assert n % tt == 0 and tt % (ROUTE_UNROLL * LANE) == 0
    cio = cand_table()
    out_sds = [jax.ShapeDtypeStruct((PEER_HEADS, PEER_TOPK, n), jnp.int32),
               jax.ShapeDtypeStruct((PEER_HEADS, PEER_TOPK, n), jnp.int32),
               jax.ShapeDtypeStruct((PEER_HEADS, PEER_TOPK, n), jnp.float32)]
    ospec = pl.BlockSpec((1, PEER_TOPK, tt), lambda i, hh: (hh, 0, i))
    return pl.pallas_call(
        functools.partial(_route_kernel, tt=tt),
        grid=(n // tt, PEER_HEADS),
        in_specs=[pl.BlockSpec((tt, d), lambda i, hh: (i, 0)),
                  pl.BlockSpec((d, PEER_DQ), lambda i, hh: (0, hh)),
                  pl.BlockSpec((1, PEER_NKEYS, PEER_DQ // 2), lambda i, hh: (hh, 0, 0)),
                  pl.BlockSpec((1, PEER_NKEYS, PEER_DQ // 2), lambda i, hh: (hh, 0, 0)),
                  pl.BlockSpec(cio.shape, lambda i, hh: (0, 0))],
        out_specs=[ospec, ospec, ospec],
        out_shape=out_sds,
        scratch_shapes=[pltpu.VMEM((tt, PEER_DQ), jnp.bfloat16)],
        compiler_params=pltpu.CompilerParams(dimension_semantics=("parallel", "arbitrary"),
                                             vmem_limit_bytes=VMEM_LIMIT),
        name="peer_route",
    )(h, wq_bf, k1_bf, k2_bf, cio)


def _gelu(x):
    return 0.5 * x * (1.0 + lax.erf(x * (2.0 ** -0.5)))


PEER_BUILD_GROUP = 32
PEER_VMEM_LIMIT = 56 * 1024 * 1024


def _expert_kernel(x_ref, i1_ref, i2_ref, g_ref, ut_ref, v_ref, res_ref, g2_ref, ng_ref, nsc_ref, nsh_ref,
                   o_ref, hn_ref, w_ref, rows_ref, xb_ref, *, tt, eb):
    j = pl.program_id(1)
    nk = PEER_NKEYS
    grp = PEER_BUILD_GROUP

    @pl.when(j == 0)
    def _build():
        xb_ref[...] = x_ref[...].astype(jnp.bfloat16)
        o_ref[...] = jnp.zeros_like(o_ref)
        rows_ref[0] = i1_ref[...].reshape(nk, tt).astype(jnp.float32).T
        rows_ref[1] = i2_ref[...].reshape(nk, tt).astype(jnp.float32).T
        rows_ref[2] = g_ref[...].reshape(nk, tt).T
        kio = lax.broadcasted_iota(jnp.int32, (nk, nk), 0).astype(jnp.float32)
        nt = (((1,), (1,)), ((), ()))

        def body(gidx, carry):
            t0 = pl.multiple_of(gidx * grp, grp)
            tiles = []
            for s in range(grp):
                i1row = jnp.broadcast_to(rows_ref[0, pl.ds(t0 + s, 1), :], (nk, nk))
                i2row = jnp.broadcast_to(rows_ref[1, pl.ds(t0 + s, 1), :], (nk, nk))
                grow = jnp.broadcast_to(rows_ref[2, pl.ds(t0 + s, 1), :], (nk, nk))
                m1 = jnp.where(i1row == kio, grow, 0.0).astype(jnp.bfloat16)
                p2 = jnp.where(i2row == kio, 1.0, 0.0).astype(jnp.bfloat16)
                tiles.append(lax.dot_general(m1, p2, nt, preferred_element_type=jnp.float32))
            w_ref[:, pl.ds(t0, grp), :] = jnp.swapaxes(jnp.stack(tiles, axis=0), 0, 1).astype(jnp.bfloat16)
            return carry

        lax.fori_loop(0, tt // grp, body, 0)

    a = jnp.dot(xb_ref[...], ut_ref[...], preferred_element_type=jnp.float32)
    nrow = eb // nk
    wj = jnp.concatenate([w_ref[j * nrow + r] for r in range(nrow)], axis=1)
    cmat = (wj.astype(jnp.float32) * _gelu(a)).astype(jnp.bfloat16)
    o_ref[...] += jnp.dot(cmat, v_ref[...], preferred_element_type=jnp.float32)

    @pl.when(j == pl.num_programs(1) - 1)
    def _finish():
        xn = res_ref[...] + g2_ref[0] * o_ref[...]
        o_ref[...] = xn
        hn_ref[...] = _rms(xn, ng_ref[...]) * (1.0 + nsc_ref[0]) + nsh_ref[0]


def peer_experts(h, i1, i2, g, ut_bf, v_bf, res, gate2, next_g, next_scale, next_shift, rows_per_mod, tt, eb):
    n, d = h.shape
    nexp = v_bf.shape[0]
    nblk = nexp // eb
    assert n % tt == 0 and nexp % eb == 0 and nblk >= 2 and nexp == PEER_NKEYS * PEER_NKEYS
    assert rows_per_mod % tt == 0
    tpm = rows_per_mod // tt
    rspec = pl.BlockSpec((PEER_HEADS, PEER_TOPK, tt), lambda i, j: (0, 0, i))
    tile = pl.BlockSpec((tt, d), lambda i, j: (i, 0))
    mod = pl.BlockSpec((1, 1, d), lambda i, j: (i // tpm, 0, 0))
    sds = jax.ShapeDtypeStruct((n, d), jnp.float32)
    return pl.pallas_call(
        functools.partial(_expert_kernel, tt=tt, eb=eb),
        grid=(n // tt, nblk),
        in_specs=[tile, rspec, rspec, rspec,
                  pl.BlockSpec((d, eb), lambda i, j: (0, j)),
                  pl.BlockSpec((eb, d), lambda i, j: (j, 0)),
                  tile, mod, pl.BlockSpec((1, d), lambda i, j: (0, 0)), mod, mod],
        out_specs=[tile, tile],
        out_shape=[sds, sds],
        scratch_shapes=[pltpu.VMEM((PEER_NKEYS, tt, PEER_NKEYS), jnp.bfloat16),
                        pltpu.VMEM((3, tt, PEER_NKEYS), jnp.float32),
                        pltpu.VMEM((tt, d), jnp.bfloat16)],
        compiler_params=pltpu.CompilerParams(dimension_semantics=("parallel", "arbitrary"),
                                             vmem_limit_bytes=PEER_VMEM_LIMIT),
        name="peer_experts",
    )(h, i1, i2, g, ut_bf, v_bf, res, gate2.reshape(-1, 1, d), next_g.reshape(1, d),
      next_scale.reshape(-1, 1, d), next_shift.reshape(-1, 1, d))


PEER_ROUTE_TILE = 512
PEER_TOKEN_TILE = 512
PEER_EXPERT_BLOCK = 1024


def peer_ffn(h, wq, k1, k2, u, v, res, gate2, next_g, next_scale, next_shift, rows_per_mod):
    bf = jnp.bfloat16
    i1, i2, g = peer_route(h, wq.astype(bf), k1.astype(bf), k2.astype(bf), PEER_ROUTE_TILE)
    return peer_experts(h, i1, i2, g, u.T.astype(bf), v.astype(bf), res, gate2, next_g, next_scale, next_shift,
                        rows_per_mod, PEER_TOKEN_TILE, PEER_EXPERT_BLOCK)


def kernel(x, c, ctx, c_ctx, mod_w, mod_b, norm1_g, norm2_g, w_in, conv_w, conv_b, hy_w1, hy_b1, hy_w2, hy_b2, hy_w3, hy_b3, hy_freq, hy_bias, rw_w0, rw_w_up, rw_a0, rw_a_up, rw_g_up, rw_k_k, rw_k_a, rw_r_k, rw_ln_w, rw_ln_b, attn_sink, hy_out_g, at_out_g, w_out, peer_wq, peer_k1, peer_k2, peer_u, peer_v, final_g):
    B, T, D = x.shape
    Lc = ctx.shape[1]
    c_s = jax.nn.silu(c)
    cc_s = jax.nn.silu(c_ctx)
    mods = [jnp.split(c_s @ mod_w[i] + mod_b[i], 6, axis=-1) for i in range(DEPTH)]
    cmods = [jnp.split(cc_s @ mod_w[i] + mod_b[i], 6, axis=-1) for i in range(DEPTH)]
    xl, xc = x.reshape(B * T, D), ctx.reshape(B * Lc, D)
    hl = rmsnorm(x, norm1_g[0]) * (1.0 + mods[0][1][:, None, :]) + mods[0][0][:, None, :]
    hc = rmsnorm(ctx, norm1_g[0]) * (1.0 + cmods[0][1]) + cmods[0][0]
    zeros = jnp.zeros((B, D), jnp.float32)
    for i in range(DEPTH):
        last = i == DEPTH - 1
        lp = dict(w_in=w_in[i], conv_w=conv_w[i], conv_b=conv_b[i],
                  hy_w1=hy_w1[i], hy_b1=hy_b1[i], hy_w2=hy_w2[i], hy_b2=hy_b2[i],
                  hy_w3=hy_w3[i], hy_b3=hy_b3[i], hy_freq=hy_freq[i], hy_bias=hy_bias[i],
                  rw_w0=rw_w0[i], rw_w_up=rw_w_up[i], rw_a0=rw_a0[i], rw_a_up=rw_a_up[i],
                  rw_g_up=rw_g_up[i], rw_k_k=rw_k_k[i], rw_k_a=rw_k_a[i], rw_r_k=rw_r_k[i],
                  rw_ln_w=rw_ln_w[i], rw_ln_b=rw_ln_b[i], attn_sink=attn_sink[i],
                  hy_out_g=hy_out_g[i], at_out_g=at_out_g[i], w_out=w_out[i])
        _, _, g1, sh2, sc2, g2 = mods[i]
        _, _, cg1, csh2, csc2, cg2 = cmods[i]
        parts_l, parts_c = token_mixer(hl, hc, lp, not last)
        peer_w = (peer_wq[i], peer_k1[i], peer_k2[i], peer_u[i], peer_v[i])
        out_w = (lp['hy_out_g'], lp['at_out_g'], norm2_g[i], lp['w_out'])
        flat = lambda parts: tuple(p.reshape(-1, p.shape[-1]) for p in parts)
        if last:
            next_l = (final_g, zeros, zeros)
        else:
            next_l = (norm1_g[i + 1], mods[i + 1][1], mods[i + 1][0])
        xl2, h2l = mix_out(flat(parts_l), xl, g1, sc2, sh2, *out_w, rows_per_mod=T)
        xl, hl = peer_ffn(h2l, *peer_w, xl2, g2, *next_l, rows_per_mod=T)
        hl = hl.reshape(B, T, D)
        if not last:
            xc2, h2c = mix_out(flat(parts_c), xc, cg1[None], csc2[None], csh2[None], *out_w, rows_per_mod=B * Lc)
            xc, hc = peer_ffn(h2c, *peer_w, xc2, cg2[None], norm1_g[i + 1], cmods[i + 1][1][None],
                              cmods[i + 1][0][None], rows_per_mod=B * Lc)
            hc = hc.reshape(B, Lc, D)
    return hl
```
